```python
import math
import functools
import jax
import jax.numpy as jnp
from jax import lax
import numpy as np

D_MODEL = 1024
BATCH = 4
SEQ = 8192
DEPTH = 2
DEC_BATCH = 128
DEC_SEQ = 4
PAST_LEN = 16384
PAGE_SIZE = 128

BRANCH_W = D_MODEL // 2
N_BRANCH = 3
D_CONV = BRANCH_W
CONV_W = 3
D_SSM = BRANCH_W
SSM_GROUP = 16
SSM_GROUPS = D_SSM // SSM_GROUP
SSM_STATE = 64
N_HEADS = 8
NOPE_DIM = 64
ROPE_DIM = 32
V_DIM = BRANCH_W // N_HEADS
KV_LORA = D_MODEL // 4
ROPE_THETA = 10000.0
ATTN_SCALE = (NOPE_DIM + ROPE_DIM) ** -0.5
Q_BLOCK = 128
NEG_INF = -1e30
PEER_HEADS = 8
PEER_KEYS = 128
PEER_EXPERTS = PEER_KEYS * PEER_KEYS
PEER_TOPK = 16
PEER_DK = 64
PEER_BLOCK = 128
EPS = 1e-6
IN_SIZES = (D_CONV, D_CONV, D_CONV, D_SSM, N_HEADS * (NOPE_DIM + ROPE_DIM), KV_LORA, ROPE_DIM, N_BRANCH * D_MODEL)
N_IN = sum(IN_SIZES)

kernel_name = "hybrid_conv_s5_mla_peer_step"


def rms_norm(x, g):
    xf = x.astype(jnp.float32)
    y = xf * lax.rsqrt(jnp.mean(xf * xf, axis=-1, keepdims=True) + EPS)
    return (y * g.astype(jnp.float32)).astype(x.dtype)


def rope(x, pos):
    half = ROPE_DIM // 2
    inv = ROPE_THETA ** (-jnp.arange(half, dtype=jnp.float32) / half)
    ang = pos.astype(jnp.float32)[:, None] * inv[None, :]
    cos = jnp.cos(ang)[:, None, :]
    sin = jnp.sin(ang)[:, None, :]
    xf = x.astype(jnp.float32)
    x1, x2 = xf[..., :half], xf[..., half:]
    return jnp.concatenate([x1 * cos - x2 * sin, x2 * cos + x1 * sin], axis=-1).astype(x.dtype)


def split_in(z):
    parts, off = [], 0
    for n in IN_SIZES:
        parts.append(z[..., off:off + n])
        off += n
    return parts


def short_conv_mixer(x_in, b_gate, c_gate, hist, conv_w):
    v = c_gate * x_in
    zc = jnp.concatenate([hist.astype(v.dtype), v], axis=1)
    L = v.shape[1]
    y = zc[:, 0:L] * conv_w[0] + zc[:, 1:L + 1] * conv_w[1] + zc[:, 2:L + 2] * conv_w[2]
    return b_gate * y, zc[:, L:]


def _cmul(ar, ai, br, bi):
    return ar * br - ai * bi, ar * bi + ai * br


def s5_mixer(u, h0_re, h0_im, lam_re, lam_im, log_dt, b_re, b_im, c_re, c_im, d, w_glu, b_glu):
    f32 = jnp.float32
    bsz, L, _ = u.shape
    lr, li = lam_re.astype(f32), lam_im.astype(f32)
    dt = jnp.exp(log_dt.astype(f32))[:, None]
    mag = jnp.exp(lr * dt)
    a_re, a_im = mag * jnp.cos(li * dt), mag * jnp.sin(li * dt)
    den = lr * lr + li * li
    f_re = ((a_re - 1.0) * lr + a_im * li) / den
    f_im = (a_im * lr - (a_re - 1.0) * li) / den
    bb_re, bb_im = _cmul(f_re[..., None], f_im[..., None], b_re.astype(f32), b_im.astype(f32))
    ug = u.astype(f32).reshape(bsz, L, SSM_GROUPS, SSM_GROUP)
    bu_re = jnp.einsum('blgc,gpc->blgp', ug, bb_re)
    bu_im = jnp.einsum('blgc,gpc->blgp', ug, bb_im)
    shape = (1, L, SSM_GROUPS, SSM_STATE)
    aa_re = jnp.broadcast_to(a_re, shape)
    aa_im = jnp.broadcast_to(a_im, shape)

    def combine(e1, e2):
        a1r, a1i, b1r, b1i = e1
        a2r, a2i, b2r, b2i = e2
        ar, ai = _cmul(a2r, a2i, a1r, a1i)
        br, bi = _cmul(a2r, a2i, b1r, b1i)
        return ar, ai, br + b2r, bi + b2i

    A_re, A_im, H_re, H_im = lax.associative_scan(combine, (aa_re, aa_im, bu_re, bu_im), axis=1)
    p_re, p_im = _cmul(A_re, A_im, h0_re.astype(f32)[:, None], h0_im.astype(f32)[:, None])
    h_re = H_re + p_re
    h_im = H_im + p_im
    y = jnp.einsum('blgp,gcp->blgc', h_re, c_re.astype(f32)) - jnp.einsum('blgp,gcp->blgc', h_im, c_im.astype(f32))
    y = y.reshape(bsz, L, D_SSM) + d.astype(f32) * u.astype(f32)
    zg = jax.nn.gelu(y)
    o = zg * jax.nn.sigmoid(zg @ w_glu.astype(f32) + b_glu.astype(f32))
    return o.astype(u.dtype), h_re[:, -1], h_im[:, -1]


def mla_project(q_raw, ckv_raw, kr_raw, pos, g_kv, g_qn, g_qr, g_kr):
    bsz, L, _ = q_raw.shape
    q = q_raw.reshape(bsz, L, N_HEADS, NOPE_DIM + ROPE_DIM)
    q_nope = rms_norm(q[..., :NOPE_DIM], g_qn)
    q_rope = rope(rms_norm(q[..., NOPE_DIM:], g_qr), pos)
    ckv = rms_norm(ckv_raw, g_kv)
    k_rope = rope(rms_norm(kr_raw, g_kr)[:, :, None, :], pos)[:, :, 0, :]
    return q_nope, q_rope, ckv, k_rope


def mla_keys(ckv, w_uk, w_uv, g_kn):
    k_nope = rms_norm(jnp.einsum('btc,chd->bthd', ckv, w_uk), g_kn)
    v = jnp.einsum('btc,chd->bthd', ckv, w_uv)
    return k_nope, v


def attend(q_nope, q_rope, k_nope, k_rope, v, q_pos, k_pos):
    s = (jnp.einsum('bqhd,bkhd->bhqk', q_nope, k_nope).astype(jnp.float32)
         + jnp.einsum('bqhr,bkr->bhqk', q_rope, k_rope).astype(jnp.float32)) * ATTN_SCALE
    mask = k_pos[None, :] <= q_pos[:, None]
    s = jnp.where(mask[None, None], s, NEG_INF)
    p = jax.nn.softmax(s, axis=-1).astype(v.dtype)
    return jnp.einsum('bhqk,bkhd->bqhd', p, v)


def mla_prompt(q_nope, q_rope, ckv, k_rope, pos, w_uk, w_uv, g_kn):
    bsz, L = ckv.shape[0], ckv.shape[1]
    k_nope, v = mla_keys(ckv, w_uk, w_uv, g_kn)
    nb = L // Q_BLOCK

    def to_blocks(t):
        return jnp.moveaxis(t.reshape((bsz, nb, Q_BLOCK) + t.shape[2:]), 1, 0)

    def block(args):
        qn, qr, qp = args
        return attend(qn, qr, k_nope, k_rope, v, qp, pos)

    o = lax.map(block, (to_blocks(q_nope), to_blocks(q_rope), pos.reshape(nb, Q_BLOCK)))
    return jnp.moveaxis(o, 0, 1).reshape(bsz, L, N_HEADS * V_DIM)


def mla_sample(q_nope, q_rope, ckv, k_rope, pos, w_uk, w_uv, g_kn, ckv_pool, kr_pool, layer, page_table):
    n_pages = page_table.shape[1]
    past = n_pages * PAGE_SIZE
    k_pos = jnp.concatenate([jnp.arange(past, dtype=jnp.int32), pos])

    def one(args):
        pages, qn, qr, c_new, r_new = args
        c_past = ckv_pool[layer, pages].reshape(past, KV_LORA).astype(c_new.dtype)
        r_past = kr_pool[layer, pages].reshape(past, ROPE_DIM).astype(r_new.dtype)
        c = jnp.concatenate([c_past, c_new], axis=0)[None]
        r = jnp.concatenate([r_past, r_new], axis=0)[None]
        k_nope, v = mla_keys(c, w_uk, w_uv, g_kn)
        return attend(qn[None], qr[None], k_nope, r, v, pos, k_pos)[0]

    o = lax.map(one, (page_table, q_nope, q_rope, ckv, k_rope))
    return o.reshape(o.shape[0], o.shape[1], N_HEADS * V_DIM)


def peer_tokens(t, w_pq, keys, u_tab, v_tab):
    T = t.shape[0]
    q = (t @ w_pq).reshape(T, PEER_HEADS, 2, PEER_DK)
    s = jnp.einsum('thsd,hsnd->thsn', q, keys).astype(jnp.float32)
    v1, i1 = lax.top_k(s[:, :, 0], PEER_TOPK)
    v2, i2 = lax.top_k(s[:, :, 1], PEER_TOPK)
    cand = (v1[..., :, None] + v2[..., None, :]).reshape(T, PEER_HEADS, PEER_TOPK * PEER_TOPK)
    sc, ci = lax.top_k(cand, PEER_TOPK)
    e = (jnp.take_along_axis(i1, ci // PEER_TOPK, axis=-1) * PEER_KEYS
         + jnp.take_along_axis(i2, ci % PEER_TOPK, axis=-1))
    g = jax.nn.softmax(sc, axis=-1).astype(t.dtype)
    a = jnp.einsum('td,thkd->thk', t, u_tab[e])
    return jnp.einsum('thk,thkd->td', g * jax.nn.gelu(a), v_tab[e])


def peer_ffn(h, blk, w_pq, keys, u_tab, v_tab):
    shp = h.shape
    hb = h.reshape(-1, blk, D_MODEL)
    out = lax.map(lambda t: peer_tokens(t, w_pq, keys, u_tab, v_tab), hb)
    return out.reshape(shp)


def trunk_layer(x, pos, conv_hist, h0_re, h0_im, attn_fn, peer_blk, lp):
    bsz, L, _ = x.shape
    h = rms_norm(x, lp['g_mix'])
    z = h @ lp['w_in']
    xc, bc, cc, us, q_raw, ckv_raw, kr_raw, gate_raw = split_in(z)
    o_conv, new_hist = short_conv_mixer(xc, bc, cc, conv_hist, lp['conv_w'])
    o_ssm, hT_re, hT_im = s5_mixer(us, h0_re, h0_im, lp['ssm_lam_re'], lp['ssm_lam_im'], lp['ssm_log_dt'],
                                   lp['ssm_b_re'], lp['ssm_b_im'], lp['ssm_c_re'], lp['ssm_c_im'],
                                   lp['ssm_d'], lp['w_glu'], lp['b_glu'])
    q_nope, q_rope, ckv, k_rope = mla_project(q_raw, ckv_raw, kr_raw, pos, lp['g_kv'], lp['g_qn'], lp['g_qr'], lp['g_kr'])
    o_att = attn_fn(q_nope, q_rope, ckv, k_rope)
    branches = jnp.stack([o_conv, o_ssm, o_att], axis=2)
    proj = jnp.einsum('blnw,nwd->blnd', branches, lp['w_br'])
    gates = jax.nn.sigmoid(gate_raw.reshape(bsz, L, N_BRANCH, D_MODEL))
    merged = jnp.einsum('blnd,blnd->bld', gates, proj)
    x = x + merged @ lp['w_o']
    h2 = rms_norm(x, lp['g_ffn'])
    x = x + peer_ffn(h2, peer_blk, lp['w_pq'], lp['peer_keys'], lp['peer_u'], lp['peer_v'])
    return x, (new_hist, hT_re, hT_im, ckv, k_rope)


def setup_inputs(seed: int = 0) -> dict:
    key = jax.random.key(seed)
    keys = iter(jax.random.split(key, 48))
    f32 = jnp.float32

    def nrm(shape, scale):
        return jax.random.normal(next(keys), shape, f32) * scale

    def gain(n):
        return 1.0 + nrm((DEPTH, n), 0.01)

    n_pages = PAST_LEN // PAGE_SIZE
    n_used = DEC_BATCH * n_pages
    n_phys = n_used + max(1, n_used // 4)
    page_table = jax.random.permutation(next(keys), n_phys)[:n_used].reshape(DEC_BATCH, n_pages).astype(jnp.int32)
    lam_im0 = math.pi * jnp.arange(SSM_STATE, dtype=f32)
    return {
        'x_prompt': nrm((BATCH, SEQ, D_MODEL), 1.0),
        'x_sample': nrm((DEC_BATCH, DEC_SEQ, D_MODEL), 1.0),
        'cache_conv': nrm((DEPTH, DEC_BATCH, CONV_W - 1, D_CONV), 1.0),
        'state_ssm_re': nrm((DEPTH, DEC_BATCH, SSM_GROUPS, SSM_STATE), 0.1),
        'state_ssm_im': nrm((DEPTH, DEC_BATCH, SSM_GROUPS, SSM_STATE), 0.1),
        'cache_ckv': nrm((DEPTH, n_phys, PAGE_SIZE, KV_LORA), 1.0),
        'cache_krope': nrm((DEPTH, n_phys, PAGE_SIZE, ROPE_DIM), 1.0),
        'page_table': page_table,
        'g_mix': gain(D_MODEL),
        'w_in': nrm((DEPTH, D_MODEL, N_IN), D_MODEL ** -0.5),
        'conv_w': nrm((DEPTH, CONV_W, D_CONV), CONV_W ** -0.5),
        'ssm_lam_re': -0.5 + nrm((DEPTH, SSM_GROUPS, SSM_STATE), 0.01),
        'ssm_lam_im': lam_im0 + nrm((DEPTH, SSM_GROUPS, SSM_STATE), 0.01),
        'ssm_log_dt': jax.random.uniform(next(keys), (DEPTH, SSM_GROUPS), f32, math.log(1e-3), math.log(1e-1)),
        'ssm_b_re': nrm((DEPTH, SSM_GROUPS, SSM_STATE, SSM_GROUP), (2 * SSM_GROUP) ** -0.5),
        'ssm_b_im': nrm((DEPTH, SSM_GROUPS, SSM_STATE, SSM_GROUP), (2 * SSM_GROUP) ** -0.5),
        'ssm_c_re': nrm((DEPTH, SSM_GROUPS, SSM_GROUP, SSM_STATE), (2 * SSM_STATE) ** -0.5),
        'ssm_c_im': nrm((DEPTH, SSM_GROUPS, SSM_GROUP, SSM_STATE), (2 * SSM_STATE) ** -0.5),
        'ssm_d': nrm((DEPTH, D_SSM), 0.5),
        'w_glu': nrm((DEPTH, D_SSM, D_SSM), D_SSM ** -0.5),
        'b_glu': nrm((DEPTH, D_SSM), 0.01),
        'g_kv': gain(KV_LORA),
        'w_uk': nrm((DEPTH, KV_LORA, N_HEADS, NOPE_DIM), KV_LORA ** -0.5),
        'w_uv': nrm((DEPTH, KV_LORA, N_HEADS, V_DIM), KV_LORA ** -0.5),
        'g_qn': gain(NOPE_DIM),
        'g_kn': gain(NOPE_DIM),
        'g_qr': gain(ROPE_DIM),
        'g_kr': gain(ROPE_DIM),
        'w_br': nrm((DEPTH, N_BRANCH, BRANCH_W, D_MODEL), BRANCH_W ** -0.5),
        'w_o': nrm((DEPTH, D_MODEL, D_MODEL), D_MODEL ** -0.5),
        'g_ffn': gain(D_MODEL),
        'w_pq': nrm((DEPTH, D_MODEL, PEER_HEADS * 2 * PEER_DK), D_MODEL ** -0.5),
        'peer_keys': nrm((DEPTH, PEER_HEADS, 2, PEER_KEYS, PEER_DK), PEER_DK ** -0.5),
        'peer_u': nrm((DEPTH, PEER_EXPERTS, D_MODEL), D_MODEL ** -0.5),
        'peer_v': nrm((DEPTH, PEER_EXPERTS, D_MODEL), 0.5 * PEER_HEADS ** -0.5),
    }


def reference(x_prompt, x_sample, cache_conv, state_ssm_re, state_ssm_im, cache_ckv, cache_krope, page_table,
              g_mix, w_in, conv_w, ssm_lam_re, ssm_lam_im, ssm_log_dt, ssm_b_re, ssm_b_im, ssm_c_re, ssm_c_im,
              ssm_d, w_glu, b_glu, g_kv, w_uk, w_uv, g_qn, g_kn, g_qr, g_kr, w_br, w_o, g_ffn,
              w_pq, peer_keys, peer_u, peer_v):
    f32 = jnp.float32
    bp = x_prompt.shape[0]
    pos_p = jnp.arange(x_prompt.shape[1], dtype=jnp.int32)
    pos_s = PAST_LEN + jnp.arange(x_sample.shape[1], dtype=jnp.int32)
    xp, xs = x_prompt, x_sample
    p_conv, p_sre, p_sim, p_ckv, p_kr = [], [], [], [], []
    s_conv, s_sre, s_sim, s_ckv, s_kr = [], [], [], [], []
    for l in range(DEPTH):
        lp = {
            'g_mix': g_mix[l], 'w_in': w_in[l], 'conv_w': conv_w[l],
            'ssm_lam_re': ssm_lam_re[l], 'ssm_lam_im': ssm_lam_im[l], 'ssm_log_dt': ssm_log_dt[l],
            'ssm_b_re': ssm_b_re[l], 'ssm_b_im': ssm_b_im[l], 'ssm_c_re': ssm_c_re[l], 'ssm_c_im': ssm_c_im[l],
            'ssm_d': ssm_d[l], 'w_glu': w_glu[l], 'b_glu': b_glu[l],
            'g_kv': g_kv[l], 'g_qn': g_qn[l], 'g_qr': g_qr[l], 'g_kr': g_kr[l],
            'w_br': w_br[l], 'w_o': w_o[l], 'g_ffn': g_ffn[l],
            'w_pq': w_pq[l], 'peer_keys': peer_keys[l], 'peer_u': peer_u[l], 'peer_v': peer_v[l],
        }
        attn_p = functools.partial(mla_prompt, pos=pos_p, w_uk=w_uk[l], w_uv=w_uv[l], g_kn=g_kn[l])
        attn_s = functools.partial(mla_sample, pos=pos_s, w_uk=w_uk[l], w_uv=w_uv[l], g_kn=g_kn[l],
                                   ckv_pool=cache_ckv, kr_pool=cache_krope, layer=l, page_table=page_table)
        xp, (hc, hr, hi, ck, kr) = trunk_layer(
            xp, pos_p, jnp.zeros((bp, CONV_W - 1, D_CONV), xp.dtype),
            jnp.zeros((bp, SSM_GROUPS, SSM_STATE), f32), jnp.zeros((bp, SSM_GROUPS, SSM_STATE), f32),
            attn_p, PEER_BLOCK, lp)
        p_conv.append(hc)
        p_sre.append(hr)
        p_sim.append(hi)
        p_ckv.append(ck.reshape(bp, -1, PAGE_SIZE, KV_LORA))
        p_kr.append(kr.reshape(bp, -1, PAGE_SIZE, ROPE_DIM))
        xs, (hc, hr, hi, ck, kr) = trunk_layer(
            xs, pos_s, cache_conv[l], state_ssm_re[l], state_ssm_im[l], attn_s, x_sample.shape[1], lp)
        s_conv.append(hc)
        s_sre.append(hr)
        s_sim.append(hi)
        s_ckv.append(ck)
        s_kr.append(kr)
    return (xp, xs,
            jnp.stack(p_conv), jnp.stack(p_sre), jnp.stack(p_sim), jnp.stack(p_ckv), jnp.stack(p_kr),
            jnp.stack(s_conv), jnp.stack(s_sre), jnp.stack(s_sim), jnp.stack(s_ckv), jnp.stack(s_kr))
```

```python
import functools
import math

import jax
import jax.numpy as jnp
from jax import lax
from jax.experimental import pallas as pl
from jax.experimental.pallas import tpu as pltpu

EPS = 1e-6
ROPE_THETA = 10000.0
NEG_INF = -1e30
PEER_TOPK = 16

LANES = 128
SUBLANES = 8
VMEM_LIMIT = 48 * 1024 * 1024
VMEM_LIMIT_TABLE = 56 * 1024 * 1024

F32 = jnp.float32
BF16 = jnp.bfloat16


def _tile(n, pref):
    if n <= pref:
        return n
    t = pref - pref % SUBLANES
    while t >= SUBLANES:
        if n % t == 0:
            return t
        t -= SUBLANES
    return n


def _cparams(sem, limit=VMEM_LIMIT):
    return pltpu.CompilerParams(dimension_semantics=sem, vmem_limit_bytes=limit)


def _proj_kernel(x_ref, g_ref, w_ref, z_ref, h_ref):
    @pl.when(pl.program_id(1) == 0)
    def _():
        x = x_ref[...]
        ms = jnp.mean(x * x, axis=-1, keepdims=True)
        h_ref[...] = (x * lax.rsqrt(ms + EPS) * g_ref[...]).astype(BF16)

    z_ref[...] = jnp.dot(h_ref[...], w_ref[...], preferred_element_type=F32)


def _proj(x2d, g, w_p):
    t, d = x2d.shape
    n = w_p.shape[1]
    tm = _tile(t, 512)
    nb = n // LANES
    k = max(c for c in range(1, nb + 1) if nb % c == 0 and c * LANES <= 2304)
    tn = k * LANES
    return pl.pallas_call(
        _proj_kernel,
        grid=(t // tm, n // tn),
        in_specs=[
            pl.BlockSpec((tm, d), lambda i, j: (i, 0)),
            pl.BlockSpec((1, d), lambda i, j: (0, 0)),
            pl.BlockSpec((d, tn), lambda i, j: (0, j)),
        ],
        out_specs=pl.BlockSpec((tm, tn), lambda i, j: (i, j)),
        out_shape=jax.ShapeDtypeStruct((t, n), F32),
        scratch_shapes=[pltpu.VMEM((tm, d), BF16)],
        compiler_params=_cparams(("parallel", "arbitrary")),
        name="proj",
    )(x2d, g.reshape(1, d), w_p)


def _norm_rope(x, gain, cos, sin, nope, rope):
    lane = lax.broadcasted_iota(jnp.int32, (1, LANES), 1)
    m_n = lane < nope
    m_r = (lane >= nope) & (lane < nope + rope)
    sq = x * x
    ss_r = jnp.sum(jnp.where(m_r, sq, 0.0), axis=-1, keepdims=True)
    inv_r = lax.rsqrt(ss_r / rope + EPS)
    if nope:
        ss_n = jnp.sum(jnp.where(m_n, sq, 0.0), axis=-1, keepdims=True)
        inv = jnp.where(m_n, lax.rsqrt(ss_n / nope + EPS), inv_r)
    else:
        inv = inv_r
    y = x * inv * gain
    half = rope // 2
    first = lane < nope + half
    partner = jnp.where(first, pltpu.roll(y, LANES - half, 1), pltpu.roll(y, half, 1))
    return y * cos + partner * sin


def _prep_kernel(*refs, heads, nope, rope, seq_len, tm, prompt):
    if prompt:
        (xc_ref, bc_ref, cc_ref, q_ref, ckv_ref, kr_ref, cos_ref, sin_ref, cw_ref, gq_ref, gkr_ref, gkv_ref,
         hxc_ref, hcc_ref, oc_ref, qo_ref, ckvo_ref, kro_ref, vt_ref) = refs
    else:
        (xc_ref, bc_ref, cc_ref, q_ref, ckv_ref, kr_ref, cos_ref, sin_ref, cw_ref, gq_ref, gkr_ref, gkv_ref,
         h1_ref, h2_ref, oc_ref, qo_ref, ckvo_ref, kro_ref, vt_ref) = refs
    i = pl.program_id(0)
    v = cc_ref[...] * xc_ref[...]
    row = lax.broadcasted_iota(jnp.int32, (tm, 1), 0)
    r1 = pltpu.roll(v, 1, 0)
    r2 = pltpu.roll(v, 2, 0)
    if prompt:
        hv = hcc_ref[...] * hxc_ref[...]
        hv = jnp.where((i % (seq_len // tm)) == 0, 0.0, hv)
        v1 = jnp.where(row == 0, hv[7:8, :], r1)
        v2 = jnp.where(row == 0, hv[6:7, :], jnp.where(row == 1, hv[7:8, :], r2))
        vt_ref[0] = v[tm - SUBLANES:, :]
    else:
        l = row % seq_len
        v1 = jnp.where(l >= 1, r1, h1_ref[...])
        v2 = jnp.where(l >= 2, r2, h2_ref[...])
        vt_ref[...] = v
    y = v2 * cw_ref[0:1, :] + v1 * cw_ref[1:2, :] + v * cw_ref[2:3, :]
    oc_ref[...] = (bc_ref[...] * y).astype(oc_ref.dtype)

    cos = cos_ref[...]
    sin = sin_ref[...]
    gq = gq_ref[...]
    for h in range(heads):
        sl = slice(LANES * h, LANES * (h + 1))
        qo_ref[:, sl] = _norm_rope(q_ref[:, sl], gq, cos, sin, nope, rope).astype(qo_ref.dtype)
    kro_ref[...] = _kr_norm_rope(kr_ref[...], gkr_ref[...], cos, sin, nope, rope)
    c = ckv_ref[...]
    ms = jnp.mean(c * c, axis=-1, keepdims=True)
    ckvo_ref[...] = c * lax.rsqrt(ms + EPS) * gkv_ref[...]


def _kr_norm_rope(x, gain, cos, sin, nope, rope):
    lane = lax.broadcasted_iota(jnp.int32, (1, LANES), 1)
    ss = jnp.sum(x * x, axis=-1, keepdims=True)
    y = x * lax.rsqrt(ss / rope + EPS) * gain
    half = rope // 2
    first = lane < nope + half
    partner = jnp.where(first, pltpu.roll(y, LANES - half, 1), pltpu.roll(y, half, 1))
    return y * cos + partner * sin


def _prep(z, lay, cos_t, sin_t, conv_w, gq, gkr, gkv, seq_len, prompt, hist1=None, hist2=None, q_dtype=BF16):
    t = z.shape[0]
    bw, heads, c_lat = lay["bw"], lay["heads"], lay["c_lat"]
    hq = heads * LANES
    if prompt:
        tm = _tile(seq_len, 512)
        assert seq_len % tm == 0 and tm >= 2 * SUBLANES
    else:
        tm = _tile(t, 512)
        assert tm % seq_len == 0
    ntab = cos_t.shape[0] // tm
    in_specs = [
        pl.BlockSpec((tm, bw), lambda i: (i, 0)),
        pl.BlockSpec((tm, bw), lambda i: (i, 1)),
        pl.BlockSpec((tm, bw), lambda i: (i, 2)),
        pl.BlockSpec((tm, hq), lambda i: (i, lay["off_q"] // hq)),
        pl.BlockSpec((tm, c_lat), lambda i: (i, lay["off_ckv"] // c_lat)),
        pl.BlockSpec((tm, LANES), lambda i: (i, lay["off_kr"] // LANES)),
        pl.BlockSpec((tm, LANES), lambda i: (i % ntab, 0)),
        pl.BlockSpec((tm, LANES), lambda i: (i % ntab, 0)),
        pl.BlockSpec(conv_w.shape, lambda i: (0, 0)),
        pl.BlockSpec((1, LANES), lambda i: (0, 0)),
        pl.BlockSpec((1, LANES), lambda i: (0, 0)),
        pl.BlockSpec((1, c_lat), lambda i: (0, 0)),
    ]
    args = [z, z, z, z, z, z, cos_t, sin_t, conv_w, gq, gkr, gkv]
    if prompt:
        rb = tm // SUBLANES
        in_specs += [
            pl.BlockSpec((SUBLANES, bw), lambda i: (jnp.maximum(i * rb - 1, 0), 0)),
            pl.BlockSpec((SUBLANES, bw), lambda i: (jnp.maximum(i * rb - 1, 0), 2)),
        ]
        args += [z, z]
        vt_spec = pl.BlockSpec((1, SUBLANES, bw), lambda i: (i, 0, 0))
        vt_shape = jax.ShapeDtypeStruct((t // tm, SUBLANES, bw), F32)
    else:
        in_specs += [pl.BlockSpec((tm, bw), lambda i: (i, 0)), pl.BlockSpec((tm, bw), lambda i: (i, 0))]
        args += [hist1, hist2]
        vt_spec = pl.BlockSpec((tm, bw), lambda i: (i, 0))
        vt_shape = jax.ShapeDtypeStruct((t, bw), F32)
    kern = functools.partial(_prep_kernel, heads=heads, nope=lay["nope"], rope=lay["rope"], seq_len=seq_len, tm=tm,
                             prompt=prompt)
    return pl.pallas_call(
        kern,
        grid=(t // tm,),
        in_specs=in_specs,
        out_specs=[
            pl.BlockSpec((tm, bw), lambda i: (i, 0)),
            pl.BlockSpec((tm, hq), lambda i: (i, 0)),
            pl.BlockSpec((tm, c_lat), lambda i: (i, 0)),
            pl.BlockSpec((tm, LANES), lambda i: (i, 0)),
            vt_spec,
        ],
        out_shape=[
            jax.ShapeDtypeStruct((t, bw), BF16),
            jax.ShapeDtypeStruct((t, hq), q_dtype),
            jax.ShapeDtypeStruct((t, c_lat), F32),
            jax.ShapeDtypeStruct((t, LANES), F32),
            vt_shape,
        ],
        compiler_params=_cparams(("parallel",)),
        name="prep_prompt" if prompt else "prep_sample",
    )(*args)


def _s5_mats(lam_re, lam_im, log_dt, b_re, b_im, c_re, c_im, lc):
    hp = lax.Precision.HIGHEST
    g, p, n_in = b_re.shape
    n_out = c_re.shape[1]
    dt = jnp.exp(log_dt)[:, None]
    lr, li = lam_re, lam_im
    mag = jnp.exp(lr * dt)
    a_re, a_im = mag * jnp.cos(li * dt), mag * jnp.sin(li * dt)
    den = lr * lr + li * li
    f_re = ((a_re - 1.0) * lr + a_im * li) / den
    f_im = (a_im * lr - (a_re - 1.0) * li) / den
    bb_re = f_re[..., None] * b_re - f_im[..., None] * b_im
    bb_im = f_re[..., None] * b_im + f_im[..., None] * b_re
    k = jnp.arange(lc + 1, dtype=F32)[:, None, None]
    pm = jnp.exp(lr[None] * dt[None] * k)
    pr = pm * jnp.cos(li[None] * dt[None] * k)
    pi = pm * jnp.sin(li[None] * dt[None] * k)
    ab_re = pr[:lc, ..., None] * bb_re[None] - pi[:lc, ..., None] * bb_im[None]
    ab_im = pr[:lc, ..., None] * bb_im[None] + pi[:lc, ..., None] * bb_re[None]
    ms = jnp.concatenate([ab_re[::-1], ab_im[::-1]], axis=2)
    ms = jnp.transpose(ms, (1, 0, 3, 2)).reshape(g, lc * n_in, 2 * p)
    kk = (jnp.einsum("gop,dgpi->dgoi", c_re, ab_re, precision=hp)
          - jnp.einsum("gop,dgpi->dgoi", c_im, ab_im, precision=hp))
    s_idx = jnp.arange(lc)[:, None]
    t_idx = jnp.arange(lc)[None, :]
    delta = t_idx - s_idx
    kt = jnp.where((delta >= 0)[:, :, None, None, None], kk[jnp.clip(delta, 0, lc - 1)], 0.0)
    tk = jnp.transpose(kt, (2, 0, 4, 1, 3)).reshape(g, lc * n_in, lc * n_out)
    ca_re = c_re[None] * pr[1:, :, None, :] - c_im[None] * pi[1:, :, None, :]
    ca_im = c_re[None] * pi[1:, :, None, :] + c_im[None] * pr[1:, :, None, :]
    gs = jnp.concatenate([ca_re, -ca_im], axis=3)
    gs = jnp.transpose(gs, (1, 3, 0, 2)).reshape(g, 2 * p, lc * n_out)
    a1 = jnp.concatenate([pr[lc], pr[lc]], axis=-1)[:, None, :]
    a2 = jnp.concatenate([-pi[lc], pi[lc]], axis=-1)[:, None, :]
    return ms.astype(BF16), tk.astype(BF16), gs.astype(BF16), a1, a2


def _s5_kernel(u_ref, tk_ref, ms_ref, gs_ref, a1_ref, a2_ref, h0_ref, y_ref, ht_ref, s_sc, hin_sc, *, n_chunks, bp,
               p_state):
    u = u_ref[...]
    s_sc[...] = jnp.dot(u, ms_ref[...], preferred_element_type=F32)
    a1 = a1_ref[...]
    a2 = a2_ref[...]

    def body(c, h):
        r = pl.multiple_of(c * bp, SUBLANES)
        hin_sc[pl.ds(r, bp), :] = h
        return a1 * h + a2 * pltpu.roll(h, p_state, 1) + s_sc[pl.ds(r, bp), :]

    h = lax.fori_loop(0, n_chunks, body, h0_ref[...])
    ht_ref[...] = h
    y_ref[...] = (jnp.dot(u, tk_ref[...], preferred_element_type=F32)
                  + jnp.dot(hin_sc[...].astype(BF16), gs_ref[...], preferred_element_type=F32))


def _s5(us, h0_re, h0_im, mats, bsz, seq_len, lc):
    ms, tk, gs, a1, a2 = mats
    g, _, p2 = ms.shape
    p_state = p2 // 2
    n_in = ms.shape[1] // lc
    n_out = tk.shape[2] // lc
    n_chunks = seq_len // lc
    bp = -(-bsz // SUBLANES) * SUBLANES
    nc = n_chunks * bp
    u = us.reshape(bsz, n_chunks, lc, g, n_in)
    u = jnp.transpose(u, (3, 1, 0, 2, 4))
    u = jnp.pad(u, ((0, 0), (0, 0), (0, bp - bsz), (0, 0), (0, 0))).reshape(g, nc, lc * n_in).astype(BF16)
    h0 = jnp.concatenate([h0_re, h0_im], axis=-1)
    h0 = jnp.pad(jnp.transpose(h0, (1, 0, 2)), ((0, 0), (0, bp - bsz), (0, 0)))
    kern = functools.partial(_s5_kernel, n_chunks=n_chunks, bp=bp, p_state=p_state)
    y, ht = pl.pallas_call(
        kern,
        grid=(g,),
        in_specs=[
            pl.BlockSpec((None, nc, lc * n_in), lambda i: (i, 0, 0)),
            pl.BlockSpec((None, lc * n_in, lc * n_out), lambda i: (i, 0, 0)),
            pl.BlockSpec((None, lc * n_in, p2), lambda i: (i, 0, 0)),
            pl.BlockSpec((None, p2, lc * n_out), lambda i: (i, 0, 0)),
            pl.BlockSpec((None, 1, p2), lambda i: (i, 0, 0)),
            pl.BlockSpec((None, 1, p2), lambda i: (i, 0, 0)),
            pl.BlockSpec((None, bp, p2), lambda i: (i, 0, 0)),
        ],
        out_specs=[
            pl.BlockSpec((None, nc, lc * n_out), lambda i: (i, 0, 0)),
            pl.BlockSpec((None, bp, p2), lambda i: (i, 0, 0)),
        ],
        out_shape=[
            jax.ShapeDtypeStruct((g, nc, lc * n_out), F32),
            jax.ShapeDtypeStruct((g, bp, p2), F32),
        ],
        scratch_shapes=[pltpu.VMEM((nc, p2), F32), pltpu.VMEM((nc, p2), F32)],
        compiler_params=_cparams(("parallel",)),
        name="s5",
    )(u, tk, ms, gs, a1, a2, h0)
    y = y.reshape(g, n_chunks, bp, lc, n_out)[:, :, :bsz]
    y = jnp.transpose(y, (2, 1, 3, 0, 4)).reshape(bsz * seq_len, g * n_out)
    ht = jnp.transpose(ht[:, :bsz], (1, 0, 2))
    return y, ht[..., :p_state], ht[..., p_state:]


def _kprep_kernel(ckv_ref, kr_ref, wuk_ref, wuv_ref, gkn_ref, k_ref, v_ref, *, heads, nope):
    c = ckv_ref[...].astype(BF16)
    kraw = jnp.dot(c, wuk_ref[...], preferred_element_type=F32)
    kr = kr_ref[...]
    gkn = gkn_ref[...]
    for h in range(heads):
        sl = slice(LANES * h, LANES * (h + 1))
        kh = kraw[:, sl]
        ss = jnp.sum(kh * kh, axis=-1, keepdims=True)
        k_ref[:, sl] = (kh * lax.rsqrt(ss / nope + EPS) * gkn + kr).astype(BF16)
    v_ref[...] = jnp.dot(c, wuv_ref[...], preferred_element_type=F32).astype(BF16)


def _kprep(ckv_n, kr_n, wuk_p, wuv, gkn, heads, nope):
    t, c_lat = ckv_n.shape
    tm = _tile(t, 512)
    hk = wuk_p.shape[1]
    hv = wuv.shape[1]
    return pl.pallas_call(
        functools.partial(_kprep_kernel, heads=heads, nope=nope),
        grid=(t // tm,),
        in_specs=[
            pl.BlockSpec((tm, c_lat), lambda i: (i, 0)),
            pl.BlockSpec((tm, LANES), lambda i: (i, 0)),
            pl.BlockSpec((c_lat, hk), lambda i: (0, 0)),
            pl.BlockSpec((c_lat, hv), lambda i: (0, 0)),
            pl.BlockSpec((1, LANES), lambda i: (0, 0)),
        ],
        out_specs=[pl.BlockSpec((tm, hk), lambda i: (i, 0)), pl.BlockSpec((tm, hv), lambda i: (i, 0))],
        out_shape=[jax.ShapeDtypeStruct((t, hk), BF16), jax.ShapeDtypeStruct((t, hv), BF16)],
        compiler_params=_cparams(("parallel",)),
        name="kprep",
    )(ckv_n, kr_n, wuk_p, wuv, gkn)


def _flash_kernel(q_ref, k_ref, v_ref, o_ref, m_sc, l_sc, acc_sc, *, scale, tq, hps, vd):
    i = pl.program_id(2)
    j = pl.program_id(3)

    @pl.when(j == 0)
    def _():
        m_sc[...] = jnp.full(m_sc.shape, NEG_INF, F32)
        l_sc[...] = jnp.zeros(l_sc.shape, F32)
        acc_sc[...] = jnp.zeros(acc_sc.shape, F32)

    @pl.when(j <= i)
    def _():
        row = lax.broadcasted_iota(jnp.int32, (tq, tq), 0)
        col = lax.broadcasted_iota(jnp.int32, (tq, tq), 1)
        keep = (col <= row) | (j < i)
        for hh in range(hps):
            q = q_ref[:, LANES * hh:LANES * (hh + 1)]
            k = k_ref[:, LANES * hh:LANES * (hh + 1)]
            s = lax.dot_general(q, k, (((1,), (1,)), ((), ())), preferred_element_type=F32) * scale
            s = jnp.where(keep, s, NEG_INF)
            m_prev = m_sc[hh]
            m_new = jnp.maximum(m_prev, jnp.max(s, axis=-1, keepdims=True))
            alpha = jnp.exp(m_prev - m_new)
            p = jnp.exp(s - m_new)
            l_sc[hh] = alpha * l_sc[hh] + jnp.sum(p, axis=-1, keepdims=True)
            acc_sc[hh] = alpha * acc_sc[hh] + jnp.dot(p.astype(BF16), v_ref[:, vd * hh:vd * (hh + 1)],
                                                      preferred_element_type=F32)
            m_sc[hh] = m_new

    @pl.when(j == pl.num_programs(3) - 1)
    def _():
        for hh in range(hps):
            o_ref[:, vd * hh:vd * (hh + 1)] = (acc_sc[hh] / l_sc[hh]).astype(o_ref.dtype)


def _flash(q, k, v, bsz, seq_len, heads, vd, scale):
    hps = LANES // vd
    assert heads % hps == 0
    tq = _tile(seq_len, 512)
    nq = seq_len // tq
    t = bsz * seq_len
    kern = functools.partial(_flash_kernel, scale=scale, tq=tq, hps=hps, vd=vd)
    return pl.pallas_call(
        kern,
        grid=(bsz, heads // hps, nq, nq),
        in_specs=[
            pl.BlockSpec((tq, hps * LANES), lambda b, h, i, j: (b * nq + i, h)),
            pl.BlockSpec((tq, hps * LANES), lambda b, h, i, j: (b * nq + jnp.minimum(i, j), h)),
            pl.BlockSpec((tq, LANES), lambda b, h, i, j: (b * nq + jnp.minimum(i, j), h)),
        ],
        out_specs=pl.BlockSpec((tq, LANES), lambda b, h, i, j: (b * nq + i, h)),
        out_shape=jax.ShapeDtypeStruct((t, heads * vd), BF16),
        scratch_shapes=[pltpu.VMEM((hps, tq, 1), F32), pltpu.VMEM((hps, tq, 1), F32), pltpu.VMEM((hps, tq, vd), F32)],
        compiler_params=_cparams(("parallel", "parallel", "parallel", "arbitrary")),
        name="flash",
    )(q, k, v)


def _paged_kernel(pt_ref, qbd_ref, qr_ref, cn_ref, rn_ref, *refs, pp, heads, nope, vd, lq, scale):
    del pt_ref
    c_pages = refs[:pp]
    r_pages = refs[pp:2 * pp]
    wuk_ref, wuv_ref, e_ref, gk_ref, o_ref, m_sc, l_sc, acc_sc = refs[2 * pp:]
    s_id = pl.program_id(1)
    rows = lq * heads

    @pl.when(s_id == 0)
    def _():
        m_sc[...] = jnp.full(m_sc.shape, NEG_INF, F32)
        l_sc[...] = jnp.zeros(l_sc.shape, F32)
        acc_sc[...] = jnp.zeros(acc_sc.shape, F32)

    qbd = (qbd_ref[...] * gk_ref[...]).astype(BF16)
    qr = qr_ref[...].astype(BF16)
    dn = (((1,), (1,)), ((), ()))

    def update(c_f32, r_f32, causal):
        cb = c_f32.astype(BF16)
        kraw = jnp.dot(cb, wuk_ref[...], preferred_element_type=F32)
        ssq = lax.dot_general(e_ref[...], (kraw * kraw).astype(BF16), dn, preferred_element_type=F32)
        inv = lax.rsqrt(ssq / nope + EPS)
        inv = jnp.concatenate([inv] * lq, axis=0)
        sn = lax.dot_general(qbd, kraw.astype(BF16), dn, preferred_element_type=F32)
        sr = lax.dot_general(qr, r_f32.astype(BF16), dn, preferred_element_type=F32)
        s = (sn * inv + sr) * scale
        if causal:
            nk = c_f32.shape[0]
            kk = lax.broadcasted_iota(jnp.int32, (rows, nk), 1)
            qq = lax.broadcasted_iota(jnp.int32, (rows, nk), 0) // heads
            s = jnp.where(kk <= qq, s, NEG_INF)
        m_prev = m_sc[...]
        m_new = jnp.maximum(m_prev, jnp.max(s, axis=-1, keepdims=True))
        alpha = jnp.exp(m_prev - m_new)
        p = jnp.exp(s - m_new)
        l_sc[...] = alpha * l_sc[...] + jnp.sum(p, axis=-1, keepdims=True)
        acc_sc[...] = alpha * acc_sc[...] + jnp.dot(p.astype(BF16), cb, preferred_element_type=F32)
        m_sc[...] = m_new

    for pg in range(pp):
        update(c_pages[pg][...], r_pages[pg][...], False)

    @pl.when(s_id == pl.num_programs(1) - 1)
    def _():
        update(cn_ref[...], rn_ref[...], True)
        lat = (acc_sc[...] / l_sc[...]).astype(BF16)
        full = jnp.dot(lat, wuv_ref[...], preferred_element_type=F32)
        colh = lax.broadcasted_iota(jnp.int32, full.shape, 1) // vd
        rowh = lax.broadcasted_iota(jnp.int32, full.shape, 0) % heads
        full = jnp.where(colh == rowh, full, 0.0)
        o_ref[...] = jnp.sum(full.reshape(lq, heads, heads * vd), axis=1)


def _paged(page_table, qbd, qr, c_new, r_new, cache_ckv, cache_krope, layer, wuk, wuv, e_mat, gk, heads, nope, vd,
           scale):
    bs, rows, _ = qbd.shape
    lq = rows // heads
    n_pages = page_table.shape[1]
    page, c_lat = cache_ckv.shape[2], cache_ckv.shape[3]
    rope = cache_krope.shape[3]
    pp = math.gcd(n_pages, 8)
    kn = c_new.shape[1]

    def cmap(p):
        return lambda b, s, pt: (layer, pt[b, s * pp + p], 0, 0)

    in_specs = [
        pl.BlockSpec((None, rows, heads * nope), lambda b, s, pt: (b, 0, 0)),
        pl.BlockSpec((None, rows, rope), lambda b, s, pt: (b, 0, 0)),
        pl.BlockSpec((None, kn, c_lat), lambda b, s, pt: (b, 0, 0)),
        pl.BlockSpec((None, kn, rope), lambda b, s, pt: (b, 0, 0)),
    ]
    in_specs += [pl.BlockSpec((None, None, page, c_lat), cmap(p)) for p in range(pp)]
    in_specs += [pl.BlockSpec((None, None, page, rope), cmap(p)) for p in range(pp)]
    in_specs += [
        pl.BlockSpec(wuk.shape, lambda b, s, pt: (0, 0)),
        pl.BlockSpec(wuv.shape, lambda b, s, pt: (0, 0)),
        pl.BlockSpec(e_mat.shape, lambda b, s, pt: (0, 0)),
        pl.BlockSpec(gk.shape, lambda b, s, pt: (0, 0)),
    ]
    kern = functools.partial(_paged_kernel, pp=pp, heads=heads, nope=nope, vd=vd, lq=lq, scale=scale)
    grid_spec = pltpu.PrefetchScalarGridSpec(
        num_scalar_prefetch=1,
        grid=(bs, n_pages // pp),
        in_specs=in_specs,
        out_specs=pl.BlockSpec((None, lq, heads * vd), lambda b, s, pt: (b, 0, 0)),
        scratch_shapes=[pltpu.VMEM((rows, 1), F32), pltpu.VMEM((rows, 1), F32), pltpu.VMEM((rows, c_lat), F32)],
    )
    return pl.pallas_call(
        kern,
        grid_spec=grid_spec,
        out_shape=jax.ShapeDtypeStruct((bs, lq, heads * vd), F32),
        compiler_params=_cparams(("parallel", "arbitrary")),
        name="paged",
    )(page_table, qbd, qr, c_new, r_new, *([cache_ckv] * pp), *([cache_krope] * pp), wuk, wuv, e_mat, gk)


def _merge_kernel(x_ref, oc_ref, ys_ref, us_ref, oa_ref, g0_ref, g1_ref, g2_ref, d_ref, wglu_ref, bglu_ref, wbr_ref,
                  wo_ref, gffn_ref, wpq_ref, x1_ref, h2_ref, qp_ref):
    y = ys_ref[...] + d_ref[...] * us_ref[...]
    zg = jax.nn.gelu(y)
    gl = jnp.dot(zg.astype(BF16), wglu_ref[...], preferred_element_type=F32) + bglu_ref[...]
    o_ssm = zg * jax.nn.sigmoid(gl)
    merged = jax.nn.sigmoid(g0_ref[...]) * jnp.dot(oc_ref[...], wbr_ref[0], preferred_element_type=F32)
    merged += jax.nn.sigmoid(g1_ref[...]) * jnp.dot(o_ssm.astype(BF16), wbr_ref[1], preferred_element_type=F32)
    merged += jax.nn.sigmoid(g2_ref[...]) * jnp.dot(oa_ref[...].astype(BF16), wbr_ref[2], preferred_element_type=F32)
    x1 = x_ref[...] + jnp.dot(merged.astype(BF16), wo_ref[...], preferred_element_type=F32)
    x1_ref[...] = x1
    ms = jnp.mean(x1 * x1, axis=-1, keepdims=True)
    h2 = x1 * lax.rsqrt(ms + EPS) * gffn_ref[...]
    h2_ref[...] = h2
    qp_ref[...] = jnp.dot(h2.astype(BF16), wpq_ref[...], preferred_element_type=F32)


def _merge(x2d, oc, ys, z, oa, lay, d, wglu, bglu, wbr, wo, gffn, wpq):
    t, dm = x2d.shape
    bw = lay["bw"]
    dq = wpq.shape[1]
    tm = _tile(t, 256)
    gi = lay["off_g"] // dm

    def full(a):
        return pl.BlockSpec(a.shape, lambda i, _n=a.ndim: (0,) * _n)

    return pl.pallas_call(
        _merge_kernel,
        grid=(t // tm,),
        in_specs=[
            pl.BlockSpec((tm, dm), lambda i: (i, 0)),
            pl.BlockSpec((tm, bw), lambda i: (i, 0)),
            pl.BlockSpec((tm, bw), lambda i: (i, 0)),
            pl.BlockSpec((tm, bw), lambda i: (i, 3)),
            pl.BlockSpec((tm, bw), lambda i: (i, 0)),
            pl.BlockSpec((tm, dm), lambda i: (i, gi)),
            pl.BlockSpec((tm, dm), lambda i: (i, gi + 1)),
            pl.BlockSpec((tm, dm), lambda i: (i, gi + 2)),
            full(d), full(wglu), full(bglu), full(wbr), full(wo), full(gffn), full(wpq),
        ],
        out_specs=[
            pl.BlockSpec((tm, dm), lambda i: (i, 0)),
            pl.BlockSpec((tm, dm), lambda i: (i, 0)),
            pl.BlockSpec((tm, dq), lambda i: (i, 0)),
        ],
        out_shape=[
            jax.ShapeDtypeStruct((t, dm), F32),
            jax.ShapeDtypeStruct((t, dm), F32),
            jax.ShapeDtypeStruct((t, dq), F32),
        ],
        compiler_params=_cparams(("parallel",)),
        name="merge",
    )(x2d, oc, ys, z, oa, z, z, z, d, wglu, bglu, wbr, wo, gffn, wpq)


def _topk_rows(x, k):
    n = x.shape[0]
    iota = lax.broadcasted_iota(jnp.int32, x.shape, 0)
    vals, idxs = [], []
    for _ in range(k):
        m = jnp.max(x, axis=0, keepdims=True)
        am = jnp.min(jnp.where(x == m, iota, n), axis=0, keepdims=True)
        vals.append(m)
        idxs.append(am)
        x = jnp.where(iota == am, -jnp.inf, x)
    return jnp.concatenate(vals, axis=0), jnp.concatenate(idxs, axis=0)


def _select_rows(table, sel, k):
    out = jnp.zeros(sel.shape, table.dtype)
    for r in range(k):
        out = jnp.where(sel == r, table[r:r + 1, :], out)
    return out


def _route_kernel(qp_ref, keys_ref, e_ref, g_ref, *, heads, n_keys, dk, topk):
    dn = (((1,), (1,)), ((), ()))
    for h in range(heads):
        sub = []
        for s in range(2):
            o = (2 * h + s) * dk
            qs = qp_ref[:, o:o + dk].astype(BF16)
            st = lax.dot_general(keys_ref[h, s], qs, dn, preferred_element_type=F32)
            sub.append(_topk_rows(st, topk))
        (v1, i1), (v2, i2) = sub
        cand = (v1[:, None, :] + v2[None, :, :]).reshape(topk * topk, v1.shape[-1])
        sc, ci = _topk_rows(cand, topk)
        e1 = _select_rows(i1, ci // topk, topk)
        e2 = _select_rows(i2, ci % topk, topk)
        ex = jnp.exp(sc - sc[0:1, :])
        e_ref[topk * h:topk * (h + 1), :] = e1 * n_keys + e2
        g_ref[topk * h:topk * (h + 1), :] = ex / jnp.sum(ex, axis=0, keepdims=True)


def _route(qp, keys_bf16):
    t, dq = qp.shape
    heads, _, n_keys, dk = keys_bf16.shape
    tm = _tile(t, 128)
    if t % LANES:
        tm = t
    npair = heads * PEER_TOPK
    kern = functools.partial(_route_kernel, heads=heads, n_keys=n_keys, dk=dk, topk=PEER_TOPK)
    return pl.pallas_call(
        kern,
        grid=(t // tm,),
        in_specs=[
            pl.BlockSpec((tm, dq), lambda i: (i, 0)),
            pl.BlockSpec(keys_bf16.shape, lambda i: (0, 0, 0, 0)),
        ],
        out_specs=[pl.BlockSpec((npair, tm), lambda i: (0, i)), pl.BlockSpec((npair, tm), lambda i: (0, i))],
        out_shape=[jax.ShapeDtypeStruct((npair, t), jnp.int32), jax.ShapeDtypeStruct((npair, t), F32)],
        compiler_params=_cparams(("parallel",)),
        name="route",
    )(qp, keys_bf16)


def _peer_u_kernel(e_ref, h_ref, tab_ref, a_ref, *, tb, npair):
    def body(t, carry):
        th = h_ref[t]
        cols = []
        for g in range(npair // SUBLANES):
            qs = []
            for j in range(SUBLANES):
                r = tab_ref[e_ref[t, SUBLANES * g + j]]
                qs.append(jnp.sum(r * th, axis=0, keepdims=True))
            cols.append(jnp.sum(jnp.concatenate(qs, axis=0), axis=-1, keepdims=True))
        a_ref[t] = jnp.concatenate(cols, axis=1)
        return carry

    lax.fori_loop(0, tb, body, 0)


def _peer_v_kernel(e_ref, w_ref, x_ref, tab_ref, o_ref, *, tb, npair):
    def body(t, carry):
        accs = [jnp.zeros(x_ref.shape[1:], F32) for _ in range(4)]
        for j in range(npair):
            accs[j % 4] = accs[j % 4] + w_ref[t, j] * tab_ref[e_ref[t, j]]
        o_ref[t] = x_ref[t] + ((accs[0] + accs[1]) + (accs[2] + accs[3]))
        return carry

    lax.fori_loop(0, tb, body, 0)


def _peer_w_kernel(a_ref, g_ref, w_ref):
    w_ref[...] = g_ref[...] * jax.nn.gelu(a_ref[0] + a_ref[1])


def _split_rows(a, rows):
    n = a.shape[0]
    return jnp.transpose(a.reshape(n, 2, rows, LANES), (1, 0, 2, 3))


def _peer(x1, h2, e_pt, g_pt, u_tab, v_tab):
    t, dm = x1.shape
    n_exp = u_tab.shape[0]
    npair = e_pt.shape[0]
    rows = dm // (2 * LANES)
    tb = _tile(t, 64)
    nblk = t // tb
    ng = npair // SUBLANES
    e_nat = jnp.transpose(e_pt)
    e_u = jnp.transpose(e_nat.reshape(t, SUBLANES, ng), (0, 2, 1)).reshape(nblk, tb, npair)
    e_v = e_nat.reshape(nblk, tb, npair)
    g_nat = jnp.transpose(g_pt)
    u_h = _split_rows(u_tab, rows)
    v_h = _split_rows(v_tab, rows)
    h_h = _split_rows(h2, rows)
    x_h = _split_rows(x1, rows)
    smem = functools.partial(pl.BlockSpec, memory_space=pltpu.SMEM)
    tab_spec = pl.BlockSpec((None, n_exp, rows, LANES), lambda hf, i: (hf, 0, 0, 0), pipeline_mode=pl.Buffered(1))
    tok_spec = pl.BlockSpec((None, tb, rows, LANES), lambda hf, i: (hf, i, 0, 0))
    a_parts = pl.pallas_call(
        functools.partial(_peer_u_kernel, tb=tb, npair=npair),
        grid=(2, nblk),
        in_specs=[smem((None, tb, npair), lambda hf, i: (i, 0, 0)), tok_spec, tab_spec],
        out_specs=pl.BlockSpec((None, tb, SUBLANES, ng), lambda hf, i: (hf, i, 0, 0)),
        out_shape=jax.ShapeDtypeStruct((2, t, SUBLANES, ng), F32),
        compiler_params=_cparams(("arbitrary", "arbitrary"), VMEM_LIMIT_TABLE),
        name="peer_u",
    )(e_u, h_h, u_h)
    tw = _tile(t, 512)
    w = pl.pallas_call(
        _peer_w_kernel,
        grid=(t // tw,),
        in_specs=[pl.BlockSpec((2, tw, npair), lambda i: (0, i, 0)), pl.BlockSpec((tw, npair), lambda i: (i, 0))],
        out_specs=pl.BlockSpec((tw, npair), lambda i: (i, 0)),
        out_shape=jax.ShapeDtypeStruct((t, npair), F32),
        compiler_params=_cparams(("parallel",)),
        name="peer_w",
    )(a_parts.reshape(2, t, npair), g_nat)
    out_h = pl.pallas_call(
        functools.partial(_peer_v_kernel, tb=tb, npair=npair),
        grid=(2, nblk),
        in_specs=[smem((None, tb, npair), lambda hf, i: (i, 0, 0)), smem((None, tb, npair), lambda hf, i: (i, 0, 0)),
                  tok_spec, tab_spec],
        out_specs=tok_spec,
        out_shape=jax.ShapeDtypeStruct((2, t, rows, LANES), F32),
        compiler_params=_cparams(("arbitrary", "arbitrary"), VMEM_LIMIT_TABLE),
        name="peer_v",
    )(e_v, w.reshape(nblk, tb, npair), x_h, v_h)
    return jnp.transpose(out_h, (1, 0, 2, 3)).reshape(t, dm)


def _layout(dm, bw, heads, nope, rope, c_lat):
    hq = heads * LANES
    lay = dict(bw=bw, heads=heads, nope=nope, rope=rope, c_lat=c_lat)
    lay["off_q"] = 4 * bw
    lay["off_g"] = lay["off_q"] + hq
    lay["off_ckv"] = lay["off_g"] + 3 * dm
    lay["off_kr"] = lay["off_ckv"] + c_lat
    lay["n"] = lay["off_kr"] + LANES
    assert nope + rope <= LANES and rope % 2 == 0
    assert lay["off_q"] % hq == 0 and lay["off_g"] % dm == 0 and lay["off_ckv"] % c_lat == 0
    assert bw % LANES == 0 and c_lat % LANES == 0
    return lay


def _pack_w_in(w_in, lay, dm):
    bw, heads, nope, rope, c_lat = lay["bw"], lay["heads"], lay["nope"], lay["rope"], lay["c_lat"]
    sizes = (bw, bw, bw, bw, heads * (nope + rope), c_lat, rope, 3 * dm)
    parts, off = [], 0
    for n in sizes:
        parts.append(w_in[:, off:off + n])
        off += n
    xc, bc, cc, us, q, ckv, kr, gates = parts
    q = jnp.pad(q.reshape(dm, heads, nope + rope), ((0, 0), (0, 0), (0, LANES - nope - rope))).reshape(dm, heads * LANES)
    kr = jnp.pad(kr, ((0, 0), (nope, LANES - nope - rope)))
    return jnp.concatenate([xc, bc, cc, us, q, gates, ckv, kr], axis=1).astype(BF16)


def _rope_tables(pos, nope, rope):
    half = rope // 2
    inv = ROPE_THETA ** (-jnp.arange(half, dtype=F32) / half)
    ang = pos.astype(F32)[:, None] * inv[None, :]
    cos, sin = jnp.cos(ang), jnp.sin(ang)
    n = pos.shape[0]
    ones = jnp.ones((n, nope), F32)
    tail = LANES - nope - rope
    cos_t = jnp.concatenate([ones, cos, cos, jnp.ones((n, tail), F32)], axis=1)
    sin_t = jnp.concatenate([jnp.zeros((n, nope), F32), -sin, sin, jnp.zeros((n, tail), F32)], axis=1)
    return cos_t, sin_t


def _lane_vec(parts):
    v = jnp.concatenate(parts)
    return jnp.pad(v, (0, LANES - v.shape[0])).reshape(1, LANES)


def _layer(x2d, bsz, seq_len, pos, lp, lay, prompt, conv_hist, h0_re, h0_im, sample_ctx):
    t, dm = x2d.shape
    bw, heads, nope, rope, c_lat = lay["bw"], lay["heads"], lay["nope"], lay["rope"], lay["c_lat"]
    vd = lp["w_uv"].shape[2]
    scale = (nope + rope) ** -0.5

    z = _proj(x2d, lp["g_mix"], lp["w_in_p"])

    cos_t, sin_t = _rope_tables(pos, nope, rope)
    gq = _lane_vec([lp["g_qn"], lp["g_qr"]])
    gkr = _lane_vec([jnp.zeros((nope,), F32), lp["g_kr"]])
    gkv = lp["g_kv"].reshape(1, c_lat)
    if prompt:
        oc, qn, ckv_n, kr_n, vt = _prep(z, lay, cos_t, sin_t, lp["conv_w"], gq, gkr, gkv, seq_len, True)
        nblk_seq = seq_len // (t // vt.shape[0])
        new_hist = vt.reshape(bsz, nblk_seq, SUBLANES, bw)[:, -1, SUBLANES - 2:, :]
    else:
        tm = _tile(t, 512)
        reps = tm // seq_len
        cos_t = jnp.tile(cos_t, (reps, 1))
        sin_t = jnp.tile(sin_t, (reps, 1))
        zeros = jnp.zeros((bsz, seq_len - 1, bw), F32)
        hist1 = jnp.concatenate([conv_hist[:, 1:2], zeros], axis=1).reshape(t, bw)
        hist2 = jnp.concatenate([conv_hist, zeros[:, 1:]], axis=1).reshape(t, bw)
        oc, qn, ckv_n, kr_n, vt = _prep(z, lay, cos_t, sin_t, lp["conv_w"], gq, gkr, gkv, seq_len, False, hist1, hist2,
                                        q_dtype=F32)
        new_hist = vt.reshape(bsz, seq_len, bw)[:, seq_len - 2:, :]

    lc = math.gcd(seq_len, 16)
    mats = _s5_mats(lp["ssm_lam_re"], lp["ssm_lam_im"], lp["ssm_log_dt"], lp["ssm_b_re"], lp["ssm_b_im"],
                    lp["ssm_c_re"], lp["ssm_c_im"], lc)
    us = z[:, 3 * bw:4 * bw]
    ys, ht_re, ht_im = _s5(us, h0_re, h0_im, mats, bsz, seq_len, lc)

    gkn = _lane_vec([lp["g_kn"]])
    if prompt:
        k_full, v_all = _kprep(ckv_n, kr_n, lp["w_uk_p"], lp["w_uv_f"], gkn, heads, nope)
        oa = _flash(qn, k_full, v_all, bsz, seq_len, heads, vd, scale)
    else:
        cache_ckv, cache_krope, page_table, layer = sample_ctx
        q4 = qn.reshape(bsz, seq_len, heads, LANES)
        eye = jnp.eye(heads, dtype=F32)
        qbd = (q4[..., :nope][:, :, :, None, :] * eye[None, None, :, :, None]).reshape(bsz, seq_len * heads, heads * nope)
        qr = q4[..., nope:nope + rope].reshape(bsz, seq_len * heads, rope)
        kn = -(-seq_len // 16) * 16
        c_new = jnp.pad(ckv_n.reshape(bsz, seq_len, c_lat), ((0, 0), (0, kn - seq_len), (0, 0)))
        r_new = jnp.pad(kr_n[:, nope:nope + rope].reshape(bsz, seq_len, rope), ((0, 0), (0, kn - seq_len), (0, 0)))
        e_mat = jnp.repeat(jnp.eye(heads, dtype=F32), nope, axis=1).astype(BF16)
        gk = jnp.tile(lp["g_kn"], heads).reshape(1, heads * nope)
        oa = _paged(page_table, qbd, qr, c_new, r_new, cache_ckv, cache_krope, layer, lp["w_uk_f"], lp["w_uv_f"], e_mat,
                    gk, heads, nope, vd, scale).reshape(t, heads * vd)

    x1, h2, qp = _merge(x2d, oc, ys, z, oa, lay, lp["ssm_d"].reshape(1, bw), lp["w_glu_b"], lp["b_glu"].reshape(1, bw),
                        lp["w_br_b"], lp["w_o_b"], lp["g_ffn"].reshape(1, dm), lp["w_pq_b"])
    e_pt, g_pt = _route(qp, lp["peer_keys_b"])
    x2 = _peer(x1, h2, e_pt, g_pt, lp["peer_u"], lp["peer_v"])
    return x2, new_hist, ht_re, ht_im, ckv_n, kr_n[:, nope:nope + rope]


def kernel(x_prompt, x_sample, cache_conv, state_ssm_re, state_ssm_im, cache_ckv, cache_krope, page_table, g_mix, w_in, conv_w, ssm_lam_re, ssm_lam_im, ssm_log_dt, ssm_b_re, ssm_b_im, ssm_c_re, ssm_c_im, ssm_d, w_glu, b_glu, g_kv, w_uk, w_uv, g_qn, g_kn, g_qr, g_kr, w_br, w_o, g_ffn, w_pq, peer_keys, peer_u, peer_v):
    bp, lp_len, dm = x_prompt.shape
    bs, ls, _ = x_sample.shape
    depth = w_in.shape[0]
    bw = conv_w.shape[2]
    c_lat, heads, nope = w_uk.shape[1], w_uk.shape[2], w_uk.shape[3]
    vd = w_uv.shape[3]
    rope = cache_krope.shape[3]
    page = cache_ckv.shape[2]
    past = page_table.shape[1] * page
    n_groups, p_state = ssm_lam_re.shape[1], ssm_lam_re.shape[2]
    assert ssm_d.shape[1] == bw and lp_len % page == 0
    lay = _layout(dm, bw, heads, nope, rope, c_lat)

    pos_p = jnp.arange(lp_len, dtype=jnp.int32)
    pos_s = past + jnp.arange(ls, dtype=jnp.int32)
    xp = x_prompt.reshape(bp * lp_len, dm)
    xs = x_sample.reshape(bs * ls, dm)
    outs = {k: [] for k in ("p_conv", "p_re", "p_im", "p_ckv", "p_kr", "s_conv", "s_re", "s_im", "s_ckv", "s_kr")}
    for l in range(depth):
        lp = dict(
            g_mix=g_mix[l], conv_w=conv_w[l], ssm_lam_re=ssm_lam_re[l], ssm_lam_im=ssm_lam_im[l],
            ssm_log_dt=ssm_log_dt[l], ssm_b_re=ssm_b_re[l], ssm_b_im=ssm_b_im[l], ssm_c_re=ssm_c_re[l],
            ssm_c_im=ssm_c_im[l], ssm_d=ssm_d[l], b_glu=b_glu[l], g_kv=g_kv[l], g_qn=g_qn[l], g_kn=g_kn[l],
            g_qr=g_qr[l], g_kr=g_kr[l], g_ffn=g_ffn[l], peer_u=peer_u[l], peer_v=peer_v[l], w_uv=w_uv[l],
        )
        lp["w_in_p"] = _pack_w_in(w_in[l], lay, dm)
        lp["w_uk_p"] = jnp.pad(w_uk[l], ((0, 0), (0, 0), (0, LANES - nope))).reshape(c_lat, heads * LANES).astype(BF16)
        lp["w_uk_f"] = w_uk[l].reshape(c_lat, heads * nope).astype(BF16)
        lp["w_uv_f"] = w_uv[l].reshape(c_lat, heads * vd).astype(BF16)
        lp["w_glu_b"] = w_glu[l].astype(BF16)
        lp["w_br_b"] = w_br[l].astype(BF16)
        lp["w_o_b"] = w_o[l].astype(BF16)
        lp["w_pq_b"] = w_pq[l].astype(BF16)
        lp["peer_keys_b"] = peer_keys[l].astype(BF16)

        zeros_state = jnp.zeros((bp, n_groups, p_state), F32)
        xp, hc, hr, hi, ck, kr = _layer(xp, bp, lp_len, pos_p, lp, lay, True, None, zeros_state, zeros_state, None)
        outs["p_conv"].append(hc)
        outs["p_re"].append(hr)
        outs["p_im"].append(hi)
        outs["p_ckv"].append(ck.reshape(bp, lp_len // page, page, c_lat))
        outs["p_kr"].append(kr.reshape(bp, lp_len // page, page, rope))
        xs, hc, hr, hi, ck, kr = _layer(xs, bs, ls, pos_s, lp, lay, False, cache_conv[l], state_ssm_re[l],
                                        state_ssm_im[l], (cache_ckv, cache_krope, page_table, l))
        outs["s_conv"].append(hc)
        outs["s_re"].append(hr)
        outs["s_im"].append(hi)
        outs["s_ckv"].append(ck.reshape(bs, ls, c_lat))
        outs["s_kr"].append(kr.reshape(bs, ls, rope))
    st = {k: jnp.stack(v) for k, v in outs.items()}
    return (xp.reshape(bp, lp_len, dm), xs.reshape(bs, ls, dm),
            st["p_conv"], st["p_re"], st["p_im"], st["p_ckv"], st["p_kr"],
            st["s_conv"], st["s_re"], st["s_im"], st["s_ckv"], st["s_kr"])
```

```python
import functools
import math

import jax
import jax.numpy as jnp
from jax import lax
from jax.experimental import pallas as pl
from jax.experimental.pallas import tpu as pltpu

EPS = 1e-6
ROPE_THETA = 10000.0
NEG_INF = -1e30
PEER_TOPK = 16

LANES = 128
SUBLANES = 8
VMEM_LIMIT = 48 * 1024 * 1024
VMEM_LIMIT_TABLE = 56 * 1024 * 1024

F32 = jnp.float32
BF16 = jnp.bfloat16


def _tile(n, pref):
    if n <= pref:
        return n
    t = pref - pref % SUBLANES
    while t >= SUBLANES:
        if n % t == 0:
            return t
        t -= SUBLANES
    return n


def _cparams(sem, limit=VMEM_LIMIT):
    return pltpu.CompilerParams(dimension_semantics=sem, vmem_limit_bytes=limit)


def _proj_kernel(x_ref, g_ref, w_ref, z_ref, h_ref):
    @pl.when(pl.program_id(1) == 0)
    def _():
        x = x_ref[...]
        ms = jnp.mean(x * x, axis=-1, keepdims=True)
        h_ref[...] = (x * lax.rsqrt(ms + EPS) * g_ref[...]).astype(BF16)

    z_ref[...] = jnp.dot(h_ref[...], w_ref[...], preferred_element_type=F32)


def _proj(x2d, g, w_p):
    t, d = x2d.shape
    n = w_p.shape[1]
    tm = _tile(t, 512)
    nb = n // LANES
    k = max(c for c in range(1, nb + 1) if nb % c == 0 and c * LANES <= 2304)
    tn = k * LANES
    return pl.pallas_call(
        _proj_kernel,
        grid=(t // tm, n // tn),
        in_specs=[
            pl.BlockSpec((tm, d), lambda i, j: (i, 0)),
            pl.BlockSpec((1, d), lambda i, j: (0, 0)),
            pl.BlockSpec((d, tn), lambda i, j: (0, j)),
        ],
        out_specs=pl.BlockSpec((tm, tn), lambda i, j: (i, j)),
        out_shape=jax.ShapeDtypeStruct((t, n), F32),
        scratch_shapes=[pltpu.VMEM((tm, d), BF16)],
        compiler_params=_cparams(("parallel", "arbitrary")),
        name="proj",
    )(x2d, g.reshape(1, d), w_p)


def _norm_rope(x, gain, cos, sin, nope, rope):
    lane = lax.broadcasted_iota(jnp.int32, (1, LANES), 1)
    m_n = lane < nope
    m_r = (lane >= nope) & (lane < nope + rope)
    sq = x * x
    ss_r = jnp.sum(jnp.where(m_r, sq, 0.0), axis=-1, keepdims=True)
    inv_r = lax.rsqrt(ss_r / rope + EPS)
    if nope:
        ss_n = jnp.sum(jnp.where(m_n, sq, 0.0), axis=-1, keepdims=True)
        inv = jnp.where(m_n, lax.rsqrt(ss_n / nope + EPS), inv_r)
    else:
        inv = inv_r
    y = x * inv * gain
    half = rope // 2
    first = lane < nope + half
    partner = jnp.where(first, pltpu.roll(y, LANES - half, 1), pltpu.roll(y, half, 1))
    return y * cos + partner * sin


def _prep_kernel(*refs, heads, nope, rope, seq_len, tm, prompt):
    if prompt:
        (xc_ref, bc_ref, cc_ref, q_ref, ckv_ref, kr_ref, cos_ref, sin_ref, cw_ref, gq_ref, gkr_ref, gkv_ref,
         hxc_ref, hcc_ref, oc_ref, qo_ref, ckvo_ref, kro_ref, vt_ref) = refs
    else:
        (xc_ref, bc_ref, cc_ref, q_ref, ckv_ref, kr_ref, cos_ref, sin_ref, cw_ref, gq_ref, gkr_ref, gkv_ref,
         h1_ref, h2_ref, oc_ref, qo_ref, ckvo_ref, kro_ref, vt_ref) = refs
    i = pl.program_id(0)
    v = cc_ref[...] * xc_ref[...]
    row = lax.broadcasted_iota(jnp.int32, (tm, 1), 0)
    r1 = pltpu.roll(v, 1, 0)
    r2 = pltpu.roll(v, 2, 0)
    if prompt:
        hv = hcc_ref[...] * hxc_ref[...]
        hv = jnp.where((i % (seq_len // tm)) == 0, 0.0, hv)
        v1 = jnp.where(row == 0, hv[7:8, :], r1)
        v2 = jnp.where(row == 0, hv[6:7, :], jnp.where(row == 1, hv[7:8, :], r2))
        vt_ref[0] = v[tm - SUBLANES:, :]
    else:
        l = row % seq_len
        v1 = jnp.where(l >= 1, r1, h1_ref[...])
        v2 = jnp.where(l >= 2, r2, h2_ref[...])
        vt_ref[...] = v
    y = v2 * cw_ref[0:1, :] + v1 * cw_ref[1:2, :] + v * cw_ref[2:3, :]
    oc_ref[...] = (bc_ref[...] * y).astype(oc_ref.dtype)

    cos = cos_ref[...]
    sin = sin_ref[...]
    gq = gq_ref[...]
    for h in range(heads):
        sl = slice(LANES * h, LANES * (h + 1))
        qo_ref[:, sl] = _norm_rope(q_ref[:, sl], gq, cos, sin, nope, rope).astype(qo_ref.dtype)
    kro_ref[...] = _kr_norm_rope(kr_ref[...], gkr_ref[...], cos, sin, nope, rope)
    c = ckv_ref[...]
    ms = jnp.mean(c * c, axis=-1, keepdims=True)
    ckvo_ref[...] = c * lax.rsqrt(ms + EPS) * gkv_ref[...]


def _kr_norm_rope(x, gain, cos, sin, nope, rope):
    lane = lax.broadcasted_iota(jnp.int32, (1, LANES), 1)
    ss = jnp.sum(x * x, axis=-1, keepdims=True)
    y = x * lax.rsqrt(ss / rope + EPS) * gain
    half = rope // 2
    first = lane < nope + half
    partner = jnp.where(first, pltpu.roll(y, LANES - half, 1), pltpu.roll(y, half, 1))
    return y * cos + partner * sin


def _prep(z, lay, cos_t, sin_t, conv_w, gq, gkr, gkv, seq_len, prompt, hist1=None, hist2=None, q_dtype=BF16):
    t = z.shape[0]
    bw, heads, c_lat = lay["bw"], lay["heads"], lay["c_lat"]
    hq = heads * LANES
    if prompt:
        tm = _tile(seq_len, 512)
        assert seq_len % tm == 0 and tm >= 2 * SUBLANES
    else:
        tm = _tile(t, 512)
        assert tm % seq_len == 0
    ntab = cos_t.shape[0] // tm
    in_specs = [
        pl.BlockSpec((tm, bw), lambda i: (i, 0)),
        pl.BlockSpec((tm, bw), lambda i: (i, 1)),
        pl.BlockSpec((tm, bw), lambda i: (i, 2)),
        pl.BlockSpec((tm, hq), lambda i: (i, lay["off_q"] // hq)),
        pl.BlockSpec((tm, c_lat), lambda i: (i, lay["off_ckv"] // c_lat)),
        pl.BlockSpec((tm, LANES), lambda i: (i, lay["off_kr"] // LANES)),
        pl.BlockSpec((tm, LANES), lambda i: (i % ntab, 0)),
        pl.BlockSpec((tm, LANES), lambda i: (i % ntab, 0)),
        pl.BlockSpec(conv_w.shape, lambda i: (0, 0)),
        pl.BlockSpec((1, LANES), lambda i: (0, 0)),
        pl.BlockSpec((1, LANES), lambda i: (0, 0)),
        pl.BlockSpec((1, c_lat), lambda i: (0, 0)),
    ]
    args = [z, z, z, z, z, z, cos_t, sin_t, conv_w, gq, gkr, gkv]
    if prompt:
        rb = tm // SUBLANES
        in_specs += [
            pl.BlockSpec((SUBLANES, bw), lambda i: (jnp.maximum(i * rb - 1, 0), 0)),
            pl.BlockSpec((SUBLANES, bw), lambda i: (jnp.maximum(i * rb - 1, 0), 2)),
        ]
        args += [z, z]
        vt_spec = pl.BlockSpec((1, SUBLANES, bw), lambda i: (i, 0, 0))
        vt_shape = jax.ShapeDtypeStruct((t // tm, SUBLANES, bw), F32)
    else:
        in_specs += [pl.BlockSpec((tm, bw), lambda i: (i, 0)), pl.BlockSpec((tm, bw), lambda i: (i, 0))]
        args += [hist1, hist2]
        vt_spec = pl.BlockSpec((tm, bw), lambda i: (i, 0))
        vt_shape = jax.ShapeDtypeStruct((t, bw), F32)
    kern = functools.partial(_prep_kernel, heads=heads, nope=lay["nope"], rope=lay["rope"], seq_len=seq_len, tm=tm,
                             prompt=prompt)
    return pl.pallas_call(
        kern,
        grid=(t // tm,),
        in_specs=in_specs,
        out_specs=[
            pl.BlockSpec((tm, bw), lambda i: (i, 0)),
            pl.BlockSpec((tm, hq), lambda i: (i, 0)),
            pl.BlockSpec((tm, c_lat), lambda i: (i, 0)),
            pl.BlockSpec((tm, LANES), lambda i: (i, 0)),
            vt_spec,
        ],
        out_shape=[
            jax.ShapeDtypeStruct((t, bw), BF16),
            jax.ShapeDtypeStruct((t, hq), q_dtype),
            jax.ShapeDtypeStruct((t, c_lat), F32),
            jax.ShapeDtypeStruct((t, LANES), F32),
            vt_shape,
        ],
        compiler_params=_cparams(("parallel",)),
        name="prep_prompt" if prompt else "prep_sample",
    )(*args)


def _s5_mats(lam_re, lam_im, log_dt, b_re, b_im, c_re, c_im, lc):
    hp = lax.Precision.HIGHEST
    g, p, n_in = b_re.shape
    n_out = c_re.shape[1]
    dt = jnp.exp(log_dt)[:, None]
    lr, li = lam_re, lam_im
    mag = jnp.exp(lr * dt)
    a_re, a_im = mag * jnp.cos(li * dt), mag * jnp.sin(li * dt)
    den = lr * lr + li * li
    f_re = ((a_re - 1.0) * lr + a_im * li) / den
    f_im = (a_im * lr - (a_re - 1.0) * li) / den
    bb_re = f_re[..., None] * b_re - f_im[..., None] * b_im
    bb_im = f_re[..., None] * b_im + f_im[..., None] * b_re
    k = jnp.arange(lc + 1, dtype=F32)[:, None, None]
    pm = jnp.exp(lr[None] * dt[None] * k)
    pr = pm * jnp.cos(li[None] * dt[None] * k)
    pi = pm * jnp.sin(li[None] * dt[None] * k)
    ab_re = pr[:lc, ..., None] * bb_re[None] - pi[:lc, ..., None] * bb_im[None]
    ab_im = pr[:lc, ..., None] * bb_im[None] + pi[:lc, ..., None] * bb_re[None]
    ms = jnp.concatenate([ab_re[::-1], ab_im[::-1]], axis=2)
    ms = jnp.transpose(ms, (1, 0, 3, 2)).reshape(g, lc * n_in, 2 * p)
    kk = (jnp.einsum("gop,dgpi->dgoi", c_re, ab_re, precision=hp)
          - jnp.einsum("gop,dgpi->dgoi", c_im, ab_im, precision=hp))
    s_idx = jnp.arange(lc)[:, None]
    t_idx = jnp.arange(lc)[None, :]
    delta = t_idx - s_idx
    kt = jnp.where((delta >= 0)[:, :, None, None, None], kk[jnp.clip(delta, 0, lc - 1)], 0.0)
    tk = jnp.transpose(kt, (2, 0, 4, 1, 3)).reshape(g, lc * n_in, lc * n_out)
    ca_re = c_re[None] * pr[1:, :, None, :] - c_im[None] * pi[1:, :, None, :]
    ca_im = c_re[None] * pi[1:, :, None, :] + c_im[None] * pr[1:, :, None, :]
    gs = jnp.concatenate([ca_re, -ca_im], axis=3)
    gs = jnp.transpose(gs, (1, 3, 0, 2)).reshape(g, 2 * p, lc * n_out)
    a1 = jnp.concatenate([pr[lc], pr[lc]], axis=-1)[:, None, :]
    a2 = jnp.concatenate([-pi[lc], pi[lc]], axis=-1)[:, None, :]
    return ms.astype(BF16), tk.astype(BF16), gs.astype(BF16), a1, a2


def _s5_kernel(u_ref, tk_ref, ms_ref, gs_ref, a1_ref, a2_ref, h0_ref, y_ref, ht_ref, s_sc, hin_sc, *, n_chunks, bp,
               p_state):
    u = u_ref[...]
    s_sc[...] = jnp.dot(u, ms_ref[...], preferred_element_type=F32)
    a1 = a1_ref[...]
    a2 = a2_ref[...]

    def body(c, h):
        r = pl.multiple_of(c * bp, SUBLANES)
        hin_sc[pl.ds(r, bp), :] = h
        return a1 * h + a2 * pltpu.roll(h, p_state, 1) + s_sc[pl.ds(r, bp), :]

    h = lax.fori_loop(0, n_chunks, body, h0_ref[...])
    ht_ref[...] = h
    y_ref[...] = (jnp.dot(u, tk_ref[...], preferred_element_type=F32)
                  + jnp.dot(hin_sc[...].astype(BF16), gs_ref[...], preferred_element_type=F32))


def _s5(us, h0_re, h0_im, mats, bsz, seq_len, lc):
    ms, tk, gs, a1, a2 = mats
    g, _, p2 = ms.shape
    p_state = p2 // 2
    n_in = ms.shape[1] // lc
    n_out = tk.shape[2] // lc
    n_chunks = seq_len // lc
    bp = -(-bsz // SUBLANES) * SUBLANES
    nc = n_chunks * bp
    u = us.reshape(bsz, n_chunks, lc, g, n_in)
    u = jnp.transpose(u, (3, 1, 0, 2, 4))
    u = jnp.pad(u, ((0, 0), (0, 0), (0, bp - bsz), (0, 0), (0, 0))).reshape(g, nc, lc * n_in).astype(BF16)
    h0 = jnp.concatenate([h0_re, h0_im], axis=-1)
    h0 = jnp.pad(jnp.transpose(h0, (1, 0, 2)), ((0, 0), (0, bp - bsz), (0, 0)))
    kern = functools.partial(_s5_kernel, n_chunks=n_chunks, bp=bp, p_state=p_state)
    y, ht = pl.pallas_call(
        kern,
        grid=(g,),
        in_specs=[
            pl.BlockSpec((None, nc, lc * n_in), lambda i: (i, 0, 0)),
            pl.BlockSpec((None, lc * n_in, lc * n_out), lambda i: (i, 0, 0)),
            pl.BlockSpec((None, lc * n_in, p2), lambda i: (i, 0, 0)),
            pl.BlockSpec((None, p2, lc * n_out), lambda i: (i, 0, 0)),
            pl.BlockSpec((None, 1, p2), lambda i: (i, 0, 0)),
            pl.BlockSpec((None, 1, p2), lambda i: (i, 0, 0)),
            pl.BlockSpec((None, bp, p2), lambda i: (i, 0, 0)),
        ],
        out_specs=[
            pl.BlockSpec((None, nc, lc * n_out), lambda i: (i, 0, 0)),
            pl.BlockSpec((None, bp, p2), lambda i: (i, 0, 0)),
        ],
        out_shape=[
            jax.ShapeDtypeStruct((g, nc, lc * n_out), F32),
            jax.ShapeDtypeStruct((g, bp, p2), F32),
        ],
        scratch_shapes=[pltpu.VMEM((nc, p2), F32), pltpu.VMEM((nc, p2), F32)],
        compiler_params=_cparams(("parallel",)),
        name="s5",
    )(u, tk, ms, gs, a1, a2, h0)
    y = y.reshape(g, n_chunks, bp, lc, n_out)[:, :, :bsz]
    y = jnp.transpose(y, (2, 1, 3, 0, 4)).reshape(bsz * seq_len, g * n_out)
    ht = jnp.transpose(ht[:, :bsz], (1, 0, 2))
    return y, ht[..., :p_state], ht[..., p_state:]


def _kprep_kernel(ckv_ref, kr_ref, wuk_ref, wuv_ref, gkn_ref, k_ref, v_ref, *, heads, nope):
    c = ckv_ref[...].astype(BF16)
    kraw = jnp.dot(c, wuk_ref[...], preferred_element_type=F32)
    kr = kr_ref[...]
    gkn = gkn_ref[...]
    for h in range(heads):
        sl = slice(LANES * h, LANES * (h + 1))
        kh = kraw[:, sl]
        ss = jnp.sum(kh * kh, axis=-1, keepdims=True)
        k_ref[:, sl] = (kh * lax.rsqrt(ss / nope + EPS) * gkn + kr).astype(BF16)
    v_ref[...] = jnp.dot(c, wuv_ref[...], preferred_element_type=F32).astype(BF16)


def _kprep(ckv_n, kr_n, wuk_p, wuv, gkn, heads, nope):
    t, c_lat = ckv_n.shape
    tm = _tile(t, 512)
    hk = wuk_p.shape[1]
    hv = wuv.shape[1]
    return pl.pallas_call(
        functools.partial(_kprep_kernel, heads=heads, nope=nope),
        grid=(t // tm,),
        in_specs=[
            pl.BlockSpec((tm, c_lat), lambda i: (i, 0)),
            pl.BlockSpec((tm, LANES), lambda i: (i, 0)),
            pl.BlockSpec((c_lat, hk), lambda i: (0, 0)),
            pl.BlockSpec((c_lat, hv), lambda i: (0, 0)),
            pl.BlockSpec((1, LANES), lambda i: (0, 0)),
        ],
        out_specs=[pl.BlockSpec((tm, hk), lambda i: (i, 0)), pl.BlockSpec((tm, hv), lambda i: (i, 0))],
        out_shape=[jax.ShapeDtypeStruct((t, hk), BF16), jax.ShapeDtypeStruct((t, hv), BF16)],
        compiler_params=_cparams(("parallel",)),
        name="kprep",
    )(ckv_n, kr_n, wuk_p, wuv, gkn)


def _flash_kernel(q_ref, k_ref, v_ref, o_ref, m_sc, l_sc, acc_sc, *, c_exp, tq, hps, vd):
    i = pl.program_id(2)
    j = pl.program_id(3)
    nct = tq // LANES

    @pl.when(j == 0)
    def _():
        m_sc[...] = jnp.full(m_sc.shape, NEG_INF, F32)
        l_sc[...] = jnp.zeros(l_sc.shape, F32)
        acc_sc[...] = jnp.zeros(acc_sc.shape, F32)

    def step(diagonal):
        if diagonal:
            row = lax.broadcasted_iota(jnp.int32, (tq, tq), 0)
            col = lax.broadcasted_iota(jnp.int32, (tq, tq), 1)
            keep = col <= row
        for hh in range(hps):
            q = q_ref[:, LANES * hh:LANES * (hh + 1)]
            k = k_ref[:, LANES * hh:LANES * (hh + 1)]
            s = lax.dot_general(q, k, (((1,), (1,)), ((), ())), preferred_element_type=F32)
            if diagonal:
                s = jnp.where(keep, s, NEG_INF)
            m_prev = m_sc[hh]
            m_new = jnp.maximum(m_prev, jnp.max(s, axis=-1, keepdims=True))
            alpha = jnp.exp2((m_prev - m_new) * c_exp)
            ps = [jnp.exp2((s[:, LANES * c:LANES * (c + 1)] - m_new) * c_exp) for c in range(nct)]
            psum = ps[0]
            for c in range(1, nct):
                psum = psum + ps[c]
            l_sc[hh] = alpha * l_sc[hh] + jnp.sum(psum, axis=-1, keepdims=True)
            p = jnp.concatenate(ps, axis=1).astype(BF16)
            acc_sc[hh] = alpha[:, :vd] * acc_sc[hh] + jnp.dot(p, v_ref[:, vd * hh:vd * (hh + 1)],
                                                              preferred_element_type=F32)
            m_sc[hh] = m_new

    @pl.when(j < i)
    def _():
        step(False)

    @pl.when(j == i)
    def _():
        step(True)

    @pl.when(j == pl.num_programs(3) - 1)
    def _():
        for hh in range(hps):
            o_ref[:, vd * hh:vd * (hh + 1)] = (acc_sc[hh] / l_sc[hh][:, :vd]).astype(o_ref.dtype)


def _flash(q, k, v, bsz, seq_len, heads, vd, scale):
    hps = LANES // vd
    assert heads % hps == 0
    tq = _tile(seq_len, 512)
    nq = seq_len // tq
    t = bsz * seq_len
    assert tq % LANES == 0
    kern = functools.partial(_flash_kernel, c_exp=scale * math.log2(math.e), tq=tq, hps=hps, vd=vd)
    return pl.pallas_call(
        kern,
        grid=(bsz, heads // hps, nq, nq),
        in_specs=[
            pl.BlockSpec((tq, hps * LANES), lambda b, h, i, j: (b * nq + i, h)),
            pl.BlockSpec((tq, hps * LANES), lambda b, h, i, j: (b * nq + jnp.minimum(i, j), h)),
            pl.BlockSpec((tq, LANES), lambda b, h, i, j: (b * nq + jnp.minimum(i, j), h)),
        ],
        out_specs=pl.BlockSpec((tq, LANES), lambda b, h, i, j: (b * nq + i, h)),
        out_shape=jax.ShapeDtypeStruct((t, heads * vd), BF16),
        scratch_shapes=[pltpu.VMEM((hps, tq, LANES), F32), pltpu.VMEM((hps, tq, LANES), F32),
                        pltpu.VMEM((hps, tq, vd), F32)],
        compiler_params=_cparams(("parallel", "parallel", "parallel", "arbitrary")),
        name="flash",
    )(q, k, v)


def _paged_kernel(pt_ref, qbd_ref, qr_ref, cn_ref, rn_ref, *refs, pp, heads, nope, vd, lq, scale):
    del pt_ref
    c_pages = refs[:pp]
    r_pages = refs[pp:2 * pp]
    wuk_ref, wuv_ref, e_ref, gk_ref, o_ref, m_sc, l_sc, acc_sc, cb_sc, rb_sc = refs[2 * pp:]
    s_id = pl.program_id(1)
    rows = lq * heads
    page = c_pages[0].shape[0]

    @pl.when(s_id == 0)
    def _():
        m_sc[...] = jnp.full(m_sc.shape, NEG_INF, F32)
        l_sc[...] = jnp.zeros(l_sc.shape, F32)
        acc_sc[...] = jnp.zeros(acc_sc.shape, F32)

    qbd = (qbd_ref[...] * gk_ref[...]).astype(BF16)
    qr = qr_ref[...].astype(BF16)
    dn = (((1,), (1,)), ((), ()))

    def scores(cb, rb):
        kraw = jnp.dot(cb, wuk_ref[...], preferred_element_type=F32)
        ssq = lax.dot_general(e_ref[...], (kraw * kraw).astype(BF16), dn, preferred_element_type=F32)
        inv = lax.rsqrt(ssq / nope + EPS)
        inv = jnp.concatenate([inv] * lq, axis=0)
        sn = lax.dot_general(qbd, kraw.astype(BF16), dn, preferred_element_type=F32)
        sr = lax.dot_general(qr, rb, dn, preferred_element_type=F32)
        return (sn * inv + sr) * scale

    def update(cb, rb, causal):
        s = scores(cb, rb)
        if causal:
            nk = cb.shape[0]
            kk = lax.broadcasted_iota(jnp.int32, (rows, nk), 1)
            qq = lax.broadcasted_iota(jnp.int32, (rows, nk), 0) // heads
            s = jnp.where(kk <= qq, s, NEG_INF)
        m_prev = m_sc[...]
        m_new = jnp.maximum(m_prev, jnp.max(s, axis=-1, keepdims=True))
        alpha = jnp.exp(m_prev - m_new)
        p = jnp.exp(s - m_new)
        l_sc[...] = alpha * l_sc[...] + jnp.sum(p, axis=-1, keepdims=True)
        acc_sc[...] = alpha * acc_sc[...] + jnp.dot(p.astype(BF16), cb, preferred_element_type=F32)
        m_sc[...] = m_new

    for pg in range(pp):
        cb_sc[page * pg:page * (pg + 1), :] = c_pages[pg][...].astype(BF16)
        rb_sc[page * pg:page * (pg + 1), :] = r_pages[pg][...].astype(BF16)
    update(cb_sc[...], rb_sc[...], False)

    @pl.when(s_id == pl.num_programs(1) - 1)
    def _():
        update(cn_ref[...].astype(BF16), rn_ref[...].astype(BF16), True)
        lat = (acc_sc[...] / l_sc[...]).astype(BF16)
        full = jnp.dot(lat, wuv_ref[...], preferred_element_type=F32)
        colh = lax.broadcasted_iota(jnp.int32, full.shape, 1) // vd
        rowh = lax.broadcasted_iota(jnp.int32, full.shape, 0) % heads
        full = jnp.where(colh == rowh, full, 0.0)
        o_ref[...] = jnp.sum(full.reshape(lq, heads, heads * vd), axis=1)


def _paged(page_table, qbd, qr, c_new, r_new, cache_ckv, cache_krope, layer, wuk, wuv, e_mat, gk, heads, nope, vd,
           scale):
    bs, rows, _ = qbd.shape
    lq = rows // heads
    n_pages = page_table.shape[1]
    page, c_lat = cache_ckv.shape[2], cache_ckv.shape[3]
    rope = cache_krope.shape[3]
    pp = math.gcd(n_pages, 16)
    kn = c_new.shape[1]

    def cmap(p):
        return lambda b, s, pt: (layer, pt[b, s * pp + p], 0, 0)

    in_specs = [
        pl.BlockSpec((None, rows, heads * nope), lambda b, s, pt: (b, 0, 0)),
        pl.BlockSpec((None, rows, rope), lambda b, s, pt: (b, 0, 0)),
        pl.BlockSpec((None, kn, c_lat), lambda b, s, pt: (b, 0, 0)),
        pl.BlockSpec((None, kn, rope), lambda b, s, pt: (b, 0, 0)),
    ]
    in_specs += [pl.BlockSpec((None, None, page, c_lat), cmap(p)) for p in range(pp)]
    in_specs += [pl.BlockSpec((None, None, page, rope), cmap(p)) for p in range(pp)]
    in_specs += [
        pl.BlockSpec(wuk.shape, lambda b, s, pt: (0, 0)),
        pl.BlockSpec(wuv.shape, lambda b, s, pt: (0, 0)),
        pl.BlockSpec(e_mat.shape, lambda b, s, pt: (0, 0)),
        pl.BlockSpec(gk.shape, lambda b, s, pt: (0, 0)),
    ]
    kern = functools.partial(_paged_kernel, pp=pp, heads=heads, nope=nope, vd=vd, lq=lq, scale=scale)
    grid_spec = pltpu.PrefetchScalarGridSpec(
        num_scalar_prefetch=1,
        grid=(bs, n_pages // pp),
        in_specs=in_specs,
        out_specs=pl.BlockSpec((None, lq, heads * vd), lambda b, s, pt: (b, 0, 0)),
        scratch_shapes=[pltpu.VMEM((rows, 1), F32), pltpu.VMEM((rows, 1), F32), pltpu.VMEM((rows, c_lat), F32),
                        pltpu.VMEM((pp * page, c_lat), BF16), pltpu.VMEM((pp * page, rope), BF16)],
    )
    return pl.pallas_call(
        kern,
        grid_spec=grid_spec,
        out_shape=jax.ShapeDtypeStruct((bs, lq, heads * vd), F32),
        compiler_params=_cparams(("parallel", "arbitrary")),
        name="paged",
    )(page_table, qbd, qr, c_new, r_new, *([cache_ckv] * pp), *([cache_krope] * pp), wuk, wuv, e_mat, gk)


def _merge_kernel(x_ref, oc_ref, ys_ref, us_ref, oa_ref, g0_ref, g1_ref, g2_ref, d_ref, wglu_ref, bglu_ref, wbr_ref,
                  wo_ref, gffn_ref, wpq_ref, x1_ref, h2_ref, qp_ref):
    y = ys_ref[...] + d_ref[...] * us_ref[...]
    zg = jax.nn.gelu(y)
    gl = jnp.dot(zg.astype(BF16), wglu_ref[...], preferred_element_type=F32) + bglu_ref[...]
    o_ssm = zg * jax.nn.sigmoid(gl)
    merged = jax.nn.sigmoid(g0_ref[...]) * jnp.dot(oc_ref[...], wbr_ref[0], preferred_element_type=F32)
    merged += jax.nn.sigmoid(g1_ref[...]) * jnp.dot(o_ssm.astype(BF16), wbr_ref[1], preferred_element_type=F32)
    merged += jax.nn.sigmoid(g2_ref[...]) * jnp.dot(oa_ref[...].astype(BF16), wbr_ref[2], preferred_element_type=F32)
    x1 = x_ref[...] + jnp.dot(merged.astype(BF16), wo_ref[...], preferred_element_type=F32)
    x1_ref[...] = x1
    ms = jnp.mean(x1 * x1, axis=-1, keepdims=True)
    h2 = x1 * lax.rsqrt(ms + EPS) * gffn_ref[...]
    h2_ref[...] = h2
    qp_ref[...] = jnp.dot(h2.astype(BF16), wpq_ref[...], preferred_element_type=F32)


def _merge(x2d, oc, ys, z, oa, lay, d, wglu, bglu, wbr, wo, gffn, wpq):
    t, dm = x2d.shape
    bw = lay["bw"]
    dq = wpq.shape[1]
    tm = _tile(t, 256)
    gi = lay["off_g"] // dm

    def full(a):
        return pl.BlockSpec(a.shape, lambda i, _n=a.ndim: (0,) * _n)

    return pl.pallas_call(
        _merge_kernel,
        grid=(t // tm,),
        in_specs=[
            pl.BlockSpec((tm, dm), lambda i: (i, 0)),
            pl.BlockSpec((tm, bw), lambda i: (i, 0)),
            pl.BlockSpec((tm, bw), lambda i: (i, 0)),
            pl.BlockSpec((tm, bw), lambda i: (i, 3)),
            pl.BlockSpec((tm, bw), lambda i: (i, 0)),
            pl.BlockSpec((tm, dm), lambda i: (i, gi)),
            pl.BlockSpec((tm, dm), lambda i: (i, gi + 1)),
            pl.BlockSpec((tm, dm), lambda i: (i, gi + 2)),
            full(d), full(wglu), full(bglu), full(wbr), full(wo), full(gffn), full(wpq),
        ],
        out_specs=[
            pl.BlockSpec((tm, dm), lambda i: (i, 0)),
            pl.BlockSpec((tm, dm), lambda i: (i, 0)),
            pl.BlockSpec((tm, dq), lambda i: (i, 0)),
        ],
        out_shape=[
            jax.ShapeDtypeStruct((t, dm), F32),
            jax.ShapeDtypeStruct((t, dm), F32),
            jax.ShapeDtypeStruct((t, dq), F32),
        ],
        compiler_params=_cparams(("parallel",)),
        name="merge",
    )(x2d, oc, ys, z, oa, z, z, z, d, wglu, bglu, wbr, wo, gffn, wpq)


def _topk_rows(x, k):
    n = x.shape[0]
    iota = lax.broadcasted_iota(jnp.int32, x.shape, 0)
    vals, idxs = [], []
    for _ in range(k):
        m = jnp.max(x, axis=0, keepdims=True)
        am = jnp.min(jnp.where(x == m, iota, n), axis=0, keepdims=True)
        vals.append(m)
        idxs.append(am)
        x = jnp.where(iota == am, -jnp.inf, x)
    return jnp.concatenate(vals, axis=0), jnp.concatenate(idxs, axis=0)


def _select_rows(table, sel, k):
    out = jnp.zeros(sel.shape, table.dtype)
    for r in range(k):
        out = jnp.where(sel == r, table[r:r + 1, :], out)
    return out


def _pair_candidates(v1, v2, k):
    chunks, meta, r0, i = [], [], 0, 0
    while i < k and k // (i + 1) >= 2:
        n = k // (i + 1)
        nr = -(-n // SUBLANES) * SUBLANES
        blk = v1[i:i + 1, :] + v2[0:nr, :]
        if n < nr:
            blk = jnp.where(lax.broadcasted_iota(jnp.int32, blk.shape, 0) < n, blk, -jnp.inf)
        chunks.append(blk)
        meta.append((r0, nr, i, None))
        r0 += nr
        i += 1
    assert (k - i) % SUBLANES == 0
    chunks.append(v1[i:k, :] + v2[0:1, :])
    meta.append((r0, k - i, None, i))
    return jnp.concatenate(chunks, axis=0), meta


def _route_kernel(qp_ref, keys_ref, e_ref, g_ref, *, heads, n_keys, dk, topk, row_mult):
    dn = (((1,), (1,)), ((), ()))
    for h in range(heads):
        sub = []
        for s in range(2):
            o = (2 * h + s) * dk
            qs = qp_ref[:, o:o + dk].astype(BF16)
            st = lax.dot_general(keys_ref[h, s], qs, dn, preferred_element_type=F32)
            sub.append(_topk_rows(st, topk))
        (v1, i1), (v2, i2) = sub
        cand, meta = _pair_candidates(v1, v2, topk)
        sc, ci = _topk_rows(cand, topk)
        ihi = jnp.zeros(ci.shape, jnp.int32)
        jlo = jnp.zeros(ci.shape, jnp.int32)
        for r0, nr, ic, i0 in meta:
            inr = (ci >= r0) & (ci < r0 + nr)
            if ic is None:
                ihi = jnp.where(inr, ci - r0 + i0, ihi)
            else:
                ihi = jnp.where(inr, ic, ihi)
                jlo = jnp.where(inr, ci - r0, jlo)
        e1 = _select_rows(i1, ihi, topk)
        e2 = _select_rows(i2, jlo, topk)
        ex = jnp.exp(sc - sc[0:1, :])
        e_ref[topk * h:topk * (h + 1), :] = (e1 * n_keys + e2) * row_mult
        g_ref[topk * h:topk * (h + 1), :] = ex / jnp.sum(ex, axis=0, keepdims=True)


def _route(qp, keys_bf16, row_mult):
    t, dq = qp.shape
    heads, _, n_keys, dk = keys_bf16.shape
    tm = _tile(t, 128)
    if t % LANES:
        tm = t
    npair = heads * PEER_TOPK
    kern = functools.partial(_route_kernel, heads=heads, n_keys=n_keys, dk=dk, topk=PEER_TOPK, row_mult=row_mult)
    return pl.pallas_call(
        kern,
        grid=(t // tm,),
        in_specs=[
            pl.BlockSpec((tm, dq), lambda i: (i, 0)),
            pl.BlockSpec(keys_bf16.shape, lambda i: (0, 0, 0, 0)),
        ],
        out_specs=[pl.BlockSpec((npair, tm), lambda i: (0, i)), pl.BlockSpec((npair, tm), lambda i: (0, i))],
        out_shape=[jax.ShapeDtypeStruct((npair, t), jnp.int32), jax.ShapeDtypeStruct((npair, t), F32)],
        compiler_params=_cparams(("parallel",)),
        name="route",
    )(qp, keys_bf16)


HI_MASK = 0xFFFF0000
_BUTTERFLY_ORDER = (0, 4, 2, 6, 1, 5, 3, 7)


def _pack_table(tab):
    e, d = tab.shape
    half = d // 2
    b = lax.bitcast_convert_type(tab.astype(BF16), jnp.uint16).astype(jnp.uint32)
    return (b[:, :half] | (b[:, half:] << 16)).reshape(e * (half // LANES), LANES)


def _unpack(x):
    return pltpu.bitcast(x << 16, F32), pltpu.bitcast(x & jnp.uint32(HI_MASK), F32)


def _fold_sublanes(x, y, k, mask):
    return jnp.where(mask, x, pltpu.roll(y, k, 0)) + jnp.where(mask, pltpu.roll(x, SUBLANES - k, 0), y)


def _peer_u_kernel(e_ref, h_ref, tab_ref, a_ref, *, tb, npair, rows):
    sub = lax.broadcasted_iota(jnp.int32, (SUBLANES, LANES), 0)
    lane = lax.broadcasted_iota(jnp.int32, (SUBLANES, LANES), 1)
    m2 = (sub % 4) < 2
    m1 = (sub % 2) == 0
    ng = npair // SUBLANES

    def body(t, carry):
        th_lo = h_ref[t, 0]
        th_hi = h_ref[t, 1]
        out = jnp.zeros((SUBLANES, LANES), F32)
        for g in range(ng):
            ps = []
            for j in _BUTTERFLY_ORDER:
                idx = pl.multiple_of(e_ref[t, SUBLANES * g + j], rows)
                lo, hi = _unpack(tab_ref[pl.ds(idx, rows), :])
                ps.append(lo * th_lo + hi * th_hi)
            v = [jnp.concatenate([ps[2 * k], ps[2 * k + 1]], axis=0) for k in range(4)]
            r = _fold_sublanes(_fold_sublanes(v[0], v[1], 2, m2), _fold_sublanes(v[2], v[3], 2, m2), 1, m1)
            out = jnp.where(lane == g, jnp.sum(r, axis=-1, keepdims=True), out)
        a_ref[t] = out[:, :ng]
        return carry

    lax.fori_loop(0, tb, body, 0, unroll=4)


def _peer_v_kernel(e_ref, w_ref, x_ref, tab_ref, o_ref, *, tb, npair, rows):
    def body(t, carry):
        lo = [jnp.zeros((rows, LANES), F32) for _ in range(2)]
        hi = [jnp.zeros((rows, LANES), F32) for _ in range(2)]
        for j in range(npair):
            idx = pl.multiple_of(e_ref[t, j], rows)
            w = w_ref[t, j]
            x_lo, x_hi = _unpack(tab_ref[pl.ds(idx, rows), :])
            lo[j % 2] = lo[j % 2] + w * x_lo
            hi[j % 2] = hi[j % 2] + w * x_hi
        o_ref[t, 0] = x_ref[t, 0] + (lo[0] + lo[1])
        o_ref[t, 1] = x_ref[t, 1] + (hi[0] + hi[1])
        return carry

    lax.fori_loop(0, tb, body, 0, unroll=2)


def _peer_w_kernel(a_ref, g_ref, w_ref):
    w_ref[...] = g_ref[...] * jax.nn.gelu(a_ref[...])


def _peer(x1, h2, e_pt, g_pt, u_pk, v_pk):
    t, dm = x1.shape
    npair = e_pt.shape[0]
    rows = dm // (2 * LANES)
    assert 2 * rows == SUBLANES and npair % SUBLANES == 0
    tb = _tile(t, 64)
    nblk = t // tb
    ng = npair // SUBLANES
    e_nat = jnp.transpose(e_pt)
    e_u = jnp.transpose(e_nat.reshape(t, SUBLANES, ng), (0, 2, 1)).reshape(nblk, tb, npair)
    e_v = e_nat.reshape(nblk, tb, npair)
    g_nat = jnp.transpose(g_pt)
    smem = functools.partial(pl.BlockSpec, memory_space=pltpu.SMEM)
    tab_spec = pl.BlockSpec(u_pk.shape, lambda i: (0, 0), pipeline_mode=pl.Buffered(1))
    tok_spec = pl.BlockSpec((tb, 2, rows, LANES), lambda i: (i, 0, 0, 0))
    idx_spec = smem((None, tb, npair), lambda i: (i, 0, 0))
    a = pl.pallas_call(
        functools.partial(_peer_u_kernel, tb=tb, npair=npair, rows=rows),
        grid=(nblk,),
        in_specs=[idx_spec, tok_spec, tab_spec],
        out_specs=pl.BlockSpec((tb, SUBLANES, ng), lambda i: (i, 0, 0)),
        out_shape=jax.ShapeDtypeStruct((t, SUBLANES, ng), F32),
        compiler_params=_cparams(("arbitrary",), VMEM_LIMIT_TABLE),
        name="peer_u",
    )(e_u, h2.reshape(t, 2, rows, LANES), u_pk)
    tw = _tile(t, 512)
    w = pl.pallas_call(
        _peer_w_kernel,
        grid=(t // tw,),
        in_specs=[pl.BlockSpec((tw, npair), lambda i: (i, 0)), pl.BlockSpec((tw, npair), lambda i: (i, 0))],
        out_specs=pl.BlockSpec((tw, npair), lambda i: (i, 0)),
        out_shape=jax.ShapeDtypeStruct((t, npair), F32),
        compiler_params=_cparams(("parallel",)),
        name="peer_w",
    )(a.reshape(t, npair), g_nat)
    out = pl.pallas_call(
        functools.partial(_peer_v_kernel, tb=tb, npair=npair, rows=rows),
        grid=(nblk,),
        in_specs=[idx_spec, idx_spec, tok_spec, tab_spec],
        out_specs=tok_spec,
        out_shape=jax.ShapeDtypeStruct((t, 2, rows, LANES), F32),
        compiler_params=_cparams(("arbitrary",), VMEM_LIMIT_TABLE),
        name="peer_v",
    )(e_v, w.reshape(nblk, tb, npair), x1.reshape(t, 2, rows, LANES), v_pk)
    return out.reshape(t, dm)


def _layout(dm, bw, heads, nope, rope, c_lat):
    hq = heads * LANES
    lay = dict(bw=bw, heads=heads, nope=nope, rope=rope, c_lat=c_lat)
    lay["off_q"] = 4 * bw
    lay["off_g"] = lay["off_q"] + hq
    lay["off_ckv"] = lay["off_g"] + 3 * dm
    lay["off_kr"] = lay["off_ckv"] + c_lat
    lay["n"] = lay["off_kr"] + LANES
    assert nope + rope <= LANES and rope % 2 == 0
    assert lay["off_q"] % hq == 0 and lay["off_g"] % dm == 0 and lay["off_ckv"] % c_lat == 0
    assert bw % LANES == 0 and c_lat % LANES == 0
    return lay


def _pack_w_in(w_in, lay, dm):
    bw, heads, nope, rope, c_lat = lay["bw"], lay["heads"], lay["nope"], lay["rope"], lay["c_lat"]
    sizes = (bw, bw, bw, bw, heads * (nope + rope), c_lat, rope, 3 * dm)
    parts, off = [], 0
    for n in sizes:
        parts.append(w_in[:, off:off + n])
        off += n
    xc, bc, cc, us, q, ckv, kr, gates = parts
    q = jnp.pad(q.reshape(dm, heads, nope + rope), ((0, 0), (0, 0), (0, LANES - nope - rope))).reshape(dm, heads * LANES)
    kr = jnp.pad(kr, ((0, 0), (nope, LANES - nope - rope)))
    return jnp.concatenate([xc, bc, cc, us, q, gates, ckv, kr], axis=1).astype(BF16)


def _rope_tables(pos, nope, rope):
    half = rope // 2
    inv = ROPE_THETA ** (-jnp.arange(half, dtype=F32) / half)
    ang = pos.astype(F32)[:, None] * inv[None, :]
    cos, sin = jnp.cos(ang), jnp.sin(ang)
    n = pos.shape[0]
    ones = jnp.ones((n, nope), F32)
    tail = LANES - nope - rope
    cos_t = jnp.concatenate([ones, cos, cos, jnp.ones((n, tail), F32)], axis=1)
    sin_t = jnp.concatenate([jnp.zeros((n, nope), F32), -sin, sin, jnp.zeros((n, tail), F32)], axis=1)
    return cos_t, sin_t


def _lane_vec(parts):
    v = jnp.concatenate(parts)
    return jnp.pad(v, (0, LANES - v.shape[0])).reshape(1, LANES)


def _layer(x2d, bsz, seq_len, pos, lp, lay, prompt, conv_hist, h0_re, h0_im, sample_ctx):
    t, dm = x2d.shape
    bw, heads, nope, rope, c_lat = lay["bw"], lay["heads"], lay["nope"], lay["rope"], lay["c_lat"]
    vd = lp["w_uv"].shape[2]
    scale = (nope + rope) ** -0.5

    z = _proj(x2d, lp["g_mix"], lp["w_in_p"])

    cos_t, sin_t = _rope_tables(pos, nope, rope)
    gq = _lane_vec([lp["g_qn"], lp["g_qr"]])
    gkr = _lane_vec([jnp.zeros((nope,), F32), lp["g_kr"]])
    gkv = lp["g_kv"].reshape(1, c_lat)
    if prompt:
        oc, qn, ckv_n, kr_n, vt = _prep(z, lay, cos_t, sin_t, lp["conv_w"], gq, gkr, gkv, seq_len, True)
        nblk_seq = seq_len // (t // vt.shape[0])
        new_hist = vt.reshape(bsz, nblk_seq, SUBLANES, bw)[:, -1, SUBLANES - 2:, :]
    else:
        tm = _tile(t, 512)
        reps = tm // seq_len
        cos_t = jnp.tile(cos_t, (reps, 1))
        sin_t = jnp.tile(sin_t, (reps, 1))
        zeros = jnp.zeros((bsz, seq_len - 1, bw), F32)
        hist1 = jnp.concatenate([conv_hist[:, 1:2], zeros], axis=1).reshape(t, bw)
        hist2 = jnp.concatenate([conv_hist, zeros[:, 1:]], axis=1).reshape(t, bw)
        oc, qn, ckv_n, kr_n, vt = _prep(z, lay, cos_t, sin_t, lp["conv_w"], gq, gkr, gkv, seq_len, False, hist1, hist2,
                                        q_dtype=F32)
        new_hist = vt.reshape(bsz, seq_len, bw)[:, seq_len - 2:, :]

    lc = math.gcd(seq_len, 16)
    mats = _s5_mats(lp["ssm_lam_re"], lp["ssm_lam_im"], lp["ssm_log_dt"], lp["ssm_b_re"], lp["ssm_b_im"],
                    lp["ssm_c_re"], lp["ssm_c_im"], lc)
    us = z[:, 3 * bw:4 * bw]
    ys, ht_re, ht_im = _s5(us, h0_re, h0_im, mats, bsz, seq_len, lc)

    gkn = _lane_vec([lp["g_kn"]])
    if prompt:
        k_full, v_all = _kprep(ckv_n, kr_n, lp["w_uk_p"], lp["w_uv_f"], gkn, heads, nope)
        oa = _flash(qn, k_full, v_all, bsz, seq_len, heads, vd, scale)
    else:
        cache_ckv, cache_krope, page_table, layer = sample_ctx
        q4 = qn.reshape(bsz, seq_len, heads, LANES)
        eye = jnp.eye(heads, dtype=F32)
        qbd = (q4[..., :nope][:, :, :, None, :] * eye[None, None, :, :, None]).reshape(bsz, seq_len * heads, heads * nope)
        qr = q4[..., nope:nope + rope].reshape(bsz, seq_len * heads, rope)
        kn = -(-seq_len // 16) * 16
        c_new = jnp.pad(ckv_n.reshape(bsz, seq_len, c_lat), ((0, 0), (0, kn - seq_len), (0, 0)))
        r_new = jnp.pad(kr_n[:, nope:nope + rope].reshape(bsz, seq_len, rope), ((0, 0), (0, kn - seq_len), (0, 0)))
        e_mat = jnp.repeat(jnp.eye(heads, dtype=F32), nope, axis=1).astype(BF16)
        gk = jnp.tile(lp["g_kn"], heads).reshape(1, heads * nope)
        oa = _paged(page_table, qbd, qr, c_new, r_new, cache_ckv, cache_krope, layer, lp["w_uk_f"], lp["w_uv_f"], e_mat,
                    gk, heads, nope, vd, scale).reshape(t, heads * vd)

    x1, h2, qp = _merge(x2d, oc, ys, z, oa, lay, lp["ssm_d"].reshape(1, bw), lp["w_glu_b"], lp["b_glu"].reshape(1, bw),
                        lp["w_br_b"], lp["w_o_b"], lp["g_ffn"].reshape(1, dm), lp["w_pq_b"])
    e_pt, g_pt = _route(qp, lp["peer_keys_b"], dm // (2 * LANES))
    x2 = _peer(x1, h2, e_pt, g_pt, lp["peer_u_pk"], lp["peer_v_pk"])
    return x2, new_hist, ht_re, ht_im, ckv_n, kr_n[:, nope:nope + rope]


def kernel(x_prompt, x_sample, cache_conv, state_ssm_re, state_ssm_im, cache_ckv, cache_krope, page_table, g_mix, w_in, conv_w, ssm_lam_re, ssm_lam_im, ssm_log_dt, ssm_b_re, ssm_b_im, ssm_c_re, ssm_c_im, ssm_d, w_glu, b_glu, g_kv, w_uk, w_uv, g_qn, g_kn, g_qr, g_kr, w_br, w_o, g_ffn, w_pq, peer_keys, peer_u, peer_v):
    bp, lp_len, dm = x_prompt.shape
    bs, ls, _ = x_sample.shape
    depth = w_in.shape[0]
    bw = conv_w.shape[2]
    c_lat, heads, nope = w_uk.shape[1], w_uk.shape[2], w_uk.shape[3]
    vd = w_uv.shape[3]
    rope = cache_krope.shape[3]
    page = cache_ckv.shape[2]
    past = page_table.shape[1] * page
    n_groups, p_state = ssm_lam_re.shape[1], ssm_lam_re.shape[2]
    assert ssm_d.shape[1] == bw and lp_len % page == 0
    lay = _layout(dm, bw, heads, nope, rope, c_lat)

    pos_p = jnp.arange(lp_len, dtype=jnp.int32)
    pos_s = past + jnp.arange(ls, dtype=jnp.int32)
    xp = x_prompt.reshape(bp * lp_len, dm)
    xs = x_sample.reshape(bs * ls, dm)
    outs = {k: [] for k in ("p_conv", "p_re", "p_im", "p_ckv", "p_kr", "s_conv", "s_re", "s_im", "s_ckv", "s_kr")}
    for l in range(depth):
        lp = dict(
            g_mix=g_mix[l], conv_w=conv_w[l], ssm_lam_re=ssm_lam_re[l], ssm_lam_im=ssm_lam_im[l],
            ssm_log_dt=ssm_log_dt[l], ssm_b_re=ssm_b_re[l], ssm_b_im=ssm_b_im[l], ssm_c_re=ssm_c_re[l],
            ssm_c_im=ssm_c_im[l], ssm_d=ssm_d[l], b_glu=b_glu[l], g_kv=g_kv[l], g_qn=g_qn[l], g_kn=g_kn[l],
            g_qr=g_qr[l], g_kr=g_kr[l], g_ffn=g_ffn[l], w_uv=w_uv[l],
        )
        lp["peer_u_pk"] = _pack_table(peer_u[l])
        lp["peer_v_pk"] = _pack_table(peer_v[l])
        lp["w_in_p"] = _pack_w_in(w_in[l], lay, dm)
        lp["w_uk_p"] = jnp.pad(w_uk[l], ((0, 0), (0, 0), (0, LANES - nope))).reshape(c_lat, heads * LANES).astype(BF16)
        lp["w_uk_f"] = w_uk[l].reshape(c_lat, heads * nope).astype(BF16)
        lp["w_uv_f"] = w_uv[l].reshape(c_lat, heads * vd).astype(BF16)
        lp["w_glu_b"] = w_glu[l].astype(BF16)
        lp["w_br_b"] = w_br[l].astype(BF16)
        lp["w_o_b"] = w_o[l].astype(BF16)
        lp["w_pq_b"] = w_pq[l].astype(BF16)
        lp["peer_keys_b"] = peer_keys[l].astype(BF16)

        zeros_state = jnp.zeros((bp, n_groups, p_state), F32)
        xp, hc, hr, hi, ck, kr = _layer(xp, bp, lp_len, pos_p, lp, lay, True, None, zeros_state, zeros_state, None)
        outs["p_conv"].append(hc)
        outs["p_re"].append(hr)
        outs["p_im"].append(hi)
        outs["p_ckv"].append(ck.reshape(bp, lp_len // page, page, c_lat))
        outs["p_kr"].append(kr.reshape(bp, lp_len // page, page, rope))
        xs, hc, hr, hi, ck, kr = _layer(xs, bs, ls, pos_s, lp, lay, False, cache_conv[l], state_ssm_re[l],
                                        state_ssm_im[l], (cache_ckv, cache_krope, page_table, l))
        outs["s_conv"].append(hc)
        outs["s_re"].append(hr)
        outs["s_im"].append(hi)
        outs["s_ckv"].append(ck.reshape(bs, ls, c_lat))
        outs["s_kr"].append(kr.reshape(bs, ls, rope))
    st = {k: jnp.stack(v) for k, v in outs.items()}
    return (xp.reshape(bp, lp_len, dm), xs.reshape(bs, ls, dm),
            st["p_conv"], st["p_re"], st["p_im"], st["p_ckv"], st["p_kr"],
            st["s_conv"], st["s_re"], st["s_im"], st["s_ckv"], st["s_kr"])
```

```python
import functools
import math

import jax
import jax.numpy as jnp
from jax import lax
from jax.experimental import pallas as pl
from jax.experimental.pallas import tpu as pltpu

EPS = 1e-6
ROPE_THETA = 10000.0
NEG_INF = -1e30
PEER_TOPK = 16

LANES = 128
SUBLANES = 8
VMEM_LIMIT = 48 * 1024 * 1024
VMEM_LIMIT_TABLE = 56 * 1024 * 1024

F32 = jnp.float32
BF16 = jnp.bfloat16


def _tile(n, pref):
    if n <= pref:
        return n
    t = pref - pref % SUBLANES
    while t >= SUBLANES:
        if n % t == 0:
            return t
        t -= SUBLANES
    return n


def _cparams(sem, limit=VMEM_LIMIT):
    return pltpu.CompilerParams(dimension_semantics=sem, vmem_limit_bytes=limit)


def _proj_kernel(x_ref, g_ref, w_ref, z_ref, h_ref):
    @pl.when(pl.program_id(1) == 0)
    def _():
        x = x_ref[...]
        ms = jnp.mean(x * x, axis=-1, keepdims=True)
        h_ref[...] = (x * lax.rsqrt(ms + EPS) * g_ref[...]).astype(BF16)

    z_ref[...] = jnp.dot(h_ref[...], w_ref[...], preferred_element_type=F32)


def _proj(x2d, g, w_p):
    t, d = x2d.shape
    n = w_p.shape[1]
    tm = _tile(t, 512)
    nb = n // LANES
    k = max(c for c in range(1, nb + 1) if nb % c == 0 and c * LANES <= 2304)
    tn = k * LANES
    return pl.pallas_call(
        _proj_kernel,
        grid=(t // tm, n // tn),
        in_specs=[
            pl.BlockSpec((tm, d), lambda i, j: (i, 0)),
            pl.BlockSpec((1, d), lambda i, j: (0, 0)),
            pl.BlockSpec((d, tn), lambda i, j: (0, j)),
        ],
        out_specs=pl.BlockSpec((tm, tn), lambda i, j: (i, j)),
        out_shape=jax.ShapeDtypeStruct((t, n), F32),
        scratch_shapes=[pltpu.VMEM((tm, d), BF16)],
        compiler_params=_cparams(("parallel", "arbitrary")),
        name="proj",
    )(x2d, g.reshape(1, d), w_p)


def _norm_rope(x, gain, cos, sin, nope, rope):
    lane = lax.broadcasted_iota(jnp.int32, (1, LANES), 1)
    m_n = lane < nope
    m_r = (lane >= nope) & (lane < nope + rope)
    sq = x * x
    ss_r = jnp.sum(jnp.where(m_r, sq, 0.0), axis=-1, keepdims=True)
    inv_r = lax.rsqrt(ss_r / rope + EPS)
    if nope:
        ss_n = jnp.sum(jnp.where(m_n, sq, 0.0), axis=-1, keepdims=True)
        inv = jnp.where(m_n, lax.rsqrt(ss_n / nope + EPS), inv_r)
    else:
        inv = inv_r
    y = x * inv * gain
    half = rope // 2
    first = lane < nope + half
    partner = jnp.where(first, pltpu.roll(y, LANES - half, 1), pltpu.roll(y, half, 1))
    return y * cos + partner * sin


def _prep_kernel(*refs, heads, nope, rope, seq_len, tm, prompt):
    if prompt:
        (xc_ref, bc_ref, cc_ref, q_ref, ckv_ref, kr_ref, cos_ref, sin_ref, cw_ref, gq_ref, gkr_ref, gkv_ref,
         hxc_ref, hcc_ref, oc_ref, qo_ref, ckvo_ref, kro_ref, vt_ref) = refs
    else:
        (xc_ref, bc_ref, cc_ref, q_ref, ckv_ref, kr_ref, cos_ref, sin_ref, cw_ref, gq_ref, gkr_ref, gkv_ref,
         h1_ref, h2_ref, oc_ref, qo_ref, ckvo_ref, kro_ref, vt_ref) = refs
    i = pl.program_id(0)
    v = cc_ref[...] * xc_ref[...]
    row = lax.broadcasted_iota(jnp.int32, (tm, 1), 0)
    r1 = pltpu.roll(v, 1, 0)
    r2 = pltpu.roll(v, 2, 0)
    if prompt:
        hv = hcc_ref[...] * hxc_ref[...]
        hv = jnp.where((i % (seq_len // tm)) == 0, 0.0, hv)
        v1 = jnp.where(row == 0, hv[7:8, :], r1)
        v2 = jnp.where(row == 0, hv[6:7, :], jnp.where(row == 1, hv[7:8, :], r2))
        vt_ref[0] = v[tm - SUBLANES:, :]
    else:
        l = row % seq_len
        v1 = jnp.where(l >= 1, r1, h1_ref[...])
        v2 = jnp.where(l >= 2, r2, h2_ref[...])
        vt_ref[...] = v
    y = v2 * cw_ref[0:1, :] + v1 * cw_ref[1:2, :] + v * cw_ref[2:3, :]
    oc_ref[...] = (bc_ref[...] * y).astype(oc_ref.dtype)

    cos = cos_ref[...]
    sin = sin_ref[...]
    gq = gq_ref[...]
    for h in range(heads):
        sl = slice(LANES * h, LANES * (h + 1))
        qo_ref[:, sl] = _norm_rope(q_ref[:, sl], gq, cos, sin, nope, rope).astype(qo_ref.dtype)
    kro_ref[...] = _kr_norm_rope(kr_ref[...], gkr_ref[...], cos, sin, nope, rope)
    c = ckv_ref[...]
    ms = jnp.mean(c * c, axis=-1, keepdims=True)
    ckvo_ref[...] = c * lax.rsqrt(ms + EPS) * gkv_ref[...]


def _kr_norm_rope(x, gain, cos, sin, nope, rope):
    lane = lax.broadcasted_iota(jnp.int32, (1, LANES), 1)
    ss = jnp.sum(x * x, axis=-1, keepdims=True)
    y = x * lax.rsqrt(ss / rope + EPS) * gain
    half = rope // 2
    first = lane < nope + half
    partner = jnp.where(first, pltpu.roll(y, LANES - half, 1), pltpu.roll(y, half, 1))
    return y * cos + partner * sin


def _prep(z, lay, cos_t, sin_t, conv_w, gq, gkr, gkv, seq_len, prompt, hist1=None, hist2=None, q_dtype=BF16):
    t = z.shape[0]
    bw, heads, c_lat = lay["bw"], lay["heads"], lay["c_lat"]
    hq = heads * LANES
    if prompt:
        tm = _tile(seq_len, 512)
        assert seq_len % tm == 0 and tm >= 2 * SUBLANES
    else:
        tm = _tile(t, 512)
        assert tm % seq_len == 0
    ntab = cos_t.shape[0] // tm
    in_specs = [
        pl.BlockSpec((tm, bw), lambda i: (i, 0)),
        pl.BlockSpec((tm, bw), lambda i: (i, 1)),
        pl.BlockSpec((tm, bw), lambda i: (i, 2)),
        pl.BlockSpec((tm, hq), lambda i: (i, lay["off_q"] // hq)),
        pl.BlockSpec((tm, c_lat), lambda i: (i, lay["off_ckv"] // c_lat)),
        pl.BlockSpec((tm, LANES), lambda i: (i, lay["off_kr"] // LANES)),
        pl.BlockSpec((tm, LANES), lambda i: (i % ntab, 0)),
        pl.BlockSpec((tm, LANES), lambda i: (i % ntab, 0)),
        pl.BlockSpec(conv_w.shape, lambda i: (0, 0)),
        pl.BlockSpec((1, LANES), lambda i: (0, 0)),
        pl.BlockSpec((1, LANES), lambda i: (0, 0)),
        pl.BlockSpec((1, c_lat), lambda i: (0, 0)),
    ]
    args = [z, z, z, z, z, z, cos_t, sin_t, conv_w, gq, gkr, gkv]
    if prompt:
        rb = tm // SUBLANES
        in_specs += [
            pl.BlockSpec((SUBLANES, bw), lambda i: (jnp.maximum(i * rb - 1, 0), 0)),
            pl.BlockSpec((SUBLANES, bw), lambda i: (jnp.maximum(i * rb - 1, 0), 2)),
        ]
        args += [z, z]
        vt_spec = pl.BlockSpec((1, SUBLANES, bw), lambda i: (i, 0, 0))
        vt_shape = jax.ShapeDtypeStruct((t // tm, SUBLANES, bw), F32)
    else:
        in_specs += [pl.BlockSpec((tm, bw), lambda i: (i, 0)), pl.BlockSpec((tm, bw), lambda i: (i, 0))]
        args += [hist1, hist2]
        vt_spec = pl.BlockSpec((tm, bw), lambda i: (i, 0))
        vt_shape = jax.ShapeDtypeStruct((t, bw), F32)
    kern = functools.partial(_prep_kernel, heads=heads, nope=lay["nope"], rope=lay["rope"], seq_len=seq_len, tm=tm,
                             prompt=prompt)
    return pl.pallas_call(
        kern,
        grid=(t // tm,),
        in_specs=in_specs,
        out_specs=[
            pl.BlockSpec((tm, bw), lambda i: (i, 0)),
            pl.BlockSpec((tm, hq), lambda i: (i, 0)),
            pl.BlockSpec((tm, c_lat), lambda i: (i, 0)),
            pl.BlockSpec((tm, LANES), lambda i: (i, 0)),
            vt_spec,
        ],
        out_shape=[
            jax.ShapeDtypeStruct((t, bw), BF16),
            jax.ShapeDtypeStruct((t, hq), q_dtype),
            jax.ShapeDtypeStruct((t, c_lat), F32),
            jax.ShapeDtypeStruct((t, LANES), F32),
            vt_shape,
        ],
        compiler_params=_cparams(("parallel",)),
        name="prep_prompt" if prompt else "prep_sample",
    )(*args)


def _s5_mats(lam_re, lam_im, log_dt, b_re, b_im, c_re, c_im, lc):
    hp = lax.Precision.HIGHEST
    g, p, n_in = b_re.shape
    n_out = c_re.shape[1]
    dt = jnp.exp(log_dt)[:, None]
    lr, li = lam_re, lam_im
    mag = jnp.exp(lr * dt)
    a_re, a_im = mag * jnp.cos(li * dt), mag * jnp.sin(li * dt)
    den = lr * lr + li * li
    f_re = ((a_re - 1.0) * lr + a_im * li) / den
    f_im = (a_im * lr - (a_re - 1.0) * li) / den
    bb_re = f_re[..., None] * b_re - f_im[..., None] * b_im
    bb_im = f_re[..., None] * b_im + f_im[..., None] * b_re
    k = jnp.arange(lc + 1, dtype=F32)[:, None, None]
    pm = jnp.exp(lr[None] * dt[None] * k)
    pr = pm * jnp.cos(li[None] * dt[None] * k)
    pi = pm * jnp.sin(li[None] * dt[None] * k)
    ab_re = pr[:lc, ..., None] * bb_re[None] - pi[:lc, ..., None] * bb_im[None]
    ab_im = pr[:lc, ..., None] * bb_im[None] + pi[:lc, ..., None] * bb_re[None]
    ms = jnp.concatenate([ab_re[::-1], ab_im[::-1]], axis=2)
    ms = jnp.transpose(ms, (1, 0, 3, 2)).reshape(g, lc * n_in, 2 * p)
    kk = (jnp.einsum("gop,dgpi->dgoi", c_re, ab_re, precision=hp)
          - jnp.einsum("gop,dgpi->dgoi", c_im, ab_im, precision=hp))
    s_idx = jnp.arange(lc)[:, None]
    t_idx = jnp.arange(lc)[None, :]
    delta = t_idx - s_idx
    kt = jnp.where((delta >= 0)[:, :, None, None, None], kk[jnp.clip(delta, 0, lc - 1)], 0.0)
    tk = jnp.transpose(kt, (2, 0, 4, 1, 3)).reshape(g, lc * n_in, lc * n_out)
    ca_re = c_re[None] * pr[1:, :, None, :] - c_im[None] * pi[1:, :, None, :]
    ca_im = c_re[None] * pi[1:, :, None, :] + c_im[None] * pr[1:, :, None, :]
    gs = jnp.concatenate([ca_re, -ca_im], axis=3)
    gs = jnp.transpose(gs, (1, 3, 0, 2)).reshape(g, 2 * p, lc * n_out)
    a1 = jnp.concatenate([pr[lc], pr[lc]], axis=-1)[:, None, :]
    a2 = jnp.concatenate([-pi[lc], pi[lc]], axis=-1)[:, None, :]
    return ms.astype(BF16), tk.astype(BF16), gs.astype(BF16), a1, a2


def _s5_kernel(u_ref, tk_ref, ms_ref, gs_ref, a1_ref, a2_ref, h0_ref, y_ref, ht_ref, s_sc, hin_sc, *, n_blocks, bsz,
               spb, p_state):
    u = u_ref[...]
    s_sc[...] = jnp.dot(u, ms_ref[...], preferred_element_type=F32)
    a1 = a1_ref[...]
    a2 = a2_ref[...]
    rb = spb * bsz

    def body(k, h):
        r = pl.multiple_of(k * rb, SUBLANES)
        s_blk = s_sc[pl.ds(r, rb), :]
        hs = []
        for q in range(spb):
            hs.append(h)
            h = a1 * h + a2 * pltpu.roll(h, p_state, 1) + s_blk[q * bsz:(q + 1) * bsz, :]
        hin_sc[pl.ds(r, rb), :] = hs[0] if spb == 1 else jnp.concatenate(hs, axis=0)
        return h

    h = lax.fori_loop(0, n_blocks, body, h0_ref[...])
    ht_ref[...] = h
    y_ref[...] = (jnp.dot(u, tk_ref[...], preferred_element_type=F32)
                  + jnp.dot(hin_sc[...].astype(BF16), gs_ref[...], preferred_element_type=F32))


def _s5(us, h0_re, h0_im, mats, bsz, seq_len, lc):
    ms, tk, gs, a1, a2 = mats
    g, _, p2 = ms.shape
    p_state = p2 // 2
    n_in = ms.shape[1] // lc
    n_out = tk.shape[2] // lc
    n_chunks = seq_len // lc
    spb = max(1, SUBLANES // bsz)
    assert (spb * bsz) % SUBLANES == 0 and n_chunks % spb == 0
    nc = n_chunks * bsz
    bp = bsz
    u = us.astype(BF16).reshape(bsz, n_chunks, lc, g, n_in)
    u = jnp.transpose(u, (3, 1, 0, 2, 4)).reshape(g, nc, lc * n_in)
    h0 = jnp.transpose(jnp.concatenate([h0_re, h0_im], axis=-1), (1, 0, 2))
    kern = functools.partial(_s5_kernel, n_blocks=n_chunks // spb, bsz=bsz, spb=spb, p_state=p_state)
    y, ht = pl.pallas_call(
        kern,
        grid=(g,),
        in_specs=[
            pl.BlockSpec((None, nc, lc * n_in), lambda i: (i, 0, 0)),
            pl.BlockSpec((None, lc * n_in, lc * n_out), lambda i: (i, 0, 0)),
            pl.BlockSpec((None, lc * n_in, p2), lambda i: (i, 0, 0)),
            pl.BlockSpec((None, p2, lc * n_out), lambda i: (i, 0, 0)),
            pl.BlockSpec((None, 1, p2), lambda i: (i, 0, 0)),
            pl.BlockSpec((None, 1, p2), lambda i: (i, 0, 0)),
            pl.BlockSpec((None, bp, p2), lambda i: (i, 0, 0)),
        ],
        out_specs=[
            pl.BlockSpec((None, nc, lc * n_out), lambda i: (i, 0, 0)),
            pl.BlockSpec((None, bp, p2), lambda i: (i, 0, 0)),
        ],
        out_shape=[
            jax.ShapeDtypeStruct((g, nc, lc * n_out), F32),
            jax.ShapeDtypeStruct((g, bp, p2), F32),
        ],
        scratch_shapes=[pltpu.VMEM((nc, p2), F32), pltpu.VMEM((nc, p2), F32)],
        compiler_params=_cparams(("parallel",)),
        name="s5",
    )(u, tk, ms, gs, a1, a2, h0)
    y = y.reshape(g, n_chunks, bp, lc, n_out)[:, :, :bsz]
    y = jnp.transpose(y, (2, 1, 3, 0, 4)).reshape(bsz * seq_len, g * n_out)
    ht = jnp.transpose(ht[:, :bsz], (1, 0, 2))
    return y, ht[..., :p_state], ht[..., p_state:]


def _kprep_kernel(ckv_ref, kr_ref, wuk_ref, wuv_ref, gkn_ref, k_ref, v_ref, *, heads, nope):
    c = ckv_ref[...].astype(BF16)
    kraw = jnp.dot(c, wuk_ref[...], preferred_element_type=F32)
    kr = kr_ref[...]
    gkn = gkn_ref[...]
    for h in range(heads):
        sl = slice(LANES * h, LANES * (h + 1))
        kh = kraw[:, sl]
        ss = jnp.sum(kh * kh, axis=-1, keepdims=True)
        k_ref[:, sl] = (kh * lax.rsqrt(ss / nope + EPS) * gkn + kr).astype(BF16)
    v_ref[...] = jnp.dot(c, wuv_ref[...], preferred_element_type=F32).astype(BF16)


def _kprep(ckv_n, kr_n, wuk_p, wuv, gkn, heads, nope):
    t, c_lat = ckv_n.shape
    tm = _tile(t, 512)
    hk = wuk_p.shape[1]
    hv = wuv.shape[1]
    return pl.pallas_call(
        functools.partial(_kprep_kernel, heads=heads, nope=nope),
        grid=(t // tm,),
        in_specs=[
            pl.BlockSpec((tm, c_lat), lambda i: (i, 0)),
            pl.BlockSpec((tm, LANES), lambda i: (i, 0)),
            pl.BlockSpec((c_lat, hk), lambda i: (0, 0)),
            pl.BlockSpec((c_lat, hv), lambda i: (0, 0)),
            pl.BlockSpec((1, LANES), lambda i: (0, 0)),
        ],
        out_specs=[pl.BlockSpec((tm, hk), lambda i: (i, 0)), pl.BlockSpec((tm, hv), lambda i: (i, 0))],
        out_shape=[jax.ShapeDtypeStruct((t, hk), BF16), jax.ShapeDtypeStruct((t, hv), BF16)],
        compiler_params=_cparams(("parallel",)),
        name="kprep",
    )(ckv_n, kr_n, wuk_p, wuv, gkn)


def _flash_kernel(ii_ref, jj_ref, q_ref, k_ref, v_ref, o_ref, m_sc, l_sc, acc_sc, *, c_exp, tq, hps, vd):
    i = ii_ref[pl.program_id(2)]
    j = jj_ref[pl.program_id(2)]
    nct = tq // LANES

    @pl.when(j == 0)
    def _():
        m_sc[...] = jnp.full(m_sc.shape, NEG_INF, F32)
        l_sc[...] = jnp.zeros(l_sc.shape, F32)
        acc_sc[...] = jnp.zeros(acc_sc.shape, F32)

    def step(diagonal):
        if diagonal:
            row = lax.broadcasted_iota(jnp.int32, (tq, tq), 0)
            col = lax.broadcasted_iota(jnp.int32, (tq, tq), 1)
            keep = col <= row
        for hh in range(hps):
            q = q_ref[:, LANES * hh:LANES * (hh + 1)]
            k = k_ref[:, LANES * hh:LANES * (hh + 1)]
            s = lax.dot_general(q, k, (((1,), (1,)), ((), ())), preferred_element_type=F32)
            if diagonal:
                s = jnp.where(keep, s, NEG_INF)
            m_prev = m_sc[hh]
            m_new = jnp.maximum(m_prev, jnp.max(s, axis=-1, keepdims=True))
            alpha = jnp.exp2((m_prev - m_new) * c_exp)
            ps = [jnp.exp2((s[:, LANES * c:LANES * (c + 1)] - m_new) * c_exp) for c in range(nct)]
            psum = ps[0]
            for c in range(1, nct):
                psum = psum + ps[c]
            l_sc[hh] = alpha * l_sc[hh] + jnp.sum(psum, axis=-1, keepdims=True)
            p = jnp.concatenate(ps, axis=1).astype(BF16)
            acc_sc[hh] = alpha[:, :vd] * acc_sc[hh] + jnp.dot(p, v_ref[:, vd * hh:vd * (hh + 1)],
                                                              preferred_element_type=F32)
            m_sc[hh] = m_new

    @pl.when(j < i)
    def _():
        step(False)

    @pl.when(j == i)
    def _():
        step(True)
        for hh in range(hps):
            o_ref[:, vd * hh:vd * (hh + 1)] = (acc_sc[hh] / l_sc[hh][:, :vd]).astype(o_ref.dtype)


def _flash(q, k, v, bsz, seq_len, heads, vd, scale):
    hps = LANES // vd
    assert heads % hps == 0
    tq = _tile(seq_len, 512)
    nq = seq_len // tq
    t = bsz * seq_len
    assert tq % LANES == 0
    kern = functools.partial(_flash_kernel, c_exp=scale * math.log2(math.e), tq=tq, hps=hps, vd=vd)
    pairs = [(i, j) for i in range(nq) for j in range(i + 1)]
    ii = jnp.asarray([p[0] for p in pairs], jnp.int32)
    jj = jnp.asarray([p[1] for p in pairs], jnp.int32)
    grid_spec = pltpu.PrefetchScalarGridSpec(
        num_scalar_prefetch=2,
        grid=(bsz, heads // hps, len(pairs)),
        in_specs=[
            pl.BlockSpec((tq, hps * LANES), lambda b, h, s, ii, jj: (b * nq + ii[s], h)),
            pl.BlockSpec((tq, hps * LANES), lambda b, h, s, ii, jj: (b * nq + jj[s], h)),
            pl.BlockSpec((tq, LANES), lambda b, h, s, ii, jj: (b * nq + jj[s], h)),
        ],
        out_specs=pl.BlockSpec((tq, LANES), lambda b, h, s, ii, jj: (b * nq + ii[s], h)),
        scratch_shapes=[pltpu.VMEM((hps, tq, LANES), F32), pltpu.VMEM((hps, tq, LANES), F32),
                        pltpu.VMEM((hps, tq, vd), F32)],
    )
    return pl.pallas_call(
        kern,
        grid_spec=grid_spec,
        out_shape=jax.ShapeDtypeStruct((t, heads * vd), BF16),
        compiler_params=_cparams(("parallel", "parallel", "arbitrary")),
        name="flash",
    )(ii, jj, q, k, v)


def _paged_kernel(pt_ref, qbd_ref, qr_ref, cn_ref, rn_ref, *refs, pp, heads, nope, vd, lq, scale):
    del pt_ref
    c_pages = refs[:pp]
    r_pages = refs[pp:2 * pp]
    wuk_ref, wuv_ref, e_ref, gk_ref, o_ref, m_sc, l_sc, acc_sc, cb_sc, rb_sc = refs[2 * pp:]
    s_id = pl.program_id(1)
    rows = lq * heads
    page = c_pages[0].shape[0]

    @pl.when(s_id == 0)
    def _():
        m_sc[...] = jnp.full(m_sc.shape, NEG_INF, F32)
        l_sc[...] = jnp.zeros(l_sc.shape, F32)
        acc_sc[...] = jnp.zeros(acc_sc.shape, F32)

    qbd = (qbd_ref[...] * gk_ref[...]).astype(BF16)
    qr = qr_ref[...].astype(BF16)
    dn = (((1,), (1,)), ((), ()))

    def scores(cb, rb):
        kraw = jnp.dot(cb, wuk_ref[...], preferred_element_type=F32)
        ssq = lax.dot_general(e_ref[...], (kraw * kraw).astype(BF16), dn, preferred_element_type=F32)
        inv = lax.rsqrt(ssq / nope + EPS)
        inv = jnp.concatenate([inv] * lq, axis=0)
        sn = lax.dot_general(qbd, kraw.astype(BF16), dn, preferred_element_type=F32)
        sr = jnp.dot(qr, rb, preferred_element_type=F32)
        return (sn * inv + sr) * scale

    def update(cb, rb, causal):
        s = scores(cb, rb)
        if causal:
            nk = cb.shape[0]
            kk = lax.broadcasted_iota(jnp.int32, (rows, nk), 1)
            qq = lax.broadcasted_iota(jnp.int32, (rows, nk), 0) // heads
            s = jnp.where(kk <= qq, s, NEG_INF)
        m_prev = m_sc[...]
        m_new = jnp.maximum(m_prev, jnp.max(s, axis=-1, keepdims=True))
        alpha = jnp.exp(m_prev - m_new)
        p = jnp.exp(s - m_new)
        l_sc[...] = alpha * l_sc[...] + jnp.sum(p, axis=-1, keepdims=True)
        acc_sc[...] = alpha * acc_sc[...] + jnp.dot(p.astype(BF16), cb, preferred_element_type=F32)
        m_sc[...] = m_new

    for pg in range(pp):
        cb_sc[page * pg:page * (pg + 1), :] = c_pages[pg][...].astype(BF16)
        rb_sc[:, page * pg:page * (pg + 1)] = r_pages[pg][...].astype(BF16)
    update(cb_sc[...], rb_sc[...], False)

    @pl.when(s_id == pl.num_programs(1) - 1)
    def _():
        update(cn_ref[...].astype(BF16), rn_ref[...].astype(BF16), True)
        lat = (acc_sc[...] / l_sc[...]).astype(BF16)
        full = jnp.dot(lat, wuv_ref[...], preferred_element_type=F32)
        colh = lax.broadcasted_iota(jnp.int32, full.shape, 1) // vd
        rowh = lax.broadcasted_iota(jnp.int32, full.shape, 0) % heads
        full = jnp.where(colh == rowh, full, 0.0)
        o_ref[...] = jnp.sum(full.reshape(lq, heads, heads * vd), axis=1)


def _paged(page_table, qbd, qr, c_new, r_new, cache_ckv, cache_krope, layer, wuk, wuv, e_mat, gk, heads, nope, vd,
           scale):
    bs, rows, _ = qbd.shape
    lq = rows // heads
    n_pages = page_table.shape[1]
    page, c_lat = cache_ckv.shape[2], cache_ckv.shape[3]
    rope = cache_krope.shape[3]
    pp = math.gcd(n_pages, 16)
    kn = c_new.shape[1]
    cache_krope = jnp.swapaxes(cache_krope, 2, 3)
    r_new = jnp.swapaxes(r_new, 1, 2)

    def cmap(p):
        return lambda b, s, pt: (layer, pt[b, s * pp + p], 0, 0)

    in_specs = [
        pl.BlockSpec((None, rows, heads * nope), lambda b, s, pt: (b, 0, 0)),
        pl.BlockSpec((None, rows, rope), lambda b, s, pt: (b, 0, 0)),
        pl.BlockSpec((None, kn, c_lat), lambda b, s, pt: (b, 0, 0)),
        pl.BlockSpec((None, rope, kn), lambda b, s, pt: (b, 0, 0)),
    ]
    in_specs += [pl.BlockSpec((None, None, page, c_lat), cmap(p)) for p in range(pp)]
    in_specs += [pl.BlockSpec((None, None, rope, page), cmap(p)) for p in range(pp)]
    in_specs += [
        pl.BlockSpec(wuk.shape, lambda b, s, pt: (0, 0)),
        pl.BlockSpec(wuv.shape, lambda b, s, pt: (0, 0)),
        pl.BlockSpec(e_mat.shape, lambda b, s, pt: (0, 0)),
        pl.BlockSpec(gk.shape, lambda b, s, pt: (0, 0)),
    ]
    kern = functools.partial(_paged_kernel, pp=pp, heads=heads, nope=nope, vd=vd, lq=lq, scale=scale)
    grid_spec = pltpu.PrefetchScalarGridSpec(
        num_scalar_prefetch=1,
        grid=(bs, n_pages // pp),
        in_specs=in_specs,
        out_specs=pl.BlockSpec((None, lq, heads * vd), lambda b, s, pt: (b, 0, 0)),
        scratch_shapes=[pltpu.VMEM((rows, 1), F32), pltpu.VMEM((rows, 1), F32), pltpu.VMEM((rows, c_lat), F32),
                        pltpu.VMEM((pp * page, c_lat), BF16), pltpu.VMEM((rope, pp * page), BF16)],
    )
    return pl.pallas_call(
        kern,
        grid_spec=grid_spec,
        out_shape=jax.ShapeDtypeStruct((bs, lq, heads * vd), F32),
        compiler_params=_cparams(("parallel", "arbitrary")),
        name="paged",
    )(page_table, qbd, qr, c_new, r_new, *([cache_ckv] * pp), *([cache_krope] * pp), wuk, wuv, e_mat, gk)


def _merge_kernel(x_ref, oc_ref, ys_ref, us_ref, oa_ref, g0_ref, g1_ref, g2_ref, d_ref, wglu_ref, bglu_ref, wbr_ref,
                  wo_ref, gffn_ref, wpq_ref, x1_ref, h2_ref, qp_ref):
    y = ys_ref[...] + d_ref[...] * us_ref[...]
    zg = jax.nn.gelu(y)
    gl = jnp.dot(zg.astype(BF16), wglu_ref[...], preferred_element_type=F32) + bglu_ref[...]
    o_ssm = zg * jax.nn.sigmoid(gl)
    merged = jax.nn.sigmoid(g0_ref[...]) * jnp.dot(oc_ref[...], wbr_ref[0], preferred_element_type=F32)
    merged += jax.nn.sigmoid(g1_ref[...]) * jnp.dot(o_ssm.astype(BF16), wbr_ref[1], preferred_element_type=F32)
    merged += jax.nn.sigmoid(g2_ref[...]) * jnp.dot(oa_ref[...].astype(BF16), wbr_ref[2], preferred_element_type=F32)
    x1 = x_ref[...] + jnp.dot(merged.astype(BF16), wo_ref[...], preferred_element_type=F32)
    x1_ref[...] = x1
    ms = jnp.mean(x1 * x1, axis=-1, keepdims=True)
    h2 = x1 * lax.rsqrt(ms + EPS) * gffn_ref[...]
    h2_ref[...] = h2
    qp_ref[...] = jnp.dot(h2.astype(BF16), wpq_ref[...], preferred_element_type=F32)


def _merge(x2d, oc, ys, z, oa, lay, d, wglu, bglu, wbr, wo, gffn, wpq):
    t, dm = x2d.shape
    bw = lay["bw"]
    dq = wpq.shape[1]
    tm = _tile(t, 256)
    gi = lay["off_g"] // dm

    def full(a):
        return pl.BlockSpec(a.shape, lambda i, _n=a.ndim: (0,) * _n)

    return pl.pallas_call(
        _merge_kernel,
        grid=(t // tm,),
        in_specs=[
            pl.BlockSpec((tm, dm), lambda i: (i, 0)),
            pl.BlockSpec((tm, bw), lambda i: (i, 0)),
            pl.BlockSpec((tm, bw), lambda i: (i, 0)),
            pl.BlockSpec((tm, bw), lambda i: (i, 3)),
            pl.BlockSpec((tm, bw), lambda i: (i, 0)),
            pl.BlockSpec((tm, dm), lambda i: (i, gi)),
            pl.BlockSpec((tm, dm), lambda i: (i, gi + 1)),
            pl.BlockSpec((tm, dm), lambda i: (i, gi + 2)),
            full(d), full(wglu), full(bglu), full(wbr), full(wo), full(gffn), full(wpq),
        ],
        out_specs=[
            pl.BlockSpec((tm, dm), lambda i: (i, 0)),
            pl.BlockSpec((tm, dm), lambda i: (i, 0)),
            pl.BlockSpec((tm, dq), lambda i: (i, 0)),
        ],
        out_shape=[
            jax.ShapeDtypeStruct((t, dm), F32),
            jax.ShapeDtypeStruct((t, dm), F32),
            jax.ShapeDtypeStruct((t, dq), F32),
        ],
        compiler_params=_cparams(("parallel",)),
        name="merge",
    )(x2d, oc, ys, z, oa, z, z, z, d, wglu, bglu, wbr, wo, gffn, wpq)


def _topk_rows(x, k):
    n = x.shape[0]
    iota = lax.broadcasted_iota(jnp.int32, x.shape, 0)
    vals, idxs = [], []
    for _ in range(k):
        m = jnp.max(x, axis=0, keepdims=True)
        am = jnp.min(jnp.where(x == m, iota, n), axis=0, keepdims=True)
        vals.append(m)
        idxs.append(am)
        x = jnp.where(iota == am, -jnp.inf, x)
    return jnp.concatenate(vals, axis=0), jnp.concatenate(idxs, axis=0)


def _select_rows(table, sel, k):
    out = jnp.zeros(sel.shape, table.dtype)
    for r in range(k):
        out = jnp.where(sel == r, table[r:r + 1, :], out)
    return out


def _pair_candidates(v1, v2, k):
    chunks, meta, r0, i = [], [], 0, 0
    while i < k and k // (i + 1) >= 2:
        n = k // (i + 1)
        nr = -(-n // SUBLANES) * SUBLANES
        blk = v1[i:i + 1, :] + v2[0:nr, :]
        if n < nr:
            blk = jnp.where(lax.broadcasted_iota(jnp.int32, blk.shape, 0) < n, blk, -jnp.inf)
        chunks.append(blk)
        meta.append((r0, nr, i, None))
        r0 += nr
        i += 1
    assert (k - i) % SUBLANES == 0
    chunks.append(v1[i:k, :] + v2[0:1, :])
    meta.append((r0, k - i, None, i))
    return jnp.concatenate(chunks, axis=0), meta


def _route_kernel(qp_ref, keys_ref, e_ref, g_ref, *, heads, n_keys, dk, topk, row_mult):
    dn = (((1,), (1,)), ((), ()))
    for h in range(heads):
        sub = []
        for s in range(2):
            o = (2 * h + s) * dk
            qs = qp_ref[:, o:o + dk].astype(BF16)
            st = lax.dot_general(keys_ref[h, s], qs, dn, preferred_element_type=F32)
            sub.append(_topk_rows(st, topk))
        (v1, i1), (v2, i2) = sub
        cand, meta = _pair_candidates(v1, v2, topk)
        sc, ci = _topk_rows(cand, topk)
        ihi = jnp.zeros(ci.shape, jnp.int32)
        jlo = jnp.zeros(ci.shape, jnp.int32)
        for r0, nr, ic, i0 in meta:
            inr = (ci >= r0) & (ci < r0 + nr)
            if ic is None:
                ihi = jnp.where(inr, ci - r0 + i0, ihi)
            else:
                ihi = jnp.where(inr, ic, ihi)
                jlo = jnp.where(inr, ci - r0, jlo)
        e1 = _select_rows(i1, ihi, topk)
        e2 = _select_rows(i2, jlo, topk)
        ex = jnp.exp(sc - sc[0:1, :])
        e_ref[topk * h:topk * (h + 1), :] = (e1 * n_keys + e2) * row_mult
        g_ref[topk * h:topk * (h + 1), :] = ex / jnp.sum(ex, axis=0, keepdims=True)


def _route(qp, keys_bf16, row_mult):
    t, dq = qp.shape
    heads, _, n_keys, dk = keys_bf16.shape
    tm = _tile(t, 128)
    if t % LANES:
        tm = t
    npair = heads * PEER_TOPK
    kern = functools.partial(_route_kernel, heads=heads, n_keys=n_keys, dk=dk, topk=PEER_TOPK, row_mult=row_mult)
    return pl.pallas_call(
        kern,
        grid=(t // tm,),
        in_specs=[
            pl.BlockSpec((tm, dq), lambda i: (i, 0)),
            pl.BlockSpec(keys_bf16.shape, lambda i: (0, 0, 0, 0)),
        ],
        out_specs=[pl.BlockSpec((npair, tm), lambda i: (0, i)), pl.BlockSpec((npair, tm), lambda i: (0, i))],
        out_shape=[jax.ShapeDtypeStruct((npair, t), jnp.int32), jax.ShapeDtypeStruct((npair, t), F32)],
        compiler_params=_cparams(("parallel",)),
        name="route",
    )(qp, keys_bf16)


HI_MASK = 0xFFFF0000
_BUTTERFLY_ORDER = (0, 4, 2, 6, 1, 5, 3, 7)


def _pack_table(tab):
    e, d = tab.shape
    half = d // 2
    b = lax.bitcast_convert_type(tab.astype(BF16), jnp.uint16).astype(jnp.uint32)
    return (b[:, :half] | (b[:, half:] << 16)).reshape(e * (half // LANES), LANES)


def _pack_table_rows(tab):
    e, d = tab.shape
    b = lax.bitcast_convert_type(tab.astype(BF16), jnp.uint16).astype(jnp.uint32).reshape(e, d // (2 * LANES), 2, LANES)
    return (b[:, :, 0, :] | (b[:, :, 1, :] << 16)).reshape(e * (d // (2 * LANES)), LANES)


def _unpack(x):
    return pltpu.bitcast(x << 16, F32), pltpu.bitcast(x & jnp.uint32(HI_MASK), F32)


def _fold_sublanes(x, y, k, mask):
    return jnp.where(mask, x, pltpu.roll(y, k, 0)) + jnp.where(mask, pltpu.roll(x, SUBLANES - k, 0), y)


def _peer_u_kernel(e_ref, h_ref, tab_ref, a_ref, *, tb, npair, rows):
    sub = lax.broadcasted_iota(jnp.int32, (SUBLANES, LANES), 0)
    lane = lax.broadcasted_iota(jnp.int32, (SUBLANES, LANES), 1)
    m4 = sub < 4
    m2 = (sub % 4) < 2
    m1 = (sub % 2) == 0
    ng = npair // SUBLANES

    def body(t, carry):
        th = h_ref[t]
        out = jnp.zeros((SUBLANES, LANES), F32)
        for g in range(ng):
            ps = []
            for j in _BUTTERFLY_ORDER:
                idx = pl.multiple_of(e_ref[t, SUBLANES * g + j], rows)
                u_row = pltpu.bitcast(tab_ref[pl.ds(idx, rows), :], BF16).astype(F32)
                ps.append(u_row * th)
            v = [_fold_sublanes(ps[2 * k], ps[2 * k + 1], 4, m4) for k in range(4)]
            r = _fold_sublanes(_fold_sublanes(v[0], v[1], 2, m2), _fold_sublanes(v[2], v[3], 2, m2), 1, m1)
            out = jnp.where(lane == g, jnp.sum(r, axis=-1, keepdims=True), out)
        a_ref[t] = out[:, :ng]
        return carry

    lax.fori_loop(0, tb, body, 0, unroll=4)


def _peer_v_kernel(e_ref, w_ref, x_ref, tab_ref, o_ref, *, tb, npair, rows):
    def body(t, carry):
        lo = [jnp.zeros((rows, LANES), F32) for _ in range(2)]
        hi = [jnp.zeros((rows, LANES), F32) for _ in range(2)]
        for j in range(npair):
            idx = pl.multiple_of(e_ref[t, j], rows)
            w = w_ref[t, j]
            x_lo, x_hi = _unpack(tab_ref[pl.ds(idx, rows), :])
            lo[j % 2] = lo[j % 2] + w * x_lo
            hi[j % 2] = hi[j % 2] + w * x_hi
        o_ref[t, 0] = x_ref[t, 0] + (lo[0] + lo[1])
        o_ref[t, 1] = x_ref[t, 1] + (hi[0] + hi[1])
        return carry

    lax.fori_loop(0, tb, body, 0, unroll=2)


def _peer_w_kernel(a_ref, g_ref, w_ref):
    w_ref[...] = g_ref[...] * jax.nn.gelu(a_ref[...])


def _peer(x1, h2, e_pt, g_pt, u_pk, v_pk):
    t, dm = x1.shape
    npair = e_pt.shape[0]
    rows = dm // (2 * LANES)
    assert 2 * rows == SUBLANES and npair % SUBLANES == 0
    tb = _tile(t, 64)
    nblk = t // tb
    ng = npair // SUBLANES
    e_nat = jnp.transpose(e_pt)
    e_u = jnp.transpose(e_nat.reshape(t, SUBLANES, ng), (0, 2, 1)).reshape(nblk, tb, npair)
    e_v = e_nat.reshape(nblk, tb, npair)
    g_nat = jnp.transpose(g_pt)
    smem = functools.partial(pl.BlockSpec, memory_space=pltpu.SMEM)
    tab_spec = pl.BlockSpec(u_pk.shape, lambda i: (0, 0), pipeline_mode=pl.Buffered(1))
    tok_spec = pl.BlockSpec((tb, 2, rows, LANES), lambda i: (i, 0, 0, 0))
    idx_spec = smem((None, tb, npair), lambda i: (i, 0, 0))
    a = pl.pallas_call(
        functools.partial(_peer_u_kernel, tb=tb, npair=npair, rows=rows),
        grid=(nblk,),
        in_specs=[idx_spec, pl.BlockSpec((tb, SUBLANES, LANES), lambda i: (i, 0, 0)), tab_spec],
        out_specs=pl.BlockSpec((tb, SUBLANES, ng), lambda i: (i, 0, 0)),
        out_shape=jax.ShapeDtypeStruct((t, SUBLANES, ng), F32),
        compiler_params=_cparams(("arbitrary",), VMEM_LIMIT_TABLE),
        name="peer_u",
    )(e_u, h2.reshape(t, SUBLANES, LANES), u_pk)
    tw = _tile(t, 512)
    w = pl.pallas_call(
        _peer_w_kernel,
        grid=(t // tw,),
        in_specs=[pl.BlockSpec((tw, npair), lambda i: (i, 0)), pl.BlockSpec((tw, npair), lambda i: (i, 0))],
        out_specs=pl.BlockSpec((tw, npair), lambda i: (i, 0)),
        out_shape=jax.ShapeDtypeStruct((t, npair), F32),
        compiler_params=_cparams(("parallel",)),
        name="peer_w",
    )(a.reshape(t, npair), g_nat)
    out = pl.pallas_call(
        functools.partial(_peer_v_kernel, tb=tb, npair=npair, rows=rows),
        grid=(nblk,),
        in_specs=[idx_spec, idx_spec, tok_spec, tab_spec],
        out_specs=tok_spec,
        out_shape=jax.ShapeDtypeStruct((t, 2, rows, LANES), F32),
        compiler_params=_cparams(("arbitrary",), VMEM_LIMIT_TABLE),
        name="peer_v",
    )(e_v, w.reshape(nblk, tb, npair), x1.reshape(t, 2, rows, LANES), v_pk)
    return out.reshape(t, dm)


def _layout(dm, bw, heads, nope, rope, c_lat):
    hq = heads * LANES
    lay = dict(bw=bw, heads=heads, nope=nope, rope=rope, c_lat=c_lat)
    lay["off_q"] = 4 * bw
    lay["off_g"] = lay["off_q"] + hq
    lay["off_ckv"] = lay["off_g"] + 3 * dm
    lay["off_kr"] = lay["off_ckv"] + c_lat
    lay["n"] = lay["off_kr"] + LANES
    assert nope + rope <= LANES and rope % 2 == 0
    assert lay["off_q"] % hq == 0 and lay["off_g"] % dm == 0 and lay["off_ckv"] % c_lat == 0
    assert bw % LANES == 0 and c_lat % LANES == 0
    return lay


def _pack_w_in(w_in, lay, dm):
    bw, heads, nope, rope, c_lat = lay["bw"], lay["heads"], lay["nope"], lay["rope"], lay["c_lat"]
    sizes = (bw, bw, bw, bw, heads * (nope + rope), c_lat, rope, 3 * dm)
    parts, off = [], 0
    for n in sizes:
        parts.append(w_in[:, off:off + n])
        off += n
    xc, bc, cc, us, q, ckv, kr, gates = parts
    q = jnp.pad(q.reshape(dm, heads, nope + rope), ((0, 0), (0, 0), (0, LANES - nope - rope))).reshape(dm, heads * LANES)
    kr = jnp.pad(kr, ((0, 0), (nope, LANES - nope - rope)))
    return jnp.concatenate([xc, bc, cc, us, q, gates, ckv, kr], axis=1).astype(BF16)


def _rope_tables(pos, nope, rope):
    half = rope // 2
    inv = ROPE_THETA ** (-jnp.arange(half, dtype=F32) / half)
    ang = pos.astype(F32)[:, None] * inv[None, :]
    cos, sin = jnp.cos(ang), jnp.sin(ang)
    n = pos.shape[0]
    ones = jnp.ones((n, nope), F32)
    tail = LANES - nope - rope
    cos_t = jnp.concatenate([ones, cos, cos, jnp.ones((n, tail), F32)], axis=1)
    sin_t = jnp.concatenate([jnp.zeros((n, nope), F32), -sin, sin, jnp.zeros((n, tail), F32)], axis=1)
    return cos_t, sin_t


def _lane_vec(parts):
    v = jnp.concatenate(parts)
    return jnp.pad(v, (0, LANES - v.shape[0])).reshape(1, LANES)


def _layer(x2d, bsz, seq_len, pos, lp, lay, prompt, conv_hist, h0_re, h0_im, sample_ctx):
    t, dm = x2d.shape
    bw, heads, nope, rope, c_lat = lay["bw"], lay["heads"], lay["nope"], lay["rope"], lay["c_lat"]
    vd = lp["w_uv"].shape[2]
    scale = (nope + rope) ** -0.5

    z = _proj(x2d, lp["g_mix"], lp["w_in_p"])

    cos_t, sin_t = _rope_tables(pos, nope, rope)
    gq = _lane_vec([lp["g_qn"], lp["g_qr"]])
    gkr = _lane_vec([jnp.zeros((nope,), F32), lp["g_kr"]])
    gkv = lp["g_kv"].reshape(1, c_lat)
    if prompt:
        oc, qn, ckv_n, kr_n, vt = _prep(z, lay, cos_t, sin_t, lp["conv_w"], gq, gkr, gkv, seq_len, True)
        nblk_seq = seq_len // (t // vt.shape[0])
        new_hist = vt.reshape(bsz, nblk_seq, SUBLANES, bw)[:, -1, SUBLANES - 2:, :]
    else:
        tm = _tile(t, 512)
        reps = tm // seq_len
        cos_t = jnp.tile(cos_t, (reps, 1))
        sin_t = jnp.tile(sin_t, (reps, 1))
        zeros = jnp.zeros((bsz, seq_len - 1, bw), F32)
        hist1 = jnp.concatenate([conv_hist[:, 1:2], zeros], axis=1).reshape(t, bw)
        hist2 = jnp.concatenate([conv_hist, zeros[:, 1:]], axis=1).reshape(t, bw)
        oc, qn, ckv_n, kr_n, vt = _prep(z, lay, cos_t, sin_t, lp["conv_w"], gq, gkr, gkv, seq_len, False, hist1, hist2,
                                        q_dtype=F32)
        new_hist = vt.reshape(bsz, seq_len, bw)[:, seq_len - 2:, :]

    lc = math.gcd(seq_len, 16)
    mats = _s5_mats(lp["ssm_lam_re"], lp["ssm_lam_im"], lp["ssm_log_dt"], lp["ssm_b_re"], lp["ssm_b_im"],
                    lp["ssm_c_re"], lp["ssm_c_im"], lc)
    us = z[:, 3 * bw:4 * bw]
    ys, ht_re, ht_im = _s5(us, h0_re, h0_im, mats, bsz, seq_len, lc)

    gkn = _lane_vec([lp["g_kn"]])
    if prompt:
        k_full, v_all = _kprep(ckv_n, kr_n, lp["w_uk_p"], lp["w_uv_f"], gkn, heads, nope)
        oa = _flash(qn, k_full, v_all, bsz, seq_len, heads, vd, scale)
    else:
        cache_ckv, cache_krope, page_table, layer = sample_ctx
        q4 = qn.reshape(bsz, seq_len, heads, LANES)
        eye = jnp.eye(heads, dtype=F32)
        qbd = (q4[..., :nope][:, :, :, None, :] * eye[None, None, :, :, None]).reshape(bsz, seq_len * heads, heads * nope)
        qr = q4[..., nope:nope + rope].reshape(bsz, seq_len * heads, rope)
        kn = -(-seq_len // 16) * 16
        c_new = jnp.pad(ckv_n.reshape(bsz, seq_len, c_lat), ((0, 0), (0, kn - seq_len), (0, 0)))
        r_new = jnp.pad(kr_n[:, nope:nope + rope].reshape(bsz, seq_len, rope), ((0, 0), (0, kn - seq_len), (0, 0)))
        e_mat = jnp.repeat(jnp.eye(heads, dtype=F32), nope, axis=1).astype(BF16)
        gk = jnp.tile(lp["g_kn"], heads).reshape(1, heads * nope)
        oa = _paged(page_table, qbd, qr, c_new, r_new, cache_ckv, cache_krope, layer, lp["w_uk_f"], lp["w_uv_f"], e_mat,
                    gk, heads, nope, vd, scale).reshape(t, heads * vd)

    x1, h2, qp = _merge(x2d, oc, ys, z, oa, lay, lp["ssm_d"].reshape(1, bw), lp["w_glu_b"], lp["b_glu"].reshape(1, bw),
                        lp["w_br_b"], lp["w_o_b"], lp["g_ffn"].reshape(1, dm), lp["w_pq_b"])
    e_pt, g_pt = _route(qp, lp["peer_keys_b"], dm // (2 * LANES))
    x2 = _peer(x1, h2, e_pt, g_pt, lp["peer_u_pk"], lp["peer_v_pk"])
    return x2, new_hist, ht_re, ht_im, ckv_n, kr_n[:, nope:nope + rope]


def kernel(x_prompt, x_sample, cache_conv, state_ssm_re, state_ssm_im, cache_ckv, cache_krope, page_table, g_mix, w_in, conv_w, ssm_lam_re, ssm_lam_im, ssm_log_dt, ssm_b_re, ssm_b_im, ssm_c_re, ssm_c_im, ssm_d, w_glu, b_glu, g_kv, w_uk, w_uv, g_qn, g_kn, g_qr, g_kr, w_br, w_o, g_ffn, w_pq, peer_keys, peer_u, peer_v):
    bp, lp_len, dm = x_prompt.shape
    bs, ls, _ = x_sample.shape
    depth = w_in.shape[0]
    bw = conv_w.shape[2]
    c_lat, heads, nope = w_uk.shape[1], w_uk.shape[2], w_uk.shape[3]
    vd = w_uv.shape[3]
    rope = cache_krope.shape[3]
    page = cache_ckv.shape[2]
    past = page_table.shape[1] * page
    n_groups, p_state = ssm_lam_re.shape[1], ssm_lam_re.shape[2]
    assert ssm_d.shape[1] == bw and lp_len % page == 0
    lay = _layout(dm, bw, heads, nope, rope, c_lat)

    pos_p = jnp.arange(lp_len, dtype=jnp.int32)
    pos_s = past + jnp.arange(ls, dtype=jnp.int32)
    xp = x_prompt.reshape(bp * lp_len, dm)
    xs = x_sample.reshape(bs * ls, dm)
    outs = {k: [] for k in ("p_conv", "p_re", "p_im", "p_ckv", "p_kr", "s_conv", "s_re", "s_im", "s_ckv", "s_kr")}
    for l in range(depth):
        lp = dict(
            g_mix=g_mix[l], conv_w=conv_w[l], ssm_lam_re=ssm_lam_re[l], ssm_lam_im=ssm_lam_im[l],
            ssm_log_dt=ssm_log_dt[l], ssm_b_re=ssm_b_re[l], ssm_b_im=ssm_b_im[l], ssm_c_re=ssm_c_re[l],
            ssm_c_im=ssm_c_im[l], ssm_d=ssm_d[l], b_glu=b_glu[l], g_kv=g_kv[l], g_qn=g_qn[l], g_kn=g_kn[l],
            g_qr=g_qr[l], g_kr=g_kr[l], g_ffn=g_ffn[l], w_uv=w_uv[l],
        )
        lp["peer_u_pk"] = _pack_table_rows(peer_u[l])
        lp["peer_v_pk"] = _pack_table(peer_v[l])
        lp["w_in_p"] = _pack_w_in(w_in[l], lay, dm)
        lp["w_uk_p"] = jnp.pad(w_uk[l], ((0, 0), (0, 0), (0, LANES - nope))).reshape(c_lat, heads * LANES).astype(BF16)
        lp["w_uk_f"] = w_uk[l].reshape(c_lat, heads * nope).astype(BF16)
        lp["w_uv_f"] = w_uv[l].reshape(c_lat, heads * vd).astype(BF16)
        lp["w_glu_b"] = w_glu[l].astype(BF16)
        lp["w_br_b"] = w_br[l].astype(BF16)
        lp["w_o_b"] = w_o[l].astype(BF16)
        lp["w_pq_b"] = w_pq[l].astype(BF16)
        lp["peer_keys_b"] = peer_keys[l].astype(BF16)

        zeros_state = jnp.zeros((bp, n_groups, p_state), F32)
        xp, hc, hr, hi, ck, kr = _layer(xp, bp, lp_len, pos_p, lp, lay, True, None, zeros_state, zeros_state, None)
        outs["p_conv"].append(hc)
        outs["p_re"].append(hr)
        outs["p_im"].append(hi)
        outs["p_ckv"].append(ck.reshape(bp, lp_len // page, page, c_lat))
        outs["p_kr"].append(kr.reshape(bp, lp_len // page, page, rope))
        xs, hc, hr, hi, ck, kr = _layer(xs, bs, ls, pos_s, lp, lay, False, cache_conv[l], state_ssm_re[l],
                                        state_ssm_im[l], (cache_ckv, cache_krope, page_table, l))
        outs["s_conv"].append(hc)
        outs["s_re"].append(hr)
        outs["s_im"].append(hi)
        outs["s_ckv"].append(ck.reshape(bs, ls, c_lat))
        outs["s_kr"].append(kr.reshape(bs, ls, rope))
    st = {k: jnp.stack(v) for k, v in outs.items()}
    return (xp.reshape(bp, lp_len, dm), xs.reshape(bs, ls, dm),
            st["p_conv"], st["p_re"], st["p_im"], st["p_ckv"], st["p_kr"],
            st["s_conv"], st["s_re"], st["s_im"], st["s_ckv"], st["s_kr"])
```

```python
import functools
import math

import jax
import jax.numpy as jnp
from jax import lax
from jax.experimental import pallas as pl
from jax.experimental.pallas import tpu as pltpu

EPS = 1e-6
ROPE_THETA = 10000.0
NEG_INF = -1e30
PEER_TOPK = 16
S5_CHUNK = 16

LANES = 128
SUBLANES = 8
VMEM_LIMIT = 48 * 1024 * 1024
VMEM_LIMIT_TABLE = 56 * 1024 * 1024

F32 = jnp.float32
BF16 = jnp.bfloat16


def _tile(n, pref):
    if n <= pref:
        return n
    t = pref - pref % SUBLANES
    while t >= SUBLANES:
        if n % t == 0:
            return t
        t -= SUBLANES
    return n


def _cparams(sem, limit=VMEM_LIMIT):
    return pltpu.CompilerParams(dimension_semantics=sem, vmem_limit_bytes=limit)


def _proj_kernel(x_ref, g_ref, w_ref, z_ref, h_ref):
    @pl.when(pl.program_id(1) == 0)
    def _():
        x = x_ref[...]
        ms = jnp.mean(x * x, axis=-1, keepdims=True)
        h_ref[...] = (x * lax.rsqrt(ms + EPS) * g_ref[...]).astype(BF16)

    z_ref[...] = jnp.dot(h_ref[...], w_ref[...], preferred_element_type=F32)


def _proj(x2d, g, w_p):
    t, d = x2d.shape
    n = w_p.shape[1]
    tm = _tile(t, 512)
    nb = n // LANES
    k = max(c for c in range(1, nb + 1) if nb % c == 0 and c * LANES <= 2304)
    tn = k * LANES
    return pl.pallas_call(
        _proj_kernel,
        grid=(t // tm, n // tn),
        in_specs=[
            pl.BlockSpec((tm, d), lambda i, j: (i, 0)),
            pl.BlockSpec((1, d), lambda i, j: (0, 0)),
            pl.BlockSpec((d, tn), lambda i, j: (0, j)),
        ],
        out_specs=pl.BlockSpec((tm, tn), lambda i, j: (i, j)),
        out_shape=jax.ShapeDtypeStruct((t, n), F32),
        scratch_shapes=[pltpu.VMEM((tm, d), BF16)],
        compiler_params=_cparams(("parallel", "arbitrary")),
        name="proj",
    )(x2d, g.reshape(1, d), w_p)


def _norm_rope(x, gain, cos, sin, nope, rope):
    lane = lax.broadcasted_iota(jnp.int32, (1, LANES), 1)
    m_n = lane < nope
    m_r = (lane >= nope) & (lane < nope + rope)
    sq = x * x
    ss_r = jnp.sum(jnp.where(m_r, sq, 0.0), axis=-1, keepdims=True)
    inv_r = lax.rsqrt(ss_r / rope + EPS)
    if nope:
        ss_n = jnp.sum(jnp.where(m_n, sq, 0.0), axis=-1, keepdims=True)
        inv = jnp.where(m_n, lax.rsqrt(ss_n / nope + EPS), inv_r)
    else:
        inv = inv_r
    y = x * inv * gain
    half = rope // 2
    first = lane < nope + half
    partner = jnp.where(first, pltpu.roll(y, LANES - half, 1), pltpu.roll(y, half, 1))
    return y * cos + partner * sin


def _prep_kernel(*refs, heads, nope, rope, seq_len, tm, prompt):
    if prompt:
        (xc_ref, bc_ref, cc_ref, q_ref, ckv_ref, kr_ref, cos_ref, sin_ref, cw_ref, gq_ref, gkr_ref, gkv_ref,
         hxc_ref, hcc_ref, oc_ref, qo_ref, ckvo_ref, kro_ref, vt_ref) = refs
    else:
        (xc_ref, bc_ref, cc_ref, q_ref, ckv_ref, kr_ref, cos_ref, sin_ref, cw_ref, gq_ref, gkr_ref, gkv_ref,
         h1_ref, h2_ref, oc_ref, qo_ref, ckvo_ref, kro_ref, vt_ref) = refs
    i = pl.program_id(0)
    v = cc_ref[...] * xc_ref[...]
    row = lax.broadcasted_iota(jnp.int32, (tm, 1), 0)
    r1 = pltpu.roll(v, 1, 0)
    r2 = pltpu.roll(v, 2, 0)
    if prompt:
        hv = hcc_ref[...] * hxc_ref[...]
        hv = jnp.where((i % (seq_len // tm)) == 0, 0.0, hv)
        v1 = jnp.where(row == 0, hv[7:8, :], r1)
        v2 = jnp.where(row == 0, hv[6:7, :], jnp.where(row == 1, hv[7:8, :], r2))
        vt_ref[0] = v[tm - SUBLANES:, :]
    else:
        l = row % seq_len
        v1 = jnp.where(l >= 1, r1, h1_ref[...])
        v2 = jnp.where(l >= 2, r2, h2_ref[...])
        vt_ref[...] = v
    y = v2 * cw_ref[0:1, :] + v1 * cw_ref[1:2, :] + v * cw_ref[2:3, :]
    oc_ref[...] = (bc_ref[...] * y).astype(oc_ref.dtype)

    cos = cos_ref[...]
    sin = sin_ref[...]
    gq = gq_ref[...]
    for h in range(heads):
        sl = slice(LANES * h, LANES * (h + 1))
        qo_ref[:, sl] = _norm_rope(q_ref[:, sl], gq, cos, sin, nope, rope).astype(qo_ref.dtype)
    kro_ref[...] = _kr_norm_rope(kr_ref[...], gkr_ref[...], cos, sin, nope, rope)
    c = ckv_ref[...]
    ms = jnp.mean(c * c, axis=-1, keepdims=True)
    ckvo_ref[...] = c * lax.rsqrt(ms + EPS) * gkv_ref[...]


def _kr_norm_rope(x, gain, cos, sin, nope, rope):
    lane = lax.broadcasted_iota(jnp.int32, (1, LANES), 1)
    ss = jnp.sum(x * x, axis=-1, keepdims=True)
    y = x * lax.rsqrt(ss / rope + EPS) * gain
    half = rope // 2
    first = lane < nope + half
    partner = jnp.where(first, pltpu.roll(y, LANES - half, 1), pltpu.roll(y, half, 1))
    return y * cos + partner * sin


def _prep(z, lay, cos_t, sin_t, conv_w, gq, gkr, gkv, seq_len, prompt, hist1=None, hist2=None, q_dtype=BF16):
    t = z.shape[0]
    bw, heads, c_lat = lay["bw"], lay["heads"], lay["c_lat"]
    hq = heads * LANES
    if prompt:
        tm = _tile(seq_len, 512)
        assert seq_len % tm == 0 and tm >= 2 * SUBLANES
    else:
        tm = _tile(t, 512)
        assert tm % seq_len == 0
    ntab = cos_t.shape[0] // tm
    in_specs = [
        pl.BlockSpec((tm, bw), lambda i: (i, 0)),
        pl.BlockSpec((tm, bw), lambda i: (i, 1)),
        pl.BlockSpec((tm, bw), lambda i: (i, 2)),
        pl.BlockSpec((tm, hq), lambda i: (i, lay["off_q"] // hq)),
        pl.BlockSpec((tm, c_lat), lambda i: (i, lay["off_ckv"] // c_lat)),
        pl.BlockSpec((tm, LANES), lambda i: (i, lay["off_kr"] // LANES)),
        pl.BlockSpec((tm, LANES), lambda i: (i % ntab, 0)),
        pl.BlockSpec((tm, LANES), lambda i: (i % ntab, 0)),
        pl.BlockSpec(conv_w.shape, lambda i: (0, 0)),
        pl.BlockSpec((1, LANES), lambda i: (0, 0)),
        pl.BlockSpec((1, LANES), lambda i: (0, 0)),
        pl.BlockSpec((1, c_lat), lambda i: (0, 0)),
    ]
    args = [z, z, z, z, z, z, cos_t, sin_t, conv_w, gq, gkr, gkv]
    if prompt:
        rb = tm // SUBLANES
        in_specs += [
            pl.BlockSpec((SUBLANES, bw), lambda i: (jnp.maximum(i * rb - 1, 0), 0)),
            pl.BlockSpec((SUBLANES, bw), lambda i: (jnp.maximum(i * rb - 1, 0), 2)),
        ]
        args += [z, z]
        vt_spec = pl.BlockSpec((1, SUBLANES, bw), lambda i: (i, 0, 0))
        vt_shape = jax.ShapeDtypeStruct((t // tm, SUBLANES, bw), F32)
    else:
        in_specs += [pl.BlockSpec((tm, bw), lambda i: (i, 0)), pl.BlockSpec((tm, bw), lambda i: (i, 0))]
        args += [hist1, hist2]
        vt_spec = pl.BlockSpec((tm, bw), lambda i: (i, 0))
        vt_shape = jax.ShapeDtypeStruct((t, bw), F32)
    kern = functools.partial(_prep_kernel, heads=heads, nope=lay["nope"], rope=lay["rope"], seq_len=seq_len, tm=tm,
                             prompt=prompt)
    return pl.pallas_call(
        kern,
        grid=(t // tm,),
        in_specs=in_specs,
        out_specs=[
            pl.BlockSpec((tm, bw), lambda i: (i, 0)),
            pl.BlockSpec((tm, hq), lambda i: (i, 0)),
            pl.BlockSpec((tm, c_lat), lambda i: (i, 0)),
            pl.BlockSpec((tm, LANES), lambda i: (i, 0)),
            vt_spec,
        ],
        out_shape=[
            jax.ShapeDtypeStruct((t, bw), BF16),
            jax.ShapeDtypeStruct((t, hq), q_dtype),
            jax.ShapeDtypeStruct((t, c_lat), F32),
            jax.ShapeDtypeStruct((t, LANES), F32),
            vt_shape,
        ],
        compiler_params=_cparams(("parallel",)),
        name="prep_prompt" if prompt else "prep_sample",
    )(*args)


def _s5_mats(lam_re, lam_im, log_dt, b_re, b_im, c_re, c_im, lc):
    hp = lax.Precision.HIGHEST
    g, p, n_in = b_re.shape
    n_out = c_re.shape[1]
    dt = jnp.exp(log_dt)[:, None]
    lr, li = lam_re, lam_im
    mag = jnp.exp(lr * dt)
    a_re, a_im = mag * jnp.cos(li * dt), mag * jnp.sin(li * dt)
    den = lr * lr + li * li
    f_re = ((a_re - 1.0) * lr + a_im * li) / den
    f_im = (a_im * lr - (a_re - 1.0) * li) / den
    bb_re = f_re[..., None] * b_re - f_im[..., None] * b_im
    bb_im = f_re[..., None] * b_im + f_im[..., None] * b_re
    k = jnp.arange(lc + 1, dtype=F32)[:, None, None]
    pm = jnp.exp(lr[None] * dt[None] * k)
    pr = pm * jnp.cos(li[None] * dt[None] * k)
    pi = pm * jnp.sin(li[None] * dt[None] * k)
    ab_re = pr[:lc, ..., None] * bb_re[None] - pi[:lc, ..., None] * bb_im[None]
    ab_im = pr[:lc, ..., None] * bb_im[None] + pi[:lc, ..., None] * bb_re[None]
    ms = jnp.concatenate([ab_re[::-1], ab_im[::-1]], axis=2)
    ms = jnp.transpose(ms, (1, 0, 3, 2)).reshape(g, lc * n_in, 2 * p)
    kk = (jnp.einsum("gop,dgpi->dgoi", c_re, ab_re, precision=hp)
          - jnp.einsum("gop,dgpi->dgoi", c_im, ab_im, precision=hp))
    s_idx = jnp.arange(lc)[:, None]
    t_idx = jnp.arange(lc)[None, :]
    delta = t_idx - s_idx
    kt = jnp.where((delta >= 0)[:, :, None, None, None], kk[jnp.clip(delta, 0, lc - 1)], 0.0)
    tk = jnp.transpose(kt, (2, 0, 4, 1, 3)).reshape(g, lc * n_in, lc * n_out)
    ca_re = c_re[None] * pr[1:, :, None, :] - c_im[None] * pi[1:, :, None, :]
    ca_im = c_re[None] * pi[1:, :, None, :] + c_im[None] * pr[1:, :, None, :]
    gs = jnp.concatenate([ca_re, -ca_im], axis=3)
    gs = jnp.transpose(gs, (1, 3, 0, 2)).reshape(g, 2 * p, lc * n_out)
    return dict(ms=ms.astype(BF16), tk=tk.astype(BF16), gs=gs.astype(BF16), pr=pr, pi=pi, lc=lc, n_in=n_in, n_out=n_out)


def _s5_ops(m, lc):
    full, n_in, n_out = m["lc"], m["n_in"], m["n_out"]
    assert lc <= full
    ms = m["ms"][:, (full - lc) * n_in:, :]
    tk = m["tk"][:, :lc * n_in, :lc * n_out]
    gs = m["gs"][:, :, :lc * n_out]
    a1 = jnp.concatenate([m["pr"][lc], m["pr"][lc]], axis=-1)[:, None, :]
    a2 = jnp.concatenate([-m["pi"][lc], m["pi"][lc]], axis=-1)[:, None, :]
    return ms, tk, gs, a1, a2


def _s5_kernel(u_ref, tk_ref, ms_ref, gs_ref, a1_ref, a2_ref, h0_ref, y_ref, ht_ref, s_sc, sw_sc, hin_sc, *, n_blocks,
               bsz, spb, p_state):
    u = u_ref[...]
    s = jnp.dot(u, ms_ref[...], preferred_element_type=F32)
    s_sc[...] = s
    sw_sc[...] = pltpu.roll(s, p_state, 1)
    a1 = a1_ref[...]
    a2 = a2_ref[...]
    rb = spb * bsz

    def body(k, carry):
        h, hw = carry
        r = pl.multiple_of(k * rb, SUBLANES)
        s_blk = s_sc[pl.ds(r, rb), :]
        sw_blk = sw_sc[pl.ds(r, rb), :]
        hs = []
        for q in range(spb):
            hs.append(h)
            rows = slice(q * bsz, (q + 1) * bsz)
            h, hw = a1 * h + a2 * hw + s_blk[rows, :], a1 * hw - a2 * h + sw_blk[rows, :]
        hin_sc[pl.ds(r, rb), :] = hs[0] if spb == 1 else jnp.concatenate(hs, axis=0)
        return h, hw

    h0 = h0_ref[...]
    h, _ = lax.fori_loop(0, n_blocks, body, (h0, pltpu.roll(h0, p_state, 1)))
    ht_ref[...] = h
    y_ref[...] = (jnp.dot(u, tk_ref[...], preferred_element_type=F32)
                  + jnp.dot(hin_sc[...].astype(BF16), gs_ref[...], preferred_element_type=F32))


def _s5(us, h0_re, h0_im, mats, bsz, seq_len, lc):
    ms, tk, gs, a1, a2 = mats
    g, _, p2 = ms.shape
    p_state = p2 // 2
    n_in = ms.shape[1] // lc
    n_out = tk.shape[2] // lc
    n_chunks = seq_len // lc
    spb = max(1, SUBLANES // bsz)
    assert (spb * bsz) % SUBLANES == 0 and n_chunks % spb == 0
    nc = n_chunks * bsz
    bp = bsz
    u = us.astype(BF16).reshape(bsz, n_chunks, lc, g, n_in)
    u = jnp.transpose(u, (3, 1, 0, 2, 4)).reshape(g, nc, lc * n_in)
    h0 = jnp.transpose(jnp.concatenate([h0_re, h0_im], axis=-1), (1, 0, 2))
    kern = functools.partial(_s5_kernel, n_blocks=n_chunks // spb, bsz=bsz, spb=spb, p_state=p_state)
    y, ht = pl.pallas_call(
        kern,
        grid=(g,),
        in_specs=[
            pl.BlockSpec((None, nc, lc * n_in), lambda i: (i, 0, 0)),
            pl.BlockSpec((None, lc * n_in, lc * n_out), lambda i: (i, 0, 0)),
            pl.BlockSpec((None, lc * n_in, p2), lambda i: (i, 0, 0)),
            pl.BlockSpec((None, p2, lc * n_out), lambda i: (i, 0, 0)),
            pl.BlockSpec((None, 1, p2), lambda i: (i, 0, 0)),
            pl.BlockSpec((None, 1, p2), lambda i: (i, 0, 0)),
            pl.BlockSpec((None, bp, p2), lambda i: (i, 0, 0)),
        ],
        out_specs=[
            pl.BlockSpec((None, nc, lc * n_out), lambda i: (i, 0, 0)),
            pl.BlockSpec((None, bp, p2), lambda i: (i, 0, 0)),
        ],
        out_shape=[
            jax.ShapeDtypeStruct((g, nc, lc * n_out), F32),
            jax.ShapeDtypeStruct((g, bp, p2), F32),
        ],
        scratch_shapes=[pltpu.VMEM((nc, p2), F32), pltpu.VMEM((nc, p2), F32), pltpu.VMEM((nc, p2), F32)],
        compiler_params=_cparams(("parallel",)),
        name="s5",
    )(u, tk, ms, gs, a1, a2, h0)
    y = y.reshape(g, n_chunks, bp, lc, n_out)[:, :, :bsz]
    y = jnp.transpose(y, (2, 1, 3, 0, 4)).reshape(bsz * seq_len, g * n_out)
    ht = jnp.transpose(ht[:, :bsz], (1, 0, 2))
    return y, ht[..., :p_state], ht[..., p_state:]


def _kprep_kernel(ckv_ref, kr_ref, wuk_ref, wuv_ref, gkn_ref, k_ref, v_ref, *, heads, nope):
    c = ckv_ref[...].astype(BF16)
    kraw = jnp.dot(c, wuk_ref[...], preferred_element_type=F32)
    kr = kr_ref[...]
    gkn = gkn_ref[...]
    for h in range(heads):
        sl = slice(LANES * h, LANES * (h + 1))
        kh = kraw[:, sl]
        ss = jnp.sum(kh * kh, axis=-1, keepdims=True)
        k_ref[:, sl] = (kh * lax.rsqrt(ss / nope + EPS) * gkn + kr).astype(BF16)
    v_ref[...] = jnp.dot(c, wuv_ref[...], preferred_element_type=F32).astype(BF16)


def _kprep(ckv_n, kr_n, wuk_p, wuv, gkn, heads, nope):
    t, c_lat = ckv_n.shape
    tm = _tile(t, 512)
    hk = wuk_p.shape[1]
    hv = wuv.shape[1]
    return pl.pallas_call(
        functools.partial(_kprep_kernel, heads=heads, nope=nope),
        grid=(t // tm,),
        in_specs=[
            pl.BlockSpec((tm, c_lat), lambda i: (i, 0)),
            pl.BlockSpec((tm, LANES), lambda i: (i, 0)),
            pl.BlockSpec((c_lat, hk), lambda i: (0, 0)),
            pl.BlockSpec((c_lat, hv), lambda i: (0, 0)),
            pl.BlockSpec((1, LANES), lambda i: (0, 0)),
        ],
        out_specs=[pl.BlockSpec((tm, hk), lambda i: (i, 0)), pl.BlockSpec((tm, hv), lambda i: (i, 0))],
        out_shape=[jax.ShapeDtypeStruct((t, hk), BF16), jax.ShapeDtypeStruct((t, hv), BF16)],
        compiler_params=_cparams(("parallel",)),
        name="kprep",
    )(ckv_n, kr_n, wuk_p, wuv, gkn)


def _flash_kernel(ii_ref, jj_ref, q_ref, k_ref, v_ref, o_ref, m_sc, l_sc, acc_sc, *, c_exp, tq, hps, vd):
    i = ii_ref[pl.program_id(2)]
    j = jj_ref[pl.program_id(2)]
    nct = tq // LANES

    @pl.when(j == 0)
    def _():
        m_sc[...] = jnp.full(m_sc.shape, NEG_INF, F32)
        l_sc[...] = jnp.zeros(l_sc.shape, F32)
        acc_sc[...] = jnp.zeros(acc_sc.shape, F32)

    def step(diagonal):
        if diagonal:
            row = lax.broadcasted_iota(jnp.int32, (tq, tq), 0)
            col = lax.broadcasted_iota(jnp.int32, (tq, tq), 1)
            keep = col <= row
        for hh in range(hps):
            q = q_ref[:, LANES * hh:LANES * (hh + 1)]
            k = k_ref[:, LANES * hh:LANES * (hh + 1)]
            s = lax.dot_general(q, k, (((1,), (1,)), ((), ())), preferred_element_type=F32)
            if diagonal:
                s = jnp.where(keep, s, NEG_INF)
            m_prev = m_sc[hh]
            m_new = jnp.maximum(m_prev, jnp.max(s, axis=-1, keepdims=True))
            alpha = jnp.exp2((m_prev - m_new) * c_exp)
            ps = [jnp.exp2((s[:, LANES * c:LANES * (c + 1)] - m_new) * c_exp) for c in range(nct)]
            psum = ps[0]
            for c in range(1, nct):
                psum = psum + ps[c]
            l_sc[hh] = alpha * l_sc[hh] + jnp.sum(psum, axis=-1, keepdims=True)
            p = jnp.concatenate(ps, axis=1).astype(BF16)
            acc_sc[hh] = alpha[:, :vd] * acc_sc[hh] + jnp.dot(p, v_ref[:, vd * hh:vd * (hh + 1)],
                                                              preferred_element_type=F32)
            m_sc[hh] = m_new

    @pl.when(j < i)
    def _():
        step(False)

    @pl.when(j == i)
    def _():
        step(True)
        for hh in range(hps):
            o_ref[:, vd * hh:vd * (hh + 1)] = (acc_sc[hh] / l_sc[hh][:, :vd]).astype(o_ref.dtype)


def _flash(q, k, v, bsz, seq_len, heads, vd, scale):
    hps = LANES // vd
    assert heads % hps == 0
    tq = _tile(seq_len, 512)
    nq = seq_len // tq
    t = bsz * seq_len
    assert tq % LANES == 0
    kern = functools.partial(_flash_kernel, c_exp=scale * math.log2(math.e), tq=tq, hps=hps, vd=vd)
    pairs = [(i, j) for i in range(nq) for j in range(i + 1)]
    ii = jnp.asarray([p[0] for p in pairs], jnp.int32)
    jj = jnp.asarray([p[1] for p in pairs], jnp.int32)
    grid_spec = pltpu.PrefetchScalarGridSpec(
        num_scalar_prefetch=2,
        grid=(bsz, heads // hps, len(pairs)),
        in_specs=[
            pl.BlockSpec((tq, hps * LANES), lambda b, h, s, ii, jj: (b * nq + ii[s], h)),
            pl.BlockSpec((tq, hps * LANES), lambda b, h, s, ii, jj: (b * nq + jj[s], h)),
            pl.BlockSpec((tq, LANES), lambda b, h, s, ii, jj: (b * nq + jj[s], h)),
        ],
        out_specs=pl.BlockSpec((tq, LANES), lambda b, h, s, ii, jj: (b * nq + ii[s], h)),
        scratch_shapes=[pltpu.VMEM((hps, tq, LANES), F32), pltpu.VMEM((hps, tq, LANES), F32),
                        pltpu.VMEM((hps, tq, vd), F32)],
    )
    return pl.pallas_call(
        kern,
        grid_spec=grid_spec,
        out_shape=jax.ShapeDtypeStruct((t, heads * vd), BF16),
        compiler_params=_cparams(("parallel", "parallel", "arbitrary")),
        name="flash",
    )(ii, jj, q, k, v)


def _paged_kernel(pt_ref, qbd_ref, qr_ref, cn_ref, rn_ref, *refs, pp, heads, nope, vd, lq, scale):
    del pt_ref
    c_pages = refs[:pp]
    r_pages = refs[pp:2 * pp]
    wuk_ref, wuv_ref, e_ref, gk_ref, o_ref, m_sc, l_sc, acc_sc, cb_sc, rb_sc = refs[2 * pp:]
    s_id = pl.program_id(1)
    rows = lq * heads
    page = c_pages[0].shape[0]

    @pl.when(s_id == 0)
    def _():
        m_sc[...] = jnp.full(m_sc.shape, NEG_INF, F32)
        l_sc[...] = jnp.zeros(l_sc.shape, F32)
        acc_sc[...] = jnp.zeros(acc_sc.shape, F32)

    qbd = (qbd_ref[...] * gk_ref[...]).astype(BF16)
    qr = qr_ref[...].astype(BF16)
    dn = (((1,), (1,)), ((), ()))

    def scores(cb, rb):
        kraw = jnp.dot(cb, wuk_ref[...], preferred_element_type=F32)
        ssq = lax.dot_general(e_ref[...], (kraw * kraw).astype(BF16), dn, preferred_element_type=F32)
        inv = lax.rsqrt(ssq / nope + EPS)
        inv = jnp.concatenate([inv] * lq, axis=0)
        sn = lax.dot_general(qbd, kraw.astype(BF16), dn, preferred_element_type=F32)
        sr = jnp.dot(qr, rb, preferred_element_type=F32)
        return (sn * inv + sr) * scale

    def update(cb, rb, causal):
        s = scores(cb, rb)
        if causal:
            nk = cb.shape[0]
            kk = lax.broadcasted_iota(jnp.int32, (rows, nk), 1)
            qq = lax.broadcasted_iota(jnp.int32, (rows, nk), 0) // heads
            s = jnp.where(kk <= qq, s, NEG_INF)
        m_prev = m_sc[...]
        m_new = jnp.maximum(m_prev, jnp.max(s, axis=-1, keepdims=True))
        alpha = jnp.exp(m_prev - m_new)
        p = jnp.exp(s - m_new)
        l_sc[...] = alpha * l_sc[...] + jnp.sum(p, axis=-1, keepdims=True)
        acc_sc[...] = alpha * acc_sc[...] + jnp.dot(p.astype(BF16), cb, preferred_element_type=F32)
        m_sc[...] = m_new

    for pg in range(pp):
        cb_sc[page * pg:page * (pg + 1), :] = c_pages[pg][...].astype(BF16)
        rb_sc[:, page * pg:page * (pg + 1)] = r_pages[pg][...].astype(BF16)
    update(cb_sc[...], rb_sc[...], False)

    @pl.when(s_id == pl.num_programs(1) - 1)
    def _():
        update(cn_ref[...].astype(BF16), rn_ref[...].astype(BF16), True)
        lat = (acc_sc[...] / l_sc[...]).astype(BF16)
        full = jnp.dot(lat, wuv_ref[...], preferred_element_type=F32)
        colh = lax.broadcasted_iota(jnp.int32, full.shape, 1) // vd
        rowh = lax.broadcasted_iota(jnp.int32, full.shape, 0) % heads
        full = jnp.where(colh == rowh, full, 0.0)
        o_ref[...] = jnp.sum(full.reshape(lq, heads, heads * vd), axis=1)


def _paged(page_table, qbd, qr, c_new, r_new, cache_ckv, cache_krope, layer, wuk, wuv, e_mat, gk, heads, nope, vd,
           scale):
    bs, rows, _ = qbd.shape
    lq = rows // heads
    n_pages = page_table.shape[1]
    page, c_lat = cache_ckv.shape[2], cache_ckv.shape[3]
    rope = cache_krope.shape[3]
    pp = math.gcd(n_pages, 16)
    kn = c_new.shape[1]
    cache_krope = jnp.swapaxes(cache_krope, 2, 3)
    r_new = jnp.swapaxes(r_new, 1, 2)

    def cmap(p):
        return lambda b, s, pt: (layer, pt[b, s * pp + p], 0, 0)

    in_specs = [
        pl.BlockSpec((None, rows, heads * nope), lambda b, s, pt: (b, 0, 0)),
        pl.BlockSpec((None, rows, rope), lambda b, s, pt: (b, 0, 0)),
        pl.BlockSpec((None, kn, c_lat), lambda b, s, pt: (b, 0, 0)),
        pl.BlockSpec((None, rope, kn), lambda b, s, pt: (b, 0, 0)),
    ]
    in_specs += [pl.BlockSpec((None, None, page, c_lat), cmap(p)) for p in range(pp)]
    in_specs += [pl.BlockSpec((None, None, rope, page), cmap(p)) for p in range(pp)]
    in_specs += [
        pl.BlockSpec(wuk.shape, lambda b, s, pt: (0, 0)),
        pl.BlockSpec(wuv.shape, lambda b, s, pt: (0, 0)),
        pl.BlockSpec(e_mat.shape, lambda b, s, pt: (0, 0)),
        pl.BlockSpec(gk.shape, lambda b, s, pt: (0, 0)),
    ]
    kern = functools.partial(_paged_kernel, pp=pp, heads=heads, nope=nope, vd=vd, lq=lq, scale=scale)
    grid_spec = pltpu.PrefetchScalarGridSpec(
        num_scalar_prefetch=1,
        grid=(bs, n_pages // pp),
        in_specs=in_specs,
        out_specs=pl.BlockSpec((None, lq, heads * vd), lambda b, s, pt: (b, 0, 0)),
        scratch_shapes=[pltpu.VMEM((rows, 1), F32), pltpu.VMEM((rows, 1), F32), pltpu.VMEM((rows, c_lat), F32),
                        pltpu.VMEM((pp * page, c_lat), BF16), pltpu.VMEM((rope, pp * page), BF16)],
    )
    return pl.pallas_call(
        kern,
        grid_spec=grid_spec,
        out_shape=jax.ShapeDtypeStruct((bs, lq, heads * vd), F32),
        compiler_params=_cparams(("parallel", "arbitrary")),
        name="paged",
    )(page_table, qbd, qr, c_new, r_new, *([cache_ckv] * pp), *([cache_krope] * pp), wuk, wuv, e_mat, gk)


def _merge_kernel(x_ref, oc_ref, ys_ref, us_ref, oa_ref, g0_ref, g1_ref, g2_ref, d_ref, wglu_ref, bglu_ref, wbr_ref,
                  wo_ref, gffn_ref, wpq_ref, x1_ref, h2_ref, qp_ref):
    y = ys_ref[...] + d_ref[...] * us_ref[...]
    zg = jax.nn.gelu(y)
    gl = jnp.dot(zg.astype(BF16), wglu_ref[...], preferred_element_type=F32) + bglu_ref[...]
    o_ssm = zg * jax.nn.sigmoid(gl)
    merged = jax.nn.sigmoid(g0_ref[...]) * jnp.dot(oc_ref[...], wbr_ref[0], preferred_element_type=F32)
    merged += jax.nn.sigmoid(g1_ref[...]) * jnp.dot(o_ssm.astype(BF16), wbr_ref[1], preferred_element_type=F32)
    merged += jax.nn.sigmoid(g2_ref[...]) * jnp.dot(oa_ref[...].astype(BF16), wbr_ref[2], preferred_element_type=F32)
    x1 = x_ref[...] + jnp.dot(merged.astype(BF16), wo_ref[...], preferred_element_type=F32)
    x1_ref[...] = x1
    ms = jnp.mean(x1 * x1, axis=-1, keepdims=True)
    h2 = x1 * lax.rsqrt(ms + EPS) * gffn_ref[...]
    h2_ref[...] = h2
    qp_ref[...] = jnp.dot(h2.astype(BF16), wpq_ref[...], preferred_element_type=F32)


def _merge(x2d, oc, ys, z, oa, lay, d, wglu, bglu, wbr, wo, gffn, wpq):
    t, dm = x2d.shape
    bw = lay["bw"]
    dq = wpq.shape[1]
    tm = _tile(t, 256)
    gi = lay["off_g"] // dm

    def full(a):
        return pl.BlockSpec(a.shape, lambda i, _n=a.ndim: (0,) * _n)

    return pl.pallas_call(
        _merge_kernel,
        grid=(t // tm,),
        in_specs=[
            pl.BlockSpec((tm, dm), lambda i: (i, 0)),
            pl.BlockSpec((tm, bw), lambda i: (i, 0)),
            pl.BlockSpec((tm, bw), lambda i: (i, 0)),
            pl.BlockSpec((tm, bw), lambda i: (i, 3)),
            pl.BlockSpec((tm, bw), lambda i: (i, 0)),
            pl.BlockSpec((tm, dm), lambda i: (i, gi)),
            pl.BlockSpec((tm, dm), lambda i: (i, gi + 1)),
            pl.BlockSpec((tm, dm), lambda i: (i, gi + 2)),
            full(d), full(wglu), full(bglu), full(wbr), full(wo), full(gffn), full(wpq),
        ],
        out_specs=[
            pl.BlockSpec((tm, dm), lambda i: (i, 0)),
            pl.BlockSpec((tm, dm), lambda i: (i, 0)),
            pl.BlockSpec((tm, dq), lambda i: (i, 0)),
        ],
        out_shape=[
            jax.ShapeDtypeStruct((t, dm), F32),
            jax.ShapeDtypeStruct((t, dm), F32),
            jax.ShapeDtypeStruct((t, dq), F32),
        ],
        compiler_params=_cparams(("parallel",)),
        name="merge",
    )(x2d, oc, ys, z, oa, z, z, z, d, wglu, bglu, wbr, wo, gffn, wpq)


def _topk_rows(x, k):
    n = x.shape[0]
    iota = lax.broadcasted_iota(jnp.int32, x.shape, 0)
    vals, idxs = [], []
    for _ in range(k):
        m = jnp.max(x, axis=0, keepdims=True)
        am = jnp.min(jnp.where(x == m, iota, n), axis=0, keepdims=True)
        vals.append(m)
        idxs.append(am)
        x = jnp.where(iota == am, -jnp.inf, x)
    return jnp.concatenate(vals, axis=0), jnp.concatenate(idxs, axis=0)


def _select_rows(table, sel, k):
    out = jnp.zeros(sel.shape, table.dtype)
    for r in range(k):
        out = jnp.where(sel == r, table[r:r + 1, :], out)
    return out


def _pair_candidates(v1, v2, k):
    chunks, meta, r0, i = [], [], 0, 0
    while i < k and k // (i + 1) >= 2:
        n = k // (i + 1)
        nr = -(-n // SUBLANES) * SUBLANES
        blk = v1[i:i + 1, :] + v2[0:nr, :]
        if n < nr:
            blk = jnp.where(lax.broadcasted_iota(jnp.int32, blk.shape, 0) < n, blk, -jnp.inf)
        chunks.append(blk)
        meta.append((r0, nr, i, None))
        r0 += nr
        i += 1
    assert (k - i) % SUBLANES == 0
    chunks.append(v1[i:k, :] + v2[0:1, :])
    meta.append((r0, k - i, None, i))
    return jnp.concatenate(chunks, axis=0), meta


def _route_kernel(qp_ref, keys_ref, e_ref, g_ref, *, heads, n_keys, dk, topk, row_mult):
    dn = (((1,), (1,)), ((), ()))
    for h in range(heads):
        sub = []
        for s in range(2):
            o = (2 * h + s) * dk
            qs = qp_ref[:, o:o + dk].astype(BF16)
            st = lax.dot_general(keys_ref[h, s], qs, dn, preferred_element_type=F32)
            sub.append(_topk_rows(st, topk))
        (v1, i1), (v2, i2) = sub
        cand, meta = _pair_candidates(v1, v2, topk)
        sc, ci = _topk_rows(cand, topk)
        ihi = jnp.zeros(ci.shape, jnp.int32)
        jlo = jnp.zeros(ci.shape, jnp.int32)
        for r0, nr, ic, i0 in meta:
            inr = (ci >= r0) & (ci < r0 + nr)
            if ic is None:
                ihi = jnp.where(inr, ci - r0 + i0, ihi)
            else:
                ihi = jnp.where(inr, ic, ihi)
                jlo = jnp.where(inr, ci - r0, jlo)
        e1 = _select_rows(i1, ihi, topk)
        e2 = _select_rows(i2, jlo, topk)
        ex = jnp.exp(sc - sc[0:1, :])
        e_ref[topk * h:topk * (h + 1), :] = (e1 * n_keys + e2) * row_mult
        g_ref[topk * h:topk * (h + 1), :] = ex / jnp.sum(ex, axis=0, keepdims=True)


def _route(qp, keys_bf16, row_mult):
    t, dq = qp.shape
    heads, _, n_keys, dk = keys_bf16.shape
    tm = _tile(t, 128)
    if t % LANES:
        tm = t
    npair = heads * PEER_TOPK
    kern = functools.partial(_route_kernel, heads=heads, n_keys=n_keys, dk=dk, topk=PEER_TOPK, row_mult=row_mult)
    return pl.pallas_call(
        kern,
        grid=(t // tm,),
        in_specs=[
            pl.BlockSpec((tm, dq), lambda i: (i, 0)),
            pl.BlockSpec(keys_bf16.shape, lambda i: (0, 0, 0, 0)),
        ],
        out_specs=[pl.BlockSpec((npair, tm), lambda i: (0, i)), pl.BlockSpec((npair, tm), lambda i: (0, i))],
        out_shape=[jax.ShapeDtypeStruct((npair, t), jnp.int32), jax.ShapeDtypeStruct((npair, t), F32)],
        compiler_params=_cparams(("parallel",)),
        name="route",
    )(qp, keys_bf16)


HI_MASK = 0xFFFF0000
_BUTTERFLY_ORDER = (0, 4, 2, 6, 1, 5, 3, 7)


def _pack_table(tab):
    e, d = tab.shape
    half = d // 2
    b = lax.bitcast_convert_type(tab.astype(BF16), jnp.uint16).astype(jnp.uint32)
    return (b[:, :half] | (b[:, half:] << 16)).reshape(e * (half // LANES), LANES)


def _pack_table_rows(tab):
    e, d = tab.shape
    b = lax.bitcast_convert_type(tab.astype(BF16), jnp.uint16).astype(jnp.uint32).reshape(e, d // (2 * LANES), 2, LANES)
    return (b[:, :, 0, :] | (b[:, :, 1, :] << 16)).reshape(e * (d // (2 * LANES)), LANES)


def _unpack(x):
    return pltpu.bitcast(x << 16, F32), pltpu.bitcast(x & jnp.uint32(HI_MASK), F32)


def _fold_sublanes(x, y, k, mask):
    return jnp.where(mask, x, pltpu.roll(y, k, 0)) + jnp.where(mask, pltpu.roll(x, SUBLANES - k, 0), y)


def _peer_u_kernel(e_ref, h_ref, tab_ref, a_ref, *, tb, npair, rows):
    sub = lax.broadcasted_iota(jnp.int32, (SUBLANES, LANES), 0)
    lane = lax.broadcasted_iota(jnp.int32, (SUBLANES, LANES), 1)
    m4 = sub < 4
    m2 = (sub % 4) < 2
    m1 = (sub % 2) == 0
    ng = npair // SUBLANES

    def body(t, carry):
        th = h_ref[pl.ds(t, 1), :].reshape(SUBLANES, LANES)
        out = jnp.zeros((SUBLANES, LANES), F32)
        for g in range(ng):
            ps = []
            for j in _BUTTERFLY_ORDER:
                idx = pl.multiple_of(e_ref[t, SUBLANES * g + j], rows)
                u_row = pltpu.bitcast(tab_ref[pl.ds(idx, rows), :], BF16).astype(F32)
                ps.append(u_row * th)
            v = [_fold_sublanes(ps[2 * k], ps[2 * k + 1], 4, m4) for k in range(4)]
            r = _fold_sublanes(_fold_sublanes(v[0], v[1], 2, m2), _fold_sublanes(v[2], v[3], 2, m2), 1, m1)
            out = jnp.where(lane == g, jnp.sum(r, axis=-1, keepdims=True), out)
        a_ref[t] = out[:, :ng]
        return carry

    lax.fori_loop(0, tb, body, 0, unroll=4)


def _peer_v_kernel(e_ref, w_ref, x_ref, tab_ref, o_ref, *, tb, npair, rows):
    def body(t, carry):
        lo = [jnp.zeros((rows, LANES), F32) for _ in range(2)]
        hi = [jnp.zeros((rows, LANES), F32) for _ in range(2)]
        for j in range(npair):
            idx = pl.multiple_of(e_ref[t, j], rows)
            w = w_ref[t, j]
            x_lo, x_hi = _unpack(tab_ref[pl.ds(idx, rows), :])
            lo[j % 2] = lo[j % 2] + w * x_lo
            hi[j % 2] = hi[j % 2] + w * x_hi
        delta = jnp.concatenate([lo[0] + lo[1], hi[0] + hi[1]], axis=0)
        x_t = x_ref[pl.ds(t, 1), :].reshape(2 * rows, LANES)
        o_ref[pl.ds(t, 1), :] = (x_t + delta).reshape(1, 2 * rows * LANES)
        return carry

    lax.fori_loop(0, tb, body, 0, unroll=2)


def _peer_w_kernel(a_ref, g_ref, w_ref):
    w_ref[...] = g_ref[...] * jax.nn.gelu(a_ref[...])


def _peer(x1, h2, e_pt, g_pt, u_pk, v_pk):
    t, dm = x1.shape
    npair = e_pt.shape[0]
    rows = dm // (2 * LANES)
    assert 2 * rows == SUBLANES and npair % SUBLANES == 0
    tb = _tile(t, 64)
    nblk = t // tb
    ng = npair // SUBLANES
    e_nat = jnp.transpose(e_pt)
    e_u = jnp.transpose(e_nat.reshape(t, SUBLANES, ng), (0, 2, 1)).reshape(nblk, tb, npair)
    e_v = e_nat.reshape(nblk, tb, npair)
    g_nat = jnp.transpose(g_pt)
    smem = functools.partial(pl.BlockSpec, memory_space=pltpu.SMEM)
    tab_spec = pl.BlockSpec(u_pk.shape, lambda i: (0, 0), pipeline_mode=pl.Buffered(1))
    tok_spec = pl.BlockSpec((tb, dm), lambda i: (i, 0))
    idx_spec = smem((None, tb, npair), lambda i: (i, 0, 0))
    a = pl.pallas_call(
        functools.partial(_peer_u_kernel, tb=tb, npair=npair, rows=rows),
        grid=(nblk,),
        in_specs=[idx_spec, tok_spec, tab_spec],
        out_specs=pl.BlockSpec((tb, SUBLANES, ng), lambda i: (i, 0, 0)),
        out_shape=jax.ShapeDtypeStruct((t, SUBLANES, ng), F32),
        compiler_params=_cparams(("arbitrary",), VMEM_LIMIT_TABLE),
        name="peer_u",
    )(e_u, h2, u_pk)
    tw = _tile(t, 512)
    w = pl.pallas_call(
        _peer_w_kernel,
        grid=(t // tw,),
        in_specs=[pl.BlockSpec((tw, npair), lambda i: (i, 0)), pl.BlockSpec((tw, npair), lambda i: (i, 0))],
        out_specs=pl.BlockSpec((tw, npair), lambda i: (i, 0)),
        out_shape=jax.ShapeDtypeStruct((t, npair), F32),
        compiler_params=_cparams(("parallel",)),
        name="peer_w",
    )(a.reshape(t, npair), g_nat)
    return pl.pallas_call(
        functools.partial(_peer_v_kernel, tb=tb, npair=npair, rows=rows),
        grid=(nblk,),
        in_specs=[idx_spec, idx_spec, tok_spec, tab_spec],
        out_specs=tok_spec,
        out_shape=jax.ShapeDtypeStruct((t, dm), F32),
        compiler_params=_cparams(("arbitrary",), VMEM_LIMIT_TABLE),
        name="peer_v",
    )(e_v, w.reshape(nblk, tb, npair), x1, v_pk)


def _layout(dm, bw, heads, nope, rope, c_lat):
    hq = heads * LANES
    lay = dict(bw=bw, heads=heads, nope=nope, rope=rope, c_lat=c_lat)
    lay["off_q"] = 4 * bw
    lay["off_g"] = lay["off_q"] + hq
    lay["off_ckv"] = lay["off_g"] + 3 * dm
    lay["off_kr"] = lay["off_ckv"] + c_lat
    lay["n"] = lay["off_kr"] + LANES
    assert nope + rope <= LANES and rope % 2 == 0
    assert lay["off_q"] % hq == 0 and lay["off_g"] % dm == 0 and lay["off_ckv"] % c_lat == 0
    assert bw % LANES == 0 and c_lat % LANES == 0
    return lay


def _pack_w_in(w_in, lay, dm):
    bw, heads, nope, rope, c_lat = lay["bw"], lay["heads"], lay["nope"], lay["rope"], lay["c_lat"]
    sizes = (bw, bw, bw, bw, heads * (nope + rope), c_lat, rope, 3 * dm)
    parts, off = [], 0
    for n in sizes:
        parts.append(w_in[:, off:off + n])
        off += n
    xc, bc, cc, us, q, ckv, kr, gates = parts
    q = jnp.pad(q.reshape(dm, heads, nope + rope), ((0, 0), (0, 0), (0, LANES - nope - rope))).reshape(dm, heads * LANES)
    kr = jnp.pad(kr, ((0, 0), (nope, LANES - nope - rope)))
    return jnp.concatenate([xc, bc, cc, us, q, gates, ckv, kr], axis=1).astype(BF16)


def _rope_tables(pos, nope, rope):
    half = rope // 2
    inv = ROPE_THETA ** (-jnp.arange(half, dtype=F32) / half)
    ang = pos.astype(F32)[:, None] * inv[None, :]
    cos, sin = jnp.cos(ang), jnp.sin(ang)
    n = pos.shape[0]
    ones = jnp.ones((n, nope), F32)
    tail = LANES - nope - rope
    cos_t = jnp.concatenate([ones, cos, cos, jnp.ones((n, tail), F32)], axis=1)
    sin_t = jnp.concatenate([jnp.zeros((n, nope), F32), -sin, sin, jnp.zeros((n, tail), F32)], axis=1)
    return cos_t, sin_t


def _lane_vec(parts):
    v = jnp.concatenate(parts)
    return jnp.pad(v, (0, LANES - v.shape[0])).reshape(1, LANES)


def _layer(x2d, bsz, seq_len, pos, lp, lay, prompt, conv_hist, h0_re, h0_im, sample_ctx):
    t, dm = x2d.shape
    bw, heads, nope, rope, c_lat = lay["bw"], lay["heads"], lay["nope"], lay["rope"], lay["c_lat"]
    vd = lp["w_uv"].shape[2]
    scale = (nope + rope) ** -0.5

    z = _proj(x2d, lp["g_mix"], lp["w_in_p"])

    cos_t, sin_t = _rope_tables(pos, nope, rope)
    gq = _lane_vec([lp["g_qn"], lp["g_qr"]])
    gkr = _lane_vec([jnp.zeros((nope,), F32), lp["g_kr"]])
    gkv = lp["g_kv"].reshape(1, c_lat)
    if prompt:
        oc, qn, ckv_n, kr_n, vt = _prep(z, lay, cos_t, sin_t, lp["conv_w"], gq, gkr, gkv, seq_len, True)
        nblk_seq = seq_len // (t // vt.shape[0])
        new_hist = vt.reshape(bsz, nblk_seq, SUBLANES, bw)[:, -1, SUBLANES - 2:, :]
    else:
        tm = _tile(t, 512)
        reps = tm // seq_len
        cos_t = jnp.tile(cos_t, (reps, 1))
        sin_t = jnp.tile(sin_t, (reps, 1))
        zeros = jnp.zeros((bsz, seq_len - 1, bw), F32)
        hist1 = jnp.concatenate([conv_hist[:, 1:2], zeros], axis=1).reshape(t, bw)
        hist2 = jnp.concatenate([conv_hist, zeros[:, 1:]], axis=1).reshape(t, bw)
        oc, qn, ckv_n, kr_n, vt = _prep(z, lay, cos_t, sin_t, lp["conv_w"], gq, gkr, gkv, seq_len, False, hist1, hist2,
                                        q_dtype=F32)
        new_hist = vt.reshape(bsz, seq_len, bw)[:, seq_len - 2:, :]

    lc = math.gcd(seq_len, S5_CHUNK)
    us = z[:, 3 * bw:4 * bw]
    ys, ht_re, ht_im = _s5(us, h0_re, h0_im, _s5_ops(lp["s5_mats"], lc), bsz, seq_len, lc)

    gkn = _lane_vec([lp["g_kn"]])
    if prompt:
        k_full, v_all = _kprep(ckv_n, kr_n, lp["w_uk_p"], lp["w_uv_f"], gkn, heads, nope)
        oa = _flash(qn, k_full, v_all, bsz, seq_len, heads, vd, scale)
    else:
        cache_ckv, cache_krope, page_table, layer = sample_ctx
        q4 = qn.reshape(bsz, seq_len, heads, LANES)
        eye = jnp.eye(heads, dtype=F32)
        qbd = (q4[..., :nope][:, :, :, None, :] * eye[None, None, :, :, None]).reshape(bsz, seq_len * heads, heads * nope)
        qr = q4[..., nope:nope + rope].reshape(bsz, seq_len * heads, rope)
        kn = -(-seq_len // 16) * 16
        c_new = jnp.pad(ckv_n.reshape(bsz, seq_len, c_lat), ((0, 0), (0, kn - seq_len), (0, 0)))
        r_new = jnp.pad(kr_n[:, nope:nope + rope].reshape(bsz, seq_len, rope), ((0, 0), (0, kn - seq_len), (0, 0)))
        e_mat = jnp.repeat(jnp.eye(heads, dtype=F32), nope, axis=1).astype(BF16)
        gk = jnp.tile(lp["g_kn"], heads).reshape(1, heads * nope)
        oa = _paged(page_table, qbd, qr, c_new, r_new, cache_ckv, cache_krope, layer, lp["w_uk_f"], lp["w_uv_f"], e_mat,
                    gk, heads, nope, vd, scale).reshape(t, heads * vd)

    x1, h2, qp = _merge(x2d, oc, ys, z, oa, lay, lp["ssm_d"].reshape(1, bw), lp["w_glu_b"], lp["b_glu"].reshape(1, bw),
                        lp["w_br_b"], lp["w_o_b"], lp["g_ffn"].reshape(1, dm), lp["w_pq_b"])
    e_pt, g_pt = _route(qp, lp["peer_keys_b"], dm // (2 * LANES))
    x2 = _peer(x1, h2, e_pt, g_pt, lp["peer_u_pk"], lp["peer_v_pk"])
    return x2, new_hist, ht_re, ht_im, ckv_n, kr_n[:, nope:nope + rope]


def kernel(x_prompt, x_sample, cache_conv, state_ssm_re, state_ssm_im, cache_ckv, cache_krope, page_table, g_mix, w_in, conv_w, ssm_lam_re, ssm_lam_im, ssm_log_dt, ssm_b_re, ssm_b_im, ssm_c_re, ssm_c_im, ssm_d, w_glu, b_glu, g_kv, w_uk, w_uv, g_qn, g_kn, g_qr, g_kr, w_br, w_o, g_ffn, w_pq, peer_keys, peer_u, peer_v):
    bp, lp_len, dm = x_prompt.shape
    bs, ls, _ = x_sample.shape
    depth = w_in.shape[0]
    bw = conv_w.shape[2]
    c_lat, heads, nope = w_uk.shape[1], w_uk.shape[2], w_uk.shape[3]
    vd = w_uv.shape[3]
    rope = cache_krope.shape[3]
    page = cache_ckv.shape[2]
    past = page_table.shape[1] * page
    n_groups, p_state = ssm_lam_re.shape[1], ssm_lam_re.shape[2]
    assert ssm_d.shape[1] == bw and lp_len % page == 0
    lay = _layout(dm, bw, heads, nope, rope, c_lat)

    pos_p = jnp.arange(lp_len, dtype=jnp.int32)
    pos_s = past + jnp.arange(ls, dtype=jnp.int32)
    xp = x_prompt.reshape(bp * lp_len, dm)
    xs = x_sample.reshape(bs * ls, dm)
    outs = {k: [] for k in ("p_conv", "p_re", "p_im", "p_ckv", "p_kr", "s_conv", "s_re", "s_im", "s_ckv", "s_kr")}
    for l in range(depth):
        lp = dict(
            g_mix=g_mix[l], conv_w=conv_w[l], ssm_lam_re=ssm_lam_re[l], ssm_lam_im=ssm_lam_im[l],
            ssm_log_dt=ssm_log_dt[l], ssm_b_re=ssm_b_re[l], ssm_b_im=ssm_b_im[l], ssm_c_re=ssm_c_re[l],
            ssm_c_im=ssm_c_im[l], ssm_d=ssm_d[l], b_glu=b_glu[l], g_kv=g_kv[l], g_qn=g_qn[l], g_kn=g_kn[l],
            g_qr=g_qr[l], g_kr=g_kr[l], g_ffn=g_ffn[l], w_uv=w_uv[l],
        )
        lp["s5_mats"] = _s5_mats(ssm_lam_re[l], ssm_lam_im[l], ssm_log_dt[l], ssm_b_re[l], ssm_b_im[l], ssm_c_re[l],
                                 ssm_c_im[l], S5_CHUNK)
        lp["peer_u_pk"] = _pack_table_rows(peer_u[l])
        lp["peer_v_pk"] = _pack_table(peer_v[l])
        lp["w_in_p"] = _pack_w_in(w_in[l], lay, dm)
        lp["w_uk_p"] = jnp.pad(w_uk[l], ((0, 0), (0, 0), (0, LANES - nope))).reshape(c_lat, heads * LANES).astype(BF16)
        lp["w_uk_f"] = w_uk[l].reshape(c_lat, heads * nope).astype(BF16)
        lp["w_uv_f"] = w_uv[l].reshape(c_lat, heads * vd).astype(BF16)
        lp["w_glu_b"] = w_glu[l].astype(BF16)
        lp["w_br_b"] = w_br[l].astype(BF16)
        lp["w_o_b"] = w_o[l].astype(BF16)
        lp["w_pq_b"] = w_pq[l].astype(BF16)
        lp["peer_keys_b"] = peer_keys[l].astype(BF16)

        zeros_state = jnp.zeros((bp, n_groups, p_state), F32)
        xp, hc, hr, hi, ck, kr = _layer(xp, bp, lp_len, pos_p, lp, lay, True, None, zeros_state, zeros_state, None)
        outs["p_conv"].append(hc)
        outs["p_re"].append(hr)
        outs["p_im"].append(hi)
        outs["p_ckv"].append(ck.reshape(bp, lp_len // page, page, c_lat))
        outs["p_kr"].append(kr.reshape(bp, lp_len // page, page, rope))
        xs, hc, hr, hi, ck, kr = _layer(xs, bs, ls, pos_s, lp, lay, False, cache_conv[l], state_ssm_re[l],
                                        state_ssm_im[l], (cache_ckv, cache_krope, page_table, l))
        outs["s_conv"].append(hc)
        outs["s_re"].append(hr)
        outs["s_im"].append(hi)
        outs["s_ckv"].append(ck.reshape(bs, ls, c_lat))
        outs["s_kr"].append(kr.reshape(bs, ls, rope))
    st = {k: jnp.stack(v) for k, v in outs.items()}
    return (xp.reshape(bp, lp_len, dm), xs.reshape(bs, ls, dm),
            st["p_conv"], st["p_re"], st["p_im"], st["p_ckv"], st["p_kr"],
            st["s_conv"], st["s_re"], st["s_im"], st["s_ckv"], st["s_kr"])
```

```python
import functools
import math

import jax
import jax.numpy as jnp
from jax import lax
from jax.experimental import pallas as pl
from jax.experimental.pallas import tpu as pltpu

EPS = 1e-6
ROPE_THETA = 10000.0
NEG_INF = -1e30
PEER_TOPK = 16
S5_CHUNK = 16

LANES = 128
SUBLANES = 8
VMEM_LIMIT = 48 * 1024 * 1024
VMEM_LIMIT_TABLE = 56 * 1024 * 1024

F32 = jnp.float32
BF16 = jnp.bfloat16


def _tile(n, pref):
    if n <= pref:
        return n
    t = pref - pref % SUBLANES
    while t >= SUBLANES:
        if n % t == 0:
            return t
        t -= SUBLANES
    return n


def _cparams(sem, limit=VMEM_LIMIT):
    return pltpu.CompilerParams(dimension_semantics=sem, vmem_limit_bytes=limit)


def _proj_kernel(x_ref, g_ref, w_ref, z_ref, h_ref):
    @pl.when(pl.program_id(1) == 0)
    def _():
        x = x_ref[...]
        ms = jnp.mean(x * x, axis=-1, keepdims=True)
        h_ref[...] = (x * lax.rsqrt(ms + EPS) * g_ref[...]).astype(BF16)

    z_ref[...] = jnp.dot(h_ref[...], w_ref[...], preferred_element_type=F32)


def _proj(x2d, g, w_p):
    t, d = x2d.shape
    n = w_p.shape[1]
    tm = _tile(t, 512)
    nb = n // LANES
    k = max(c for c in range(1, nb + 1) if nb % c == 0 and c * LANES <= 2304)
    tn = k * LANES
    return pl.pallas_call(
        _proj_kernel,
        grid=(t // tm, n // tn),
        in_specs=[
            pl.BlockSpec((tm, d), lambda i, j: (i, 0)),
            pl.BlockSpec((1, d), lambda i, j: (0, 0)),
            pl.BlockSpec((d, tn), lambda i, j: (0, j)),
        ],
        out_specs=pl.BlockSpec((tm, tn), lambda i, j: (i, j)),
        out_shape=jax.ShapeDtypeStruct((t, n), F32),
        scratch_shapes=[pltpu.VMEM((tm, d), BF16)],
        compiler_params=_cparams(("parallel", "arbitrary")),
        name="proj",
    )(x2d, g.reshape(1, d), w_p)


def _norm_rope(x, gain, cos, sin, nope, rope):
    lane = lax.broadcasted_iota(jnp.int32, (1, LANES), 1)
    m_n = lane < nope
    m_r = (lane >= nope) & (lane < nope + rope)
    sq = x * x
    ss_r = jnp.sum(jnp.where(m_r, sq, 0.0), axis=-1, keepdims=True)
    inv_r = lax.rsqrt(ss_r / rope + EPS)
    if nope:
        ss_n = jnp.sum(jnp.where(m_n, sq, 0.0), axis=-1, keepdims=True)
        inv = jnp.where(m_n, lax.rsqrt(ss_n / nope + EPS), inv_r)
    else:
        inv = inv_r
    y = x * inv * gain
    half = rope // 2
    first = lane < nope + half
    partner = jnp.where(first, pltpu.roll(y, LANES - half, 1), pltpu.roll(y, half, 1))
    return y * cos + partner * sin


def _prep_kernel(*refs, heads, nope, rope, seq_len, tm, prompt):
    if prompt:
        (xc_ref, bc_ref, cc_ref, q_ref, ckv_ref, kr_ref, cos_ref, sin_ref, cw_ref, gq_ref, gkr_ref, gkv_ref,
         hxc_ref, hcc_ref, oc_ref, qo_ref, ckvo_ref, kro_ref, vt_ref) = refs
    else:
        (xc_ref, bc_ref, cc_ref, q_ref, ckv_ref, kr_ref, cos_ref, sin_ref, cw_ref, gq_ref, gkr_ref, gkv_ref,
         h1_ref, h2_ref, oc_ref, qo_ref, ckvo_ref, kro_ref, vt_ref) = refs
    i = pl.program_id(0)
    v = cc_ref[...] * xc_ref[...]
    row = lax.broadcasted_iota(jnp.int32, (tm, 1), 0)
    r1 = pltpu.roll(v, 1, 0)
    r2 = pltpu.roll(v, 2, 0)
    if prompt:
        hv = hcc_ref[...] * hxc_ref[...]
        hv = jnp.where((i % (seq_len // tm)) == 0, 0.0, hv)
        v1 = jnp.where(row == 0, hv[7:8, :], r1)
        v2 = jnp.where(row == 0, hv[6:7, :], jnp.where(row == 1, hv[7:8, :], r2))
        vt_ref[0] = v[tm - SUBLANES:, :]
    else:
        l = row % seq_len
        v1 = jnp.where(l >= 1, r1, h1_ref[...])
        v2 = jnp.where(l >= 2, r2, h2_ref[...])
        vt_ref[...] = v
    y = v2 * cw_ref[0:1, :] + v1 * cw_ref[1:2, :] + v * cw_ref[2:3, :]
    oc_ref[...] = (bc_ref[...] * y).astype(oc_ref.dtype)

    cos = cos_ref[...]
    sin = sin_ref[...]
    gq = gq_ref[...]
    for h in range(heads):
        sl = slice(LANES * h, LANES * (h + 1))
        qo_ref[:, sl] = _norm_rope(q_ref[:, sl], gq, cos, sin, nope, rope).astype(qo_ref.dtype)
    kro_ref[...] = _kr_norm_rope(kr_ref[...], gkr_ref[...], cos, sin, nope, rope)
    c = ckv_ref[...]
    ms = jnp.mean(c * c, axis=-1, keepdims=True)
    ckvo_ref[...] = c * lax.rsqrt(ms + EPS) * gkv_ref[...]


def _kr_norm_rope(x, gain, cos, sin, nope, rope):
    lane = lax.broadcasted_iota(jnp.int32, (1, LANES), 1)
    ss = jnp.sum(x * x, axis=-1, keepdims=True)
    y = x * lax.rsqrt(ss / rope + EPS) * gain
    half = rope // 2
    first = lane < nope + half
    partner = jnp.where(first, pltpu.roll(y, LANES - half, 1), pltpu.roll(y, half, 1))
    return y * cos + partner * sin


def _prep(z, lay, cos_t, sin_t, conv_w, gq, gkr, gkv, seq_len, prompt, hist1=None, hist2=None, q_dtype=BF16):
    t = z.shape[0]
    bw, heads, c_lat = lay["bw"], lay["heads"], lay["c_lat"]
    hq = heads * LANES
    if prompt:
        tm = _tile(seq_len, 512)
        assert seq_len % tm == 0 and tm >= 2 * SUBLANES
    else:
        tm = _tile(t, 512)
        assert tm % seq_len == 0
    ntab = cos_t.shape[0] // tm
    in_specs = [
        pl.BlockSpec((tm, bw), lambda i: (i, 0)),
        pl.BlockSpec((tm, bw), lambda i: (i, 1)),
        pl.BlockSpec((tm, bw), lambda i: (i, 2)),
        pl.BlockSpec((tm, hq), lambda i: (i, lay["off_q"] // hq)),
        pl.BlockSpec((tm, c_lat), lambda i: (i, lay["off_ckv"] // c_lat)),
        pl.BlockSpec((tm, LANES), lambda i: (i, lay["off_kr"] // LANES)),
        pl.BlockSpec((tm, LANES), lambda i: (i % ntab, 0)),
        pl.BlockSpec((tm, LANES), lambda i: (i % ntab, 0)),
        pl.BlockSpec(conv_w.shape, lambda i: (0, 0)),
        pl.BlockSpec((1, LANES), lambda i: (0, 0)),
        pl.BlockSpec((1, LANES), lambda i: (0, 0)),
        pl.BlockSpec((1, c_lat), lambda i: (0, 0)),
    ]
    args = [z, z, z, z, z, z, cos_t, sin_t, conv_w, gq, gkr, gkv]
    if prompt:
        rb = tm // SUBLANES
        in_specs += [
            pl.BlockSpec((SUBLANES, bw), lambda i: (jnp.maximum(i * rb - 1, 0), 0)),
            pl.BlockSpec((SUBLANES, bw), lambda i: (jnp.maximum(i * rb - 1, 0), 2)),
        ]
        args += [z, z]
        vt_spec = pl.BlockSpec((1, SUBLANES, bw), lambda i: (i, 0, 0))
        vt_shape = jax.ShapeDtypeStruct((t // tm, SUBLANES, bw), F32)
    else:
        in_specs += [pl.BlockSpec((tm, bw), lambda i: (i, 0)), pl.BlockSpec((tm, bw), lambda i: (i, 0))]
        args += [hist1, hist2]
        vt_spec = pl.BlockSpec((tm, bw), lambda i: (i, 0))
        vt_shape = jax.ShapeDtypeStruct((t, bw), F32)
    kern = functools.partial(_prep_kernel, heads=heads, nope=lay["nope"], rope=lay["rope"], seq_len=seq_len, tm=tm,
                             prompt=prompt)
    return pl.pallas_call(
        kern,
        grid=(t // tm,),
        in_specs=in_specs,
        out_specs=[
            pl.BlockSpec((tm, bw), lambda i: (i, 0)),
            pl.BlockSpec((tm, hq), lambda i: (i, 0)),
            pl.BlockSpec((tm, c_lat), lambda i: (i, 0)),
            pl.BlockSpec((tm, LANES), lambda i: (i, 0)),
            vt_spec,
        ],
        out_shape=[
            jax.ShapeDtypeStruct((t, bw), BF16),
            jax.ShapeDtypeStruct((t, hq), q_dtype),
            jax.ShapeDtypeStruct((t, c_lat), F32),
            jax.ShapeDtypeStruct((t, LANES), F32),
            vt_shape,
        ],
        compiler_params=_cparams(("parallel",)),
        name="prep_prompt" if prompt else "prep_sample",
    )(*args)


def _s5_mats(lam_re, lam_im, log_dt, b_re, b_im, c_re, c_im, lc):
    hp = lax.Precision.HIGHEST
    g, p, n_in = b_re.shape
    n_out = c_re.shape[1]
    dt = jnp.exp(log_dt)[:, None]
    lr, li = lam_re, lam_im
    mag = jnp.exp(lr * dt)
    a_re, a_im = mag * jnp.cos(li * dt), mag * jnp.sin(li * dt)
    den = lr * lr + li * li
    f_re = ((a_re - 1.0) * lr + a_im * li) / den
    f_im = (a_im * lr - (a_re - 1.0) * li) / den
    bb_re = f_re[..., None] * b_re - f_im[..., None] * b_im
    bb_im = f_re[..., None] * b_im + f_im[..., None] * b_re
    k = jnp.arange(lc + 1, dtype=F32)[:, None, None]
    pm = jnp.exp(lr[None] * dt[None] * k)
    pr = pm * jnp.cos(li[None] * dt[None] * k)
    pi = pm * jnp.sin(li[None] * dt[None] * k)
    ab_re = pr[:lc, ..., None] * bb_re[None] - pi[:lc, ..., None] * bb_im[None]
    ab_im = pr[:lc, ..., None] * bb_im[None] + pi[:lc, ..., None] * bb_re[None]
    ms = jnp.concatenate([ab_re[::-1], ab_im[::-1]], axis=2)
    ms = jnp.transpose(ms, (1, 0, 3, 2)).reshape(g, lc * n_in, 2 * p)
    kk = (jnp.einsum("gop,dgpi->dgoi", c_re, ab_re, precision=hp)
          - jnp.einsum("gop,dgpi->dgoi", c_im, ab_im, precision=hp))
    s_idx = jnp.arange(lc)[:, None]
    t_idx = jnp.arange(lc)[None, :]
    delta = t_idx - s_idx
    kt = jnp.where((delta >= 0)[:, :, None, None, None], kk[jnp.clip(delta, 0, lc - 1)], 0.0)
    tk = jnp.transpose(kt, (2, 0, 4, 1, 3)).reshape(g, lc * n_in, lc * n_out)
    ca_re = c_re[None] * pr[1:, :, None, :] - c_im[None] * pi[1:, :, None, :]
    ca_im = c_re[None] * pi[1:, :, None, :] + c_im[None] * pr[1:, :, None, :]
    gs = jnp.concatenate([ca_re, -ca_im], axis=3)
    gs = jnp.transpose(gs, (1, 3, 0, 2)).reshape(g, 2 * p, lc * n_out)
    return dict(ms=ms.astype(BF16), tk=tk.astype(BF16), gs=gs.astype(BF16), pr=pr, pi=pi, lc=lc, n_in=n_in, n_out=n_out)


def _s5_ops(m, lc):
    full, n_in, n_out = m["lc"], m["n_in"], m["n_out"]
    assert lc <= full
    ms = m["ms"][:, (full - lc) * n_in:, :]
    tk = m["tk"][:, :lc * n_in, :lc * n_out]
    gs = m["gs"][:, :, :lc * n_out]
    a1 = jnp.concatenate([m["pr"][lc], m["pr"][lc]], axis=-1)[:, None, :]
    a2 = jnp.concatenate([-m["pi"][lc], m["pi"][lc]], axis=-1)[:, None, :]
    return ms, tk, gs, a1, a2


def _s5_kernel(u_ref, tk_ref, ms_ref, gs_ref, a1_ref, a2_ref, h0_ref, y_ref, ht_ref, s_sc, sw_sc, hin_sc, *, n_blocks,
               bsz, spb, p_state):
    u = u_ref[...]
    s = jnp.dot(u, ms_ref[...], preferred_element_type=F32)
    s_sc[...] = s
    sw_sc[...] = pltpu.roll(s, p_state, 1)
    a1 = a1_ref[...]
    a2 = a2_ref[...]
    rb = spb * bsz

    def body(k, carry):
        h, hw = carry
        r = pl.multiple_of(k * rb, SUBLANES)
        s_blk = s_sc[pl.ds(r, rb), :]
        sw_blk = sw_sc[pl.ds(r, rb), :]
        hs = []
        for q in range(spb):
            hs.append(h)
            rows = slice(q * bsz, (q + 1) * bsz)
            h, hw = a1 * h + a2 * hw + s_blk[rows, :], a1 * hw - a2 * h + sw_blk[rows, :]
        hin_sc[pl.ds(r, rb), :] = hs[0] if spb == 1 else jnp.concatenate(hs, axis=0)
        return h, hw

    h0 = h0_ref[...]
    h, _ = lax.fori_loop(0, n_blocks, body, (h0, pltpu.roll(h0, p_state, 1)))
    ht_ref[...] = h
    y_ref[...] = (jnp.dot(u, tk_ref[...], preferred_element_type=F32)
                  + jnp.dot(hin_sc[...].astype(BF16), gs_ref[...], preferred_element_type=F32))


def _s5(us, h0_re, h0_im, mats, bsz, seq_len, lc):
    ms, tk, gs, a1, a2 = mats
    g, _, p2 = ms.shape
    p_state = p2 // 2
    n_in = ms.shape[1] // lc
    n_out = tk.shape[2] // lc
    n_chunks = seq_len // lc
    spb = max(1, SUBLANES // bsz)
    assert (spb * bsz) % SUBLANES == 0 and n_chunks % spb == 0
    nc = n_chunks * bsz
    bp = bsz
    u = us.astype(BF16).reshape(bsz, n_chunks, lc, g, n_in)
    u = jnp.transpose(u, (3, 1, 0, 2, 4)).reshape(g, nc, lc * n_in)
    h0 = jnp.transpose(jnp.concatenate([h0_re, h0_im], axis=-1), (1, 0, 2))
    kern = functools.partial(_s5_kernel, n_blocks=n_chunks // spb, bsz=bsz, spb=spb, p_state=p_state)
    y, ht = pl.pallas_call(
        kern,
        grid=(g,),
        in_specs=[
            pl.BlockSpec((None, nc, lc * n_in), lambda i: (i, 0, 0)),
            pl.BlockSpec((None, lc * n_in, lc * n_out), lambda i: (i, 0, 0)),
            pl.BlockSpec((None, lc * n_in, p2), lambda i: (i, 0, 0)),
            pl.BlockSpec((None, p2, lc * n_out), lambda i: (i, 0, 0)),
            pl.BlockSpec((None, 1, p2), lambda i: (i, 0, 0)),
            pl.BlockSpec((None, 1, p2), lambda i: (i, 0, 0)),
            pl.BlockSpec((None, bp, p2), lambda i: (i, 0, 0)),
        ],
        out_specs=[
            pl.BlockSpec((None, nc, lc * n_out), lambda i: (i, 0, 0)),
            pl.BlockSpec((None, bp, p2), lambda i: (i, 0, 0)),
        ],
        out_shape=[
            jax.ShapeDtypeStruct((g, nc, lc * n_out), F32),
            jax.ShapeDtypeStruct((g, bp, p2), F32),
        ],
        scratch_shapes=[pltpu.VMEM((nc, p2), F32), pltpu.VMEM((nc, p2), F32), pltpu.VMEM((nc, p2), F32)],
        compiler_params=_cparams(("parallel",)),
        name="s5",
    )(u, tk, ms, gs, a1, a2, h0)
    y = y.reshape(g, n_chunks, bp, lc, n_out)[:, :, :bsz]
    y = jnp.transpose(y, (2, 1, 3, 0, 4)).reshape(bsz * seq_len, g * n_out)
    ht = jnp.transpose(ht[:, :bsz], (1, 0, 2))
    return y, ht[..., :p_state], ht[..., p_state:]


def _kprep_kernel(ckv_ref, kr_ref, wuk_ref, wuv_ref, gkn_ref, k_ref, v_ref, *, heads, nope):
    c = ckv_ref[...].astype(BF16)
    kraw = jnp.dot(c, wuk_ref[...], preferred_element_type=F32)
    kr = kr_ref[...]
    gkn = gkn_ref[...]
    for h in range(heads):
        sl = slice(LANES * h, LANES * (h + 1))
        kh = kraw[:, sl]
        ss = jnp.sum(kh * kh, axis=-1, keepdims=True)
        k_ref[:, sl] = (kh * lax.rsqrt(ss / nope + EPS) * gkn + kr).astype(BF16)
    v_ref[...] = jnp.dot(c, wuv_ref[...], preferred_element_type=F32).astype(BF16)


def _kprep(ckv_n, kr_n, wuk_p, wuv, gkn, heads, nope):
    t, c_lat = ckv_n.shape
    tm = _tile(t, 512)
    hk = wuk_p.shape[1]
    hv = wuv.shape[1]
    return pl.pallas_call(
        functools.partial(_kprep_kernel, heads=heads, nope=nope),
        grid=(t // tm,),
        in_specs=[
            pl.BlockSpec((tm, c_lat), lambda i: (i, 0)),
            pl.BlockSpec((tm, LANES), lambda i: (i, 0)),
            pl.BlockSpec((c_lat, hk), lambda i: (0, 0)),
            pl.BlockSpec((c_lat, hv), lambda i: (0, 0)),
            pl.BlockSpec((1, LANES), lambda i: (0, 0)),
        ],
        out_specs=[pl.BlockSpec((tm, hk), lambda i: (i, 0)), pl.BlockSpec((tm, hv), lambda i: (i, 0))],
        out_shape=[jax.ShapeDtypeStruct((t, hk), BF16), jax.ShapeDtypeStruct((t, hv), BF16)],
        compiler_params=_cparams(("parallel",)),
        name="kprep",
    )(ckv_n, kr_n, wuk_p, wuv, gkn)


def _flash_kernel(ii_ref, jj_ref, q_ref, k_ref, v_ref, o_ref, m_sc, l_sc, acc_sc, *, c_exp, tq, hps, vd):
    i = ii_ref[pl.program_id(2)]
    j = jj_ref[pl.program_id(2)]
    nct = tq // LANES

    @pl.when(j == 0)
    def _():
        m_sc[...] = jnp.full(m_sc.shape, NEG_INF, F32)
        l_sc[...] = jnp.zeros(l_sc.shape, F32)
        acc_sc[...] = jnp.zeros(acc_sc.shape, F32)

    def step(diagonal):
        if diagonal:
            row = lax.broadcasted_iota(jnp.int32, (tq, tq), 0)
            col = lax.broadcasted_iota(jnp.int32, (tq, tq), 1)
            keep = col <= row
        for hh in range(hps):
            q = q_ref[:, LANES * hh:LANES * (hh + 1)]
            k = k_ref[:, LANES * hh:LANES * (hh + 1)]
            s = lax.dot_general(q, k, (((1,), (1,)), ((), ())), preferred_element_type=F32)
            if diagonal:
                s = jnp.where(keep, s, NEG_INF)
            m_prev = m_sc[hh]
            m_new = jnp.maximum(m_prev, jnp.max(s, axis=-1, keepdims=True))
            alpha = jnp.exp2((m_prev - m_new) * c_exp)
            ps = [jnp.exp2((s[:, LANES * c:LANES * (c + 1)] - m_new) * c_exp) for c in range(nct)]
            psum = ps[0]
            for c in range(1, nct):
                psum = psum + ps[c]
            l_sc[hh] = alpha * l_sc[hh] + jnp.sum(psum, axis=-1, keepdims=True)
            p = jnp.concatenate(ps, axis=1).astype(BF16)
            acc_sc[hh] = alpha[:, :vd] * acc_sc[hh] + jnp.dot(p, v_ref[:, vd * hh:vd * (hh + 1)],
                                                              preferred_element_type=F32)
            m_sc[hh] = m_new

    @pl.when(j < i)
    def _():
        step(False)

    @pl.when(j == i)
    def _():
        step(True)
        for hh in range(hps):
            o_ref[:, vd * hh:vd * (hh + 1)] = (acc_sc[hh] / l_sc[hh][:, :vd]).astype(o_ref.dtype)


def _flash(q, k, v, bsz, seq_len, heads, vd, scale):
    hps = LANES // vd
    assert heads % hps == 0
    tq = _tile(seq_len, 1024)
    nq = seq_len // tq
    t = bsz * seq_len
    assert tq % LANES == 0
    kern = functools.partial(_flash_kernel, c_exp=scale * math.log2(math.e), tq=tq, hps=hps, vd=vd)
    pairs = [(i, j) for i in range(nq) for j in range(i + 1)]
    ii = jnp.asarray([p[0] for p in pairs], jnp.int32)
    jj = jnp.asarray([p[1] for p in pairs], jnp.int32)
    grid_spec = pltpu.PrefetchScalarGridSpec(
        num_scalar_prefetch=2,
        grid=(bsz, heads // hps, len(pairs)),
        in_specs=[
            pl.BlockSpec((tq, hps * LANES), lambda b, h, s, ii, jj: (b * nq + ii[s], h)),
            pl.BlockSpec((tq, hps * LANES), lambda b, h, s, ii, jj: (b * nq + jj[s], h)),
            pl.BlockSpec((tq, LANES), lambda b, h, s, ii, jj: (b * nq + jj[s], h)),
        ],
        out_specs=pl.BlockSpec((tq, LANES), lambda b, h, s, ii, jj: (b * nq + ii[s], h)),
        scratch_shapes=[pltpu.VMEM((hps, tq, LANES), F32), pltpu.VMEM((hps, tq, LANES), F32),
                        pltpu.VMEM((hps, tq, vd), F32)],
    )
    return pl.pallas_call(
        kern,
        grid_spec=grid_spec,
        out_shape=jax.ShapeDtypeStruct((t, heads * vd), BF16),
        compiler_params=_cparams(("parallel", "parallel", "arbitrary")),
        name="flash",
    )(ii, jj, q, k, v)


def _paged_kernel(pt_ref, qbd_ref, qr_ref, cn_ref, rn_ref, *refs, pp, heads, nope, vd, lq, scale):
    del pt_ref
    c_pages = refs[:pp]
    r_pages = refs[pp:2 * pp]
    wuk_ref, wuv_ref, e_ref, gk_ref, o_ref, m_sc, l_sc, acc_sc, cb_sc, rb_sc = refs[2 * pp:]
    s_id = pl.program_id(1)
    rows = lq * heads
    page = c_pages[0].shape[0]

    @pl.when(s_id == 0)
    def _():
        m_sc[...] = jnp.full(m_sc.shape, NEG_INF, F32)
        l_sc[...] = jnp.zeros(l_sc.shape, F32)
        acc_sc[...] = jnp.zeros(acc_sc.shape, F32)

    qbd = (qbd_ref[...] * gk_ref[...]).astype(BF16)
    qr = qr_ref[...].astype(BF16)
    dn = (((1,), (1,)), ((), ()))

    def scores(cb, rb):
        kraw = jnp.dot(cb, wuk_ref[...], preferred_element_type=F32)
        ssq = lax.dot_general(e_ref[...], (kraw * kraw).astype(BF16), dn, preferred_element_type=F32)
        inv = lax.rsqrt(ssq / nope + EPS)
        inv = jnp.concatenate([inv] * lq, axis=0)
        sn = lax.dot_general(qbd, kraw.astype(BF16), dn, preferred_element_type=F32)
        sr = jnp.dot(qr, rb, preferred_element_type=F32)
        return (sn * inv + sr) * scale

    def update(cb, rb, causal):
        s = scores(cb, rb)
        if causal:
            nk = cb.shape[0]
            kk = lax.broadcasted_iota(jnp.int32, (rows, nk), 1)
            qq = lax.broadcasted_iota(jnp.int32, (rows, nk), 0) // heads
            s = jnp.where(kk <= qq, s, NEG_INF)
        m_prev = m_sc[...]
        m_new = jnp.maximum(m_prev, jnp.max(s, axis=-1, keepdims=True))
        alpha = jnp.exp(m_prev - m_new)
        p = jnp.exp(s - m_new)
        l_sc[...] = alpha * l_sc[...] + jnp.sum(p, axis=-1, keepdims=True)
        acc_sc[...] = alpha * acc_sc[...] + jnp.dot(p.astype(BF16), cb, preferred_element_type=F32)
        m_sc[...] = m_new

    for pg in range(pp):
        cb_sc[page * pg:page * (pg + 1), :] = c_pages[pg][...].astype(BF16)
        rb_sc[:, page * pg:page * (pg + 1)] = r_pages[pg][...].astype(BF16)
    update(cb_sc[...], rb_sc[...], False)

    @pl.when(s_id == pl.num_programs(1) - 1)
    def _():
        update(cn_ref[...].astype(BF16), rn_ref[...].astype(BF16), True)
        lat = (acc_sc[...] / l_sc[...]).astype(BF16)
        full = jnp.dot(lat, wuv_ref[...], preferred_element_type=F32)
        colh = lax.broadcasted_iota(jnp.int32, full.shape, 1) // vd
        rowh = lax.broadcasted_iota(jnp.int32, full.shape, 0) % heads
        full = jnp.where(colh == rowh, full, 0.0)
        o_ref[...] = jnp.sum(full.reshape(lq, heads, heads * vd), axis=1)


def _paged(page_table, qbd, qr, c_new, r_new, cache_ckv, cache_krope, layer, wuk, wuv, e_mat, gk, heads, nope, vd,
           scale):
    bs, rows, _ = qbd.shape
    lq = rows // heads
    n_pages = page_table.shape[1]
    page, c_lat = cache_ckv.shape[2], cache_ckv.shape[3]
    rope = cache_krope.shape[3]
    pp = math.gcd(n_pages, 32)
    kn = c_new.shape[1]
    cache_krope = jnp.swapaxes(cache_krope, 2, 3)
    r_new = jnp.swapaxes(r_new, 1, 2)

    def cmap(p):
        return lambda b, s, pt: (layer, pt[b, s * pp + p], 0, 0)

    in_specs = [
        pl.BlockSpec((None, rows, heads * nope), lambda b, s, pt: (b, 0, 0)),
        pl.BlockSpec((None, rows, rope), lambda b, s, pt: (b, 0, 0)),
        pl.BlockSpec((None, kn, c_lat), lambda b, s, pt: (b, 0, 0)),
        pl.BlockSpec((None, rope, kn), lambda b, s, pt: (b, 0, 0)),
    ]
    in_specs += [pl.BlockSpec((None, None, page, c_lat), cmap(p)) for p in range(pp)]
    in_specs += [pl.BlockSpec((None, None, rope, page), cmap(p)) for p in range(pp)]
    in_specs += [
        pl.BlockSpec(wuk.shape, lambda b, s, pt: (0, 0)),
        pl.BlockSpec(wuv.shape, lambda b, s, pt: (0, 0)),
        pl.BlockSpec(e_mat.shape, lambda b, s, pt: (0, 0)),
        pl.BlockSpec(gk.shape, lambda b, s, pt: (0, 0)),
    ]
    kern = functools.partial(_paged_kernel, pp=pp, heads=heads, nope=nope, vd=vd, lq=lq, scale=scale)
    grid_spec = pltpu.PrefetchScalarGridSpec(
        num_scalar_prefetch=1,
        grid=(bs, n_pages // pp),
        in_specs=in_specs,
        out_specs=pl.BlockSpec((None, lq, heads * vd), lambda b, s, pt: (b, 0, 0)),
        scratch_shapes=[pltpu.VMEM((rows, 1), F32), pltpu.VMEM((rows, 1), F32), pltpu.VMEM((rows, c_lat), F32),
                        pltpu.VMEM((pp * page, c_lat), BF16), pltpu.VMEM((rope, pp * page), BF16)],
    )
    return pl.pallas_call(
        kern,
        grid_spec=grid_spec,
        out_shape=jax.ShapeDtypeStruct((bs, lq, heads * vd), F32),
        compiler_params=_cparams(("parallel", "arbitrary")),
        name="paged",
    )(page_table, qbd, qr, c_new, r_new, *([cache_ckv] * pp), *([cache_krope] * pp), wuk, wuv, e_mat, gk)


def _merge_kernel(x_ref, oc_ref, ys_ref, us_ref, oa_ref, g0_ref, g1_ref, g2_ref, d_ref, wglu_ref, bglu_ref, wbr_ref,
                  wo_ref, gffn_ref, wpq_ref, x1_ref, h2_ref, qp_ref):
    y = ys_ref[...] + d_ref[...] * us_ref[...]
    zg = jax.nn.gelu(y)
    gl = jnp.dot(zg.astype(BF16), wglu_ref[...], preferred_element_type=F32) + bglu_ref[...]
    o_ssm = zg * jax.nn.sigmoid(gl)
    merged = jax.nn.sigmoid(g0_ref[...]) * jnp.dot(oc_ref[...], wbr_ref[0], preferred_element_type=F32)
    merged += jax.nn.sigmoid(g1_ref[...]) * jnp.dot(o_ssm.astype(BF16), wbr_ref[1], preferred_element_type=F32)
    merged += jax.nn.sigmoid(g2_ref[...]) * jnp.dot(oa_ref[...].astype(BF16), wbr_ref[2], preferred_element_type=F32)
    x1 = x_ref[...] + jnp.dot(merged.astype(BF16), wo_ref[...], preferred_element_type=F32)
    x1_ref[...] = x1
    ms = jnp.mean(x1 * x1, axis=-1, keepdims=True)
    h2 = x1 * lax.rsqrt(ms + EPS) * gffn_ref[...]
    h2_ref[...] = h2
    qp_ref[...] = jnp.dot(h2.astype(BF16), wpq_ref[...], preferred_element_type=F32)


def _merge(x2d, oc, ys, z, oa, lay, d, wglu, bglu, wbr, wo, gffn, wpq):
    t, dm = x2d.shape
    bw = lay["bw"]
    dq = wpq.shape[1]
    tm = _tile(t, 256)
    gi = lay["off_g"] // dm

    def full(a):
        return pl.BlockSpec(a.shape, lambda i, _n=a.ndim: (0,) * _n)

    return pl.pallas_call(
        _merge_kernel,
        grid=(t // tm,),
        in_specs=[
            pl.BlockSpec((tm, dm), lambda i: (i, 0)),
            pl.BlockSpec((tm, bw), lambda i: (i, 0)),
            pl.BlockSpec((tm, bw), lambda i: (i, 0)),
            pl.BlockSpec((tm, bw), lambda i: (i, 3)),
            pl.BlockSpec((tm, bw), lambda i: (i, 0)),
            pl.BlockSpec((tm, dm), lambda i: (i, gi)),
            pl.BlockSpec((tm, dm), lambda i: (i, gi + 1)),
            pl.BlockSpec((tm, dm), lambda i: (i, gi + 2)),
            full(d), full(wglu), full(bglu), full(wbr), full(wo), full(gffn), full(wpq),
        ],
        out_specs=[
            pl.BlockSpec((tm, dm), lambda i: (i, 0)),
            pl.BlockSpec((tm, dm), lambda i: (i, 0)),
            pl.BlockSpec((tm, dq), lambda i: (i, 0)),
        ],
        out_shape=[
            jax.ShapeDtypeStruct((t, dm), F32),
            jax.ShapeDtypeStruct((t, dm), F32),
            jax.ShapeDtypeStruct((t, dq), F32),
        ],
        compiler_params=_cparams(("parallel",)),
        name="merge",
    )(x2d, oc, ys, z, oa, z, z, z, d, wglu, bglu, wbr, wo, gffn, wpq)


def _topk_rows(x, k):
    n = x.shape[0]
    iota = lax.broadcasted_iota(jnp.int32, x.shape, 0)
    vals, idxs = [], []
    for _ in range(k):
        m = jnp.max(x, axis=0, keepdims=True)
        am = jnp.min(jnp.where(x == m, iota, n), axis=0, keepdims=True)
        vals.append(m)
        idxs.append(am)
        x = jnp.where(iota == am, -jnp.inf, x)
    return jnp.concatenate(vals, axis=0), jnp.concatenate(idxs, axis=0)


def _select_rows(table, sel, k):
    out = jnp.zeros(sel.shape, table.dtype)
    for r in range(k):
        out = jnp.where(sel == r, table[r:r + 1, :], out)
    return out


def _pair_candidates(v1, v2, k):
    chunks, meta, r0, i = [], [], 0, 0
    while i < k and k // (i + 1) >= 2:
        n = k // (i + 1)
        nr = -(-n // SUBLANES) * SUBLANES
        blk = v1[i:i + 1, :] + v2[0:nr, :]
        if n < nr:
            blk = jnp.where(lax.broadcasted_iota(jnp.int32, blk.shape, 0) < n, blk, -jnp.inf)
        chunks.append(blk)
        meta.append((r0, nr, i, None))
        r0 += nr
        i += 1
    assert (k - i) % SUBLANES == 0
    chunks.append(v1[i:k, :] + v2[0:1, :])
    meta.append((r0, k - i, None, i))
    return jnp.concatenate(chunks, axis=0), meta


def _route_kernel(qp_ref, keys_ref, e_ref, g_ref, *, heads, n_keys, dk, topk, row_mult):
    dn = (((1,), (1,)), ((), ()))
    for h in range(heads):
        sub = []
        for s in range(2):
            o = (2 * h + s) * dk
            qs = qp_ref[:, o:o + dk].astype(BF16)
            st = lax.dot_general(keys_ref[h, s], qs, dn, preferred_element_type=F32)
            sub.append(_topk_rows(st, topk))
        (v1, i1), (v2, i2) = sub
        cand, meta = _pair_candidates(v1, v2, topk)
        sc, ci = _topk_rows(cand, topk)
        ihi = jnp.zeros(ci.shape, jnp.int32)
        jlo = jnp.zeros(ci.shape, jnp.int32)
        for r0, nr, ic, i0 in meta:
            inr = (ci >= r0) & (ci < r0 + nr)
            if ic is None:
                ihi = jnp.where(inr, ci - r0 + i0, ihi)
            else:
                ihi = jnp.where(inr, ic, ihi)
                jlo = jnp.where(inr, ci - r0, jlo)
        e1 = _select_rows(i1, ihi, topk)
        e2 = _select_rows(i2, jlo, topk)
        ex = jnp.exp(sc - sc[0:1, :])
        e_ref[topk * h:topk * (h + 1), :] = (e1 * n_keys + e2) * row_mult
        g_ref[topk * h:topk * (h + 1), :] = ex / jnp.sum(ex, axis=0, keepdims=True)


def _route(qp, keys_bf16, row_mult):
    t, dq = qp.shape
    heads, _, n_keys, dk = keys_bf16.shape
    tm = _tile(t, 128)
    if t % LANES:
        tm = t
    npair = heads * PEER_TOPK
    kern = functools.partial(_route_kernel, heads=heads, n_keys=n_keys, dk=dk, topk=PEER_TOPK, row_mult=row_mult)
    return pl.pallas_call(
        kern,
        grid=(t // tm,),
        in_specs=[
            pl.BlockSpec((tm, dq), lambda i: (i, 0)),
            pl.BlockSpec(keys_bf16.shape, lambda i: (0, 0, 0, 0)),
        ],
        out_specs=[pl.BlockSpec((npair, tm), lambda i: (0, i)), pl.BlockSpec((npair, tm), lambda i: (0, i))],
        out_shape=[jax.ShapeDtypeStruct((npair, t), jnp.int32), jax.ShapeDtypeStruct((npair, t), F32)],
        compiler_params=_cparams(("parallel",)),
        name="route",
    )(qp, keys_bf16)


HI_MASK = 0xFFFF0000
_BUTTERFLY_ORDER = (0, 4, 2, 6, 1, 5, 3, 7)


def _pack_table(tab):
    e, d = tab.shape
    half = d // 2
    b = lax.bitcast_convert_type(tab.astype(BF16), jnp.uint16).astype(jnp.uint32)
    return (b[:, :half] | (b[:, half:] << 16)).reshape(e * (half // LANES), LANES)


def _pack_table_rows(tab):
    e, d = tab.shape
    b = lax.bitcast_convert_type(tab.astype(BF16), jnp.uint16).astype(jnp.uint32).reshape(e, d // (2 * LANES), 2, LANES)
    return (b[:, :, 0, :] | (b[:, :, 1, :] << 16)).reshape(e * (d // (2 * LANES)), LANES)


def _unpack(x):
    return pltpu.bitcast(x << 16, F32), pltpu.bitcast(x & jnp.uint32(HI_MASK), F32)


def _fold_sublanes(x, y, k, mask):
    return jnp.where(mask, x, pltpu.roll(y, k, 0)) + jnp.where(mask, pltpu.roll(x, SUBLANES - k, 0), y)


def _peer_u_kernel(e_ref, h_ref, tab_ref, a_ref, *, tb, npair, rows):
    sub = lax.broadcasted_iota(jnp.int32, (SUBLANES, LANES), 0)
    lane = lax.broadcasted_iota(jnp.int32, (SUBLANES, LANES), 1)
    m4 = sub < 4
    m2 = (sub % 4) < 2
    m1 = (sub % 2) == 0
    ng = npair // SUBLANES

    def body(t, carry):
        th = h_ref[pl.ds(t, 1), :].reshape(SUBLANES, LANES)
        out = jnp.zeros((SUBLANES, LANES), F32)
        for g in range(ng):
            ps = []
            for j in _BUTTERFLY_ORDER:
                idx = pl.multiple_of(e_ref[t, SUBLANES * g + j], rows)
                u_row = pltpu.bitcast(tab_ref[pl.ds(idx, rows), :], BF16).astype(F32)
                ps.append(u_row * th)
            v = [_fold_sublanes(ps[2 * k], ps[2 * k + 1], 4, m4) for k in range(4)]
            r = _fold_sublanes(_fold_sublanes(v[0], v[1], 2, m2), _fold_sublanes(v[2], v[3], 2, m2), 1, m1)
            out = jnp.where(lane == g, jnp.sum(r, axis=-1, keepdims=True), out)
        a_ref[t] = out[:, :ng]
        return carry

    lax.fori_loop(0, tb, body, 0, unroll=8)


def _peer_v_kernel(e_ref, w_ref, x_ref, tab_ref, o_ref, *, tb, npair, rows):
    def body(t, carry):
        lo = [jnp.zeros((rows, LANES), F32) for _ in range(2)]
        hi = [jnp.zeros((rows, LANES), F32) for _ in range(2)]
        for j in range(npair):
            idx = pl.multiple_of(e_ref[t, j], rows)
            w = w_ref[t, j]
            x_lo, x_hi = _unpack(tab_ref[pl.ds(idx, rows), :])
            lo[j % 2] = lo[j % 2] + w * x_lo
            hi[j % 2] = hi[j % 2] + w * x_hi
        delta = jnp.concatenate([lo[0] + lo[1], hi[0] + hi[1]], axis=0)
        x_t = x_ref[pl.ds(t, 1), :].reshape(2 * rows, LANES)
        o_ref[pl.ds(t, 1), :] = (x_t + delta).reshape(1, 2 * rows * LANES)
        return carry

    lax.fori_loop(0, tb, body, 0, unroll=2)


def _peer_w_kernel(a_ref, g_ref, w_ref):
    w_ref[...] = g_ref[...] * jax.nn.gelu(a_ref[...])


def _peer(x1, h2, e_pt, g_pt, u_pk, v_pk):
    t, dm = x1.shape
    npair = e_pt.shape[0]
    rows = dm // (2 * LANES)
    assert 2 * rows == SUBLANES and npair % SUBLANES == 0
    tb = _tile(t, 64)
    nblk = t // tb
    ng = npair // SUBLANES
    e_nat = jnp.transpose(e_pt)
    e_u = jnp.transpose(e_nat.reshape(t, SUBLANES, ng), (0, 2, 1)).reshape(nblk, tb, npair)
    e_v = e_nat.reshape(nblk, tb, npair)
    g_nat = jnp.transpose(g_pt)
    smem = functools.partial(pl.BlockSpec, memory_space=pltpu.SMEM)
    tab_spec = pl.BlockSpec(u_pk.shape, lambda i: (0, 0), pipeline_mode=pl.Buffered(1))
    tok_spec = pl.BlockSpec((tb, dm), lambda i: (i, 0))
    idx_spec = smem((None, tb, npair), lambda i: (i, 0, 0))
    a = pl.pallas_call(
        functools.partial(_peer_u_kernel, tb=tb, npair=npair, rows=rows),
        grid=(nblk,),
        in_specs=[idx_spec, tok_spec, tab_spec],
        out_specs=pl.BlockSpec((tb, SUBLANES, ng), lambda i: (i, 0, 0)),
        out_shape=jax.ShapeDtypeStruct((t, SUBLANES, ng), F32),
        compiler_params=_cparams(("arbitrary",), VMEM_LIMIT_TABLE),
        name="peer_u",
    )(e_u, h2, u_pk)
    tw = _tile(t, 512)
    w = pl.pallas_call(
        _peer_w_kernel,
        grid=(t // tw,),
        in_specs=[pl.BlockSpec((tw, npair), lambda i: (i, 0)), pl.BlockSpec((tw, npair), lambda i: (i, 0))],
        out_specs=pl.BlockSpec((tw, npair), lambda i: (i, 0)),
        out_shape=jax.ShapeDtypeStruct((t, npair), F32),
        compiler_params=_cparams(("parallel",)),
        name="peer_w",
    )(a.reshape(t, npair), g_nat)
    return pl.pallas_call(
        functools.partial(_peer_v_kernel, tb=tb, npair=npair, rows=rows),
        grid=(nblk,),
        in_specs=[idx_spec, idx_spec, tok_spec, tab_spec],
        out_specs=tok_spec,
        out_shape=jax.ShapeDtypeStruct((t, dm), F32),
        compiler_params=_cparams(("arbitrary",), VMEM_LIMIT_TABLE),
        name="peer_v",
    )(e_v, w.reshape(nblk, tb, npair), x1, v_pk)


def _layout(dm, bw, heads, nope, rope, c_lat):
    hq = heads * LANES
    lay = dict(bw=bw, heads=heads, nope=nope, rope=rope, c_lat=c_lat)
    lay["off_q"] = 4 * bw
    lay["off_g"] = lay["off_q"] + hq
    lay["off_ckv"] = lay["off_g"] + 3 * dm
    lay["off_kr"] = lay["off_ckv"] + c_lat
    lay["n"] = lay["off_kr"] + LANES
    assert nope + rope <= LANES and rope % 2 == 0
    assert lay["off_q"] % hq == 0 and lay["off_g"] % dm == 0 and lay["off_ckv"] % c_lat == 0
    assert bw % LANES == 0 and c_lat % LANES == 0
    return lay


def _pack_w_in(w_in, lay, dm):
    bw, heads, nope, rope, c_lat = lay["bw"], lay["heads"], lay["nope"], lay["rope"], lay["c_lat"]
    sizes = (bw, bw, bw, bw, heads * (nope + rope), c_lat, rope, 3 * dm)
    parts, off = [], 0
    for n in sizes:
        parts.append(w_in[:, off:off + n])
        off += n
    xc, bc, cc, us, q, ckv, kr, gates = parts
    q = jnp.pad(q.reshape(dm, heads, nope + rope), ((0, 0), (0, 0), (0, LANES - nope - rope))).reshape(dm, heads * LANES)
    kr = jnp.pad(kr, ((0, 0), (nope, LANES - nope - rope)))
    return jnp.concatenate([xc, bc, cc, us, q, gates, ckv, kr], axis=1).astype(BF16)


def _rope_tables(pos, nope, rope):
    half = rope // 2
    inv = ROPE_THETA ** (-jnp.arange(half, dtype=F32) / half)
    ang = pos.astype(F32)[:, None] * inv[None, :]
    cos, sin = jnp.cos(ang), jnp.sin(ang)
    n = pos.shape[0]
    ones = jnp.ones((n, nope), F32)
    tail = LANES - nope - rope
    cos_t = jnp.concatenate([ones, cos, cos, jnp.ones((n, tail), F32)], axis=1)
    sin_t = jnp.concatenate([jnp.zeros((n, nope), F32), -sin, sin, jnp.zeros((n, tail), F32)], axis=1)
    return cos_t, sin_t


def _lane_vec(parts):
    v = jnp.concatenate(parts)
    return jnp.pad(v, (0, LANES - v.shape[0])).reshape(1, LANES)


def _layer(x2d, bsz, seq_len, pos, lp, lay, prompt, conv_hist, h0_re, h0_im, sample_ctx):
    t, dm = x2d.shape
    bw, heads, nope, rope, c_lat = lay["bw"], lay["heads"], lay["nope"], lay["rope"], lay["c_lat"]
    vd = lp["w_uv"].shape[2]
    scale = (nope + rope) ** -0.5

    z = _proj(x2d, lp["g_mix"], lp["w_in_p"])

    cos_t, sin_t = _rope_tables(pos, nope, rope)
    gq = _lane_vec([lp["g_qn"], lp["g_qr"]])
    gkr = _lane_vec([jnp.zeros((nope,), F32), lp["g_kr"]])
    gkv = lp["g_kv"].reshape(1, c_lat)
    if prompt:
        oc, qn, ckv_n, kr_n, vt = _prep(z, lay, cos_t, sin_t, lp["conv_w"], gq, gkr, gkv, seq_len, True)
        nblk_seq = seq_len // (t // vt.shape[0])
        new_hist = vt.reshape(bsz, nblk_seq, SUBLANES, bw)[:, -1, SUBLANES - 2:, :]
    else:
        tm = _tile(t, 512)
        reps = tm // seq_len
        cos_t = jnp.tile(cos_t, (reps, 1))
        sin_t = jnp.tile(sin_t, (reps, 1))
        zeros = jnp.zeros((bsz, seq_len - 1, bw), F32)
        hist1 = jnp.concatenate([conv_hist[:, 1:2], zeros], axis=1).reshape(t, bw)
        hist2 = jnp.concatenate([conv_hist, zeros[:, 1:]], axis=1).reshape(t, bw)
        oc, qn, ckv_n, kr_n, vt = _prep(z, lay, cos_t, sin_t, lp["conv_w"], gq, gkr, gkv, seq_len, False, hist1, hist2,
                                        q_dtype=F32)
        new_hist = vt.reshape(bsz, seq_len, bw)[:, seq_len - 2:, :]

    lc = math.gcd(seq_len, S5_CHUNK)
    us = z[:, 3 * bw:4 * bw]
    ys, ht_re, ht_im = _s5(us, h0_re, h0_im, _s5_ops(lp["s5_mats"], lc), bsz, seq_len, lc)

    gkn = _lane_vec([lp["g_kn"]])
    if prompt:
        k_full, v_all = _kprep(ckv_n, kr_n, lp["w_uk_p"], lp["w_uv_f"], gkn, heads, nope)
        oa = _flash(qn, k_full, v_all, bsz, seq_len, heads, vd, scale)
    else:
        cache_ckv, cache_krope, page_table, layer = sample_ctx
        q4 = qn.reshape(bsz, seq_len, heads, LANES)
        eye = jnp.eye(heads, dtype=F32)
        qbd = (q4[..., :nope][:, :, :, None, :] * eye[None, None, :, :, None]).reshape(bsz, seq_len * heads, heads * nope)
        qr = q4[..., nope:nope + rope].reshape(bsz, seq_len * heads, rope)
        kn = -(-seq_len // 16) * 16
        c_new = jnp.pad(ckv_n.reshape(bsz, seq_len, c_lat), ((0, 0), (0, kn - seq_len), (0, 0)))
        r_new = jnp.pad(kr_n[:, nope:nope + rope].reshape(bsz, seq_len, rope), ((0, 0), (0, kn - seq_len), (0, 0)))
        e_mat = jnp.repeat(jnp.eye(heads, dtype=F32), nope, axis=1).astype(BF16)
        gk = jnp.tile(lp["g_kn"], heads).reshape(1, heads * nope)
        oa = _paged(page_table, qbd, qr, c_new, r_new, cache_ckv, cache_krope, layer, lp["w_uk_f"], lp["w_uv_f"], e_mat,
                    gk, heads, nope, vd, scale).reshape(t, heads * vd)

    x1, h2, qp = _merge(x2d, oc, ys, z, oa, lay, lp["ssm_d"].reshape(1, bw), lp["w_glu_b"], lp["b_glu"].reshape(1, bw),
                        lp["w_br_b"], lp["w_o_b"], lp["g_ffn"].reshape(1, dm), lp["w_pq_b"])
    e_pt, g_pt = _route(qp, lp["peer_keys_b"], dm // (2 * LANES))
    x2 = _peer(x1, h2, e_pt, g_pt, lp["peer_u_pk"], lp["peer_v_pk"])
    return x2, new_hist, ht_re, ht_im, ckv_n, kr_n[:, nope:nope + rope]


def kernel(x_prompt, x_sample, cache_conv, state_ssm_re, state_ssm_im, cache_ckv, cache_krope, page_table, g_mix, w_in, conv_w, ssm_lam_re, ssm_lam_im, ssm_log_dt, ssm_b_re, ssm_b_im, ssm_c_re, ssm_c_im, ssm_d, w_glu, b_glu, g_kv, w_uk, w_uv, g_qn, g_kn, g_qr, g_kr, w_br, w_o, g_ffn, w_pq, peer_keys, peer_u, peer_v):
    bp, lp_len, dm = x_prompt.shape
    bs, ls, _ = x_sample.shape
    depth = w_in.shape[0]
    bw = conv_w.shape[2]
    c_lat, heads, nope = w_uk.shape[1], w_uk.shape[2], w_uk.shape[3]
    vd = w_uv.shape[3]
    rope = cache_krope.shape[3]
    page = cache_ckv.shape[2]
    past = page_table.shape[1] * page
    n_groups, p_state = ssm_lam_re.shape[1], ssm_lam_re.shape[2]
    assert ssm_d.shape[1] == bw and lp_len % page == 0
    lay = _layout(dm, bw, heads, nope, rope, c_lat)

    pos_p = jnp.arange(lp_len, dtype=jnp.int32)
    pos_s = past + jnp.arange(ls, dtype=jnp.int32)
    xp = x_prompt.reshape(bp * lp_len, dm)
    xs = x_sample.reshape(bs * ls, dm)
    outs = {k: [] for k in ("p_conv", "p_re", "p_im", "p_ckv", "p_kr", "s_conv", "s_re", "s_im", "s_ckv", "s_kr")}
    for l in range(depth):
        lp = dict(
            g_mix=g_mix[l], conv_w=conv_w[l], ssm_lam_re=ssm_lam_re[l], ssm_lam_im=ssm_lam_im[l],
            ssm_log_dt=ssm_log_dt[l], ssm_b_re=ssm_b_re[l], ssm_b_im=ssm_b_im[l], ssm_c_re=ssm_c_re[l],
            ssm_c_im=ssm_c_im[l], ssm_d=ssm_d[l], b_glu=b_glu[l], g_kv=g_kv[l], g_qn=g_qn[l], g_kn=g_kn[l],
            g_qr=g_qr[l], g_kr=g_kr[l], g_ffn=g_ffn[l], w_uv=w_uv[l],
        )
        lp["s5_mats"] = _s5_mats(ssm_lam_re[l], ssm_lam_im[l], ssm_log_dt[l], ssm_b_re[l], ssm_b_im[l], ssm_c_re[l],
                                 ssm_c_im[l], S5_CHUNK)
        lp["peer_u_pk"] = _pack_table_rows(peer_u[l])
        lp["peer_v_pk"] = _pack_table(peer_v[l])
        lp["w_in_p"] = _pack_w_in(w_in[l], lay, dm)
        lp["w_uk_p"] = jnp.pad(w_uk[l], ((0, 0), (0, 0), (0, LANES - nope))).reshape(c_lat, heads * LANES).astype(BF16)
        lp["w_uk_f"] = w_uk[l].reshape(c_lat, heads * nope).astype(BF16)
        lp["w_uv_f"] = w_uv[l].reshape(c_lat, heads * vd).astype(BF16)
        lp["w_glu_b"] = w_glu[l].astype(BF16)
        lp["w_br_b"] = w_br[l].astype(BF16)
        lp["w_o_b"] = w_o[l].astype(BF16)
        lp["w_pq_b"] = w_pq[l].astype(BF16)
        lp["peer_keys_b"] = peer_keys[l].astype(BF16)

        zeros_state = jnp.zeros((bp, n_groups, p_state), F32)
        xp, hc, hr, hi, ck, kr = _layer(xp, bp, lp_len, pos_p, lp, lay, True, None, zeros_state, zeros_state, None)
        outs["p_conv"].append(hc)
        outs["p_re"].append(hr)
        outs["p_im"].append(hi)
        outs["p_ckv"].append(ck.reshape(bp, lp_len // page, page, c_lat))
        outs["p_kr"].append(kr.reshape(bp, lp_len // page, page, rope))
        xs, hc, hr, hi, ck, kr = _layer(xs, bs, ls, pos_s, lp, lay, False, cache_conv[l], state_ssm_re[l],
                                        state_ssm_im[l], (cache_ckv, cache_krope, page_table, l))
        outs["s_conv"].append(hc)
        outs["s_re"].append(hr)
        outs["s_im"].append(hi)
        outs["s_ckv"].append(ck.reshape(bs, ls, c_lat))
        outs["s_kr"].append(kr.reshape(bs, ls, rope))
    st = {k: jnp.stack(v) for k, v in outs.items()}
    return (xp.reshape(bp, lp_len, dm), xs.reshape(bs, ls, dm),
            st["p_conv"], st["p_re"], st["p_im"], st["p_ckv"], st["p_kr"],
            st["s_conv"], st["s_re"], st["s_im"], st["s_ckv"], st["s_kr"])
```

```python
import functools
import math

import jax
import jax.numpy as jnp
from jax import lax
from jax.experimental import pallas as pl
from jax.experimental.pallas import tpu as pltpu

EPS = 1e-6
ROPE_THETA = 10000.0
NEG_INF = -1e30
PEER_TOPK = 16
S5_CHUNK = 16

LANES = 128
SUBLANES = 8
VMEM_LIMIT = 48 * 1024 * 1024
VMEM_LIMIT_TABLE = 56 * 1024 * 1024

F32 = jnp.float32
BF16 = jnp.bfloat16


def _tile(n, pref):
    if n <= pref:
        return n
    t = pref - pref % SUBLANES
    while t >= SUBLANES:
        if n % t == 0:
            return t
        t -= SUBLANES
    return n


def _cparams(sem, limit=VMEM_LIMIT):
    return pltpu.CompilerParams(dimension_semantics=sem, vmem_limit_bytes=limit)


def _proj_kernel(x_ref, g_ref, w_ref, z_ref, h_ref):
    @pl.when(pl.program_id(1) == 0)
    def _():
        x = x_ref[...]
        ms = jnp.mean(x * x, axis=-1, keepdims=True)
        h_ref[...] = (x * lax.rsqrt(ms + EPS) * g_ref[...]).astype(BF16)

    z_ref[...] = jnp.dot(h_ref[...], w_ref[...], preferred_element_type=F32)


def _proj(x2d, g, w_p):
    t, d = x2d.shape
    n = w_p.shape[1]
    tm = _tile(t, 512)
    nb = n // LANES
    k = max(c for c in range(1, nb + 1) if nb % c == 0 and c * LANES <= 2304)
    tn = k * LANES
    return pl.pallas_call(
        _proj_kernel,
        grid=(t // tm, n // tn),
        in_specs=[
            pl.BlockSpec((tm, d), lambda i, j: (i, 0)),
            pl.BlockSpec((1, d), lambda i, j: (0, 0)),
            pl.BlockSpec((d, tn), lambda i, j: (0, j)),
        ],
        out_specs=pl.BlockSpec((tm, tn), lambda i, j: (i, j)),
        out_shape=jax.ShapeDtypeStruct((t, n), F32),
        scratch_shapes=[pltpu.VMEM((tm, d), BF16)],
        compiler_params=_cparams(("parallel", "arbitrary")),
        name="proj",
    )(x2d, g.reshape(1, d), w_p)


def _norm_rope(x, gain, cos, sin, nope, rope):
    lane = lax.broadcasted_iota(jnp.int32, (1, LANES), 1)
    m_n = lane < nope
    m_r = (lane >= nope) & (lane < nope + rope)
    sq = x * x
    ss_r = jnp.sum(jnp.where(m_r, sq, 0.0), axis=-1, keepdims=True)
    inv_r = lax.rsqrt(ss_r / rope + EPS)
    if nope:
        ss_n = jnp.sum(jnp.where(m_n, sq, 0.0), axis=-1, keepdims=True)
        inv = jnp.where(m_n, lax.rsqrt(ss_n / nope + EPS), inv_r)
    else:
        inv = inv_r
    y = x * inv * gain
    half = rope // 2
    first = lane < nope + half
    partner = jnp.where(first, pltpu.roll(y, LANES - half, 1), pltpu.roll(y, half, 1))
    return y * cos + partner * sin


def _prep_kernel(*refs, heads, nope, rope, seq_len, tm, prompt):
    if prompt:
        (xc_ref, bc_ref, cc_ref, q_ref, ckv_ref, kr_ref, cos_ref, sin_ref, cw_ref, gq_ref, gkr_ref, gkv_ref,
         hxc_ref, hcc_ref, oc_ref, qo_ref, ckvo_ref, kro_ref, vt_ref) = refs
    else:
        (xc_ref, bc_ref, cc_ref, q_ref, ckv_ref, kr_ref, cos_ref, sin_ref, cw_ref, gq_ref, gkr_ref, gkv_ref,
         h1_ref, h2_ref, oc_ref, qo_ref, ckvo_ref, kro_ref, vt_ref) = refs
    i = pl.program_id(0)
    v = cc_ref[...] * xc_ref[...]
    row = lax.broadcasted_iota(jnp.int32, (tm, 1), 0)
    r1 = pltpu.roll(v, 1, 0)
    r2 = pltpu.roll(v, 2, 0)
    if prompt:
        hv = hcc_ref[...] * hxc_ref[...]
        hv = jnp.where((i % (seq_len // tm)) == 0, 0.0, hv)
        v1 = jnp.where(row == 0, hv[7:8, :], r1)
        v2 = jnp.where(row == 0, hv[6:7, :], jnp.where(row == 1, hv[7:8, :], r2))
        vt_ref[0] = v[tm - SUBLANES:, :]
    else:
        l = row % seq_len
        v1 = jnp.where(l >= 1, r1, h1_ref[...])
        v2 = jnp.where(l >= 2, r2, h2_ref[...])
        vt_ref[...] = v
    y = v2 * cw_ref[0:1, :] + v1 * cw_ref[1:2, :] + v * cw_ref[2:3, :]
    oc_ref[...] = (bc_ref[...] * y).astype(oc_ref.dtype)

    cos = cos_ref[...]
    sin = sin_ref[...]
    gq = gq_ref[...]
    for h in range(heads):
        sl = slice(LANES * h, LANES * (h + 1))
        qo_ref[:, sl] = _norm_rope(q_ref[:, sl], gq, cos, sin, nope, rope).astype(qo_ref.dtype)
    kro_ref[...] = _kr_norm_rope(kr_ref[...], gkr_ref[...], cos, sin, nope, rope)
    c = ckv_ref[...]
    ms = jnp.mean(c * c, axis=-1, keepdims=True)
    ckvo_ref[...] = c * lax.rsqrt(ms + EPS) * gkv_ref[...]


def _kr_norm_rope(x, gain, cos, sin, nope, rope):
    lane = lax.broadcasted_iota(jnp.int32, (1, LANES), 1)
    ss = jnp.sum(x * x, axis=-1, keepdims=True)
    y = x * lax.rsqrt(ss / rope + EPS) * gain
    half = rope // 2
    first = lane < nope + half
    partner = jnp.where(first, pltpu.roll(y, LANES - half, 1), pltpu.roll(y, half, 1))
    return y * cos + partner * sin


def _prep(z, lay, cos_t, sin_t, conv_w, gq, gkr, gkv, seq_len, prompt, hist1=None, hist2=None, q_dtype=BF16):
    t = z.shape[0]
    bw, heads, c_lat = lay["bw"], lay["heads"], lay["c_lat"]
    hq = heads * LANES
    if prompt:
        tm = _tile(seq_len, 512)
        assert seq_len % tm == 0 and tm >= 2 * SUBLANES
    else:
        tm = _tile(t, 512)
        assert tm % seq_len == 0
    ntab = cos_t.shape[0] // tm
    in_specs = [
        pl.BlockSpec((tm, bw), lambda i: (i, 0)),
        pl.BlockSpec((tm, bw), lambda i: (i, 1)),
        pl.BlockSpec((tm, bw), lambda i: (i, 2)),
        pl.BlockSpec((tm, hq), lambda i: (i, lay["off_q"] // hq)),
        pl.BlockSpec((tm, c_lat), lambda i: (i, lay["off_ckv"] // c_lat)),
        pl.BlockSpec((tm, LANES), lambda i: (i, lay["off_kr"] // LANES)),
        pl.BlockSpec((tm, LANES), lambda i: (i % ntab, 0)),
        pl.BlockSpec((tm, LANES), lambda i: (i % ntab, 0)),
        pl.BlockSpec(conv_w.shape, lambda i: (0, 0)),
        pl.BlockSpec((1, LANES), lambda i: (0, 0)),
        pl.BlockSpec((1, LANES), lambda i: (0, 0)),
        pl.BlockSpec((1, c_lat), lambda i: (0, 0)),
    ]
    args = [z, z, z, z, z, z, cos_t, sin_t, conv_w, gq, gkr, gkv]
    if prompt:
        rb = tm // SUBLANES
        in_specs += [
            pl.BlockSpec((SUBLANES, bw), lambda i: (jnp.maximum(i * rb - 1, 0), 0)),
            pl.BlockSpec((SUBLANES, bw), lambda i: (jnp.maximum(i * rb - 1, 0), 2)),
        ]
        args += [z, z]
        vt_spec = pl.BlockSpec((1, SUBLANES, bw), lambda i: (i, 0, 0))
        vt_shape = jax.ShapeDtypeStruct((t // tm, SUBLANES, bw), F32)
    else:
        in_specs += [pl.BlockSpec((tm, bw), lambda i: (i, 0)), pl.BlockSpec((tm, bw), lambda i: (i, 0))]
        args += [hist1, hist2]
        vt_spec = pl.BlockSpec((tm, bw), lambda i: (i, 0))
        vt_shape = jax.ShapeDtypeStruct((t, bw), F32)
    kern = functools.partial(_prep_kernel, heads=heads, nope=lay["nope"], rope=lay["rope"], seq_len=seq_len, tm=tm,
                             prompt=prompt)
    return pl.pallas_call(
        kern,
        grid=(t // tm,),
        in_specs=in_specs,
        out_specs=[
            pl.BlockSpec((tm, bw), lambda i: (i, 0)),
            pl.BlockSpec((tm, hq), lambda i: (i, 0)),
            pl.BlockSpec((tm, c_lat), lambda i: (i, 0)),
            pl.BlockSpec((tm, LANES), lambda i: (i, 0)),
            vt_spec,
        ],
        out_shape=[
            jax.ShapeDtypeStruct((t, bw), BF16),
            jax.ShapeDtypeStruct((t, hq), q_dtype),
            jax.ShapeDtypeStruct((t, c_lat), F32),
            jax.ShapeDtypeStruct((t, LANES), F32),
            vt_shape,
        ],
        compiler_params=_cparams(("parallel",)),
        name="prep_prompt" if prompt else "prep_sample",
    )(*args)


def _s5_mats(lam_re, lam_im, log_dt, b_re, b_im, c_re, c_im, lc):
    hp = lax.Precision.HIGHEST
    g, p, n_in = b_re.shape
    n_out = c_re.shape[1]
    dt = jnp.exp(log_dt)[:, None]
    lr, li = lam_re, lam_im
    mag = jnp.exp(lr * dt)
    a_re, a_im = mag * jnp.cos(li * dt), mag * jnp.sin(li * dt)
    den = lr * lr + li * li
    f_re = ((a_re - 1.0) * lr + a_im * li) / den
    f_im = (a_im * lr - (a_re - 1.0) * li) / den
    bb_re = f_re[..., None] * b_re - f_im[..., None] * b_im
    bb_im = f_re[..., None] * b_im + f_im[..., None] * b_re
    k = jnp.arange(lc + 1, dtype=F32)[:, None, None]
    pm = jnp.exp(lr[None] * dt[None] * k)
    pr = pm * jnp.cos(li[None] * dt[None] * k)
    pi = pm * jnp.sin(li[None] * dt[None] * k)
    ab_re = pr[:lc, ..., None] * bb_re[None] - pi[:lc, ..., None] * bb_im[None]
    ab_im = pr[:lc, ..., None] * bb_im[None] + pi[:lc, ..., None] * bb_re[None]
    ms = jnp.concatenate([ab_re[::-1], ab_im[::-1]], axis=2)
    ms = jnp.transpose(ms, (1, 0, 3, 2)).reshape(g, lc * n_in, 2 * p)
    kk = (jnp.einsum("gop,dgpi->dgoi", c_re, ab_re, precision=hp)
          - jnp.einsum("gop,dgpi->dgoi", c_im, ab_im, precision=hp))
    s_idx = jnp.arange(lc)[:, None]
    t_idx = jnp.arange(lc)[None, :]
    delta = t_idx - s_idx
    kt = jnp.where((delta >= 0)[:, :, None, None, None], kk[jnp.clip(delta, 0, lc - 1)], 0.0)
    tk = jnp.transpose(kt, (2, 0, 4, 1, 3)).reshape(g, lc * n_in, lc * n_out)
    ca_re = c_re[None] * pr[1:, :, None, :] - c_im[None] * pi[1:, :, None, :]
    ca_im = c_re[None] * pi[1:, :, None, :] + c_im[None] * pr[1:, :, None, :]
    gs = jnp.concatenate([ca_re, -ca_im], axis=3)
    gs = jnp.transpose(gs, (1, 3, 0, 2)).reshape(g, 2 * p, lc * n_out)
    return dict(ms=ms.astype(BF16), tk=tk.astype(BF16), gs=gs.astype(BF16), pr=pr, pi=pi, lc=lc, n_in=n_in, n_out=n_out)


def _s5_ops(m, lc):
    full, n_in, n_out = m["lc"], m["n_in"], m["n_out"]
    assert lc <= full
    ms = m["ms"][:, (full - lc) * n_in:, :]
    tk = m["tk"][:, :lc * n_in, :lc * n_out]
    gs = m["gs"][:, :, :lc * n_out]
    a1 = jnp.concatenate([m["pr"][lc], m["pr"][lc]], axis=-1)[:, None, :]
    a2 = jnp.concatenate([-m["pi"][lc], m["pi"][lc]], axis=-1)[:, None, :]
    return ms, tk, gs, a1, a2


def _s5_kernel(u_ref, tk_ref, ms_ref, gs_ref, a1_ref, a2_ref, h0_ref, y_ref, ht_ref, s_sc, sw_sc, hin_sc, *, n_blocks,
               bsz, spb, p_state):
    u = u_ref[...]
    s = jnp.dot(u, ms_ref[...], preferred_element_type=F32)
    s_sc[...] = s
    sw_sc[...] = pltpu.roll(s, p_state, 1)
    a1 = a1_ref[...]
    a2 = a2_ref[...]
    rb = spb * bsz

    def body(k, carry):
        h, hw = carry
        r = pl.multiple_of(k * rb, SUBLANES)
        s_blk = s_sc[pl.ds(r, rb), :]
        sw_blk = sw_sc[pl.ds(r, rb), :]
        hs = []
        for q in range(spb):
            hs.append(h)
            rows = slice(q * bsz, (q + 1) * bsz)
            h, hw = a1 * h + a2 * hw + s_blk[rows, :], a1 * hw - a2 * h + sw_blk[rows, :]
        hin_sc[pl.ds(r, rb), :] = hs[0] if spb == 1 else jnp.concatenate(hs, axis=0)
        return h, hw

    h0 = h0_ref[...]
    h, _ = lax.fori_loop(0, n_blocks, body, (h0, pltpu.roll(h0, p_state, 1)))
    ht_ref[...] = h
    y_ref[...] = (jnp.dot(u, tk_ref[...], preferred_element_type=F32)
                  + jnp.dot(hin_sc[...].astype(BF16), gs_ref[...], preferred_element_type=F32))


def _s5(us, h0_re, h0_im, mats, bsz, seq_len, lc):
    ms, tk, gs, a1, a2 = mats
    g, _, p2 = ms.shape
    p_state = p2 // 2
    n_in = ms.shape[1] // lc
    n_out = tk.shape[2] // lc
    n_chunks = seq_len // lc
    spb = max(1, SUBLANES // bsz)
    assert (spb * bsz) % SUBLANES == 0 and n_chunks % spb == 0
    nc = n_chunks * bsz
    bp = bsz
    u = us.astype(BF16).reshape(bsz, n_chunks, lc, g, n_in)
    u = jnp.transpose(u, (3, 1, 0, 2, 4)).reshape(g, nc, lc * n_in)
    h0 = jnp.transpose(jnp.concatenate([h0_re, h0_im], axis=-1), (1, 0, 2))
    kern = functools.partial(_s5_kernel, n_blocks=n_chunks // spb, bsz=bsz, spb=spb, p_state=p_state)
    y, ht = pl.pallas_call(
        kern,
        grid=(g,),
        in_specs=[
            pl.BlockSpec((None, nc, lc * n_in), lambda i: (i, 0, 0)),
            pl.BlockSpec((None, lc * n_in, lc * n_out), lambda i: (i, 0, 0)),
            pl.BlockSpec((None, lc * n_in, p2), lambda i: (i, 0, 0)),
            pl.BlockSpec((None, p2, lc * n_out), lambda i: (i, 0, 0)),
            pl.BlockSpec((None, 1, p2), lambda i: (i, 0, 0)),
            pl.BlockSpec((None, 1, p2), lambda i: (i, 0, 0)),
            pl.BlockSpec((None, bp, p2), lambda i: (i, 0, 0)),
        ],
        out_specs=[
            pl.BlockSpec((None, nc, lc * n_out), lambda i: (i, 0, 0)),
            pl.BlockSpec((None, bp, p2), lambda i: (i, 0, 0)),
        ],
        out_shape=[
            jax.ShapeDtypeStruct((g, nc, lc * n_out), F32),
            jax.ShapeDtypeStruct((g, bp, p2), F32),
        ],
        scratch_shapes=[pltpu.VMEM((nc, p2), F32), pltpu.VMEM((nc, p2), F32), pltpu.VMEM((nc, p2), F32)],
        compiler_params=_cparams(("parallel",)),
        name="s5",
    )(u, tk, ms, gs, a1, a2, h0)
    y = y.reshape(g, n_chunks, bp, lc, n_out)[:, :, :bsz]
    y = jnp.transpose(y, (2, 1, 3, 0, 4)).reshape(bsz * seq_len, g * n_out)
    ht = jnp.transpose(ht[:, :bsz], (1, 0, 2))
    return y, ht[..., :p_state], ht[..., p_state:]


def _kprep_kernel(ckv_ref, kr_ref, wuk_ref, wuv_ref, gkn_ref, k_ref, v_ref, *, heads, nope):
    c = ckv_ref[...].astype(BF16)
    kraw = jnp.dot(c, wuk_ref[...], preferred_element_type=F32)
    kr = kr_ref[...]
    gkn = gkn_ref[...]
    for h in range(heads):
        sl = slice(LANES * h, LANES * (h + 1))
        kh = kraw[:, sl]
        ss = jnp.sum(kh * kh, axis=-1, keepdims=True)
        k_ref[:, sl] = (kh * lax.rsqrt(ss / nope + EPS) * gkn + kr).astype(BF16)
    v_ref[...] = jnp.dot(c, wuv_ref[...], preferred_element_type=F32).astype(BF16)


def _kprep(ckv_n, kr_n, wuk_p, wuv, gkn, heads, nope):
    t, c_lat = ckv_n.shape
    tm = _tile(t, 512)
    hk = wuk_p.shape[1]
    hv = wuv.shape[1]
    return pl.pallas_call(
        functools.partial(_kprep_kernel, heads=heads, nope=nope),
        grid=(t // tm,),
        in_specs=[
            pl.BlockSpec((tm, c_lat), lambda i: (i, 0)),
            pl.BlockSpec((tm, LANES), lambda i: (i, 0)),
            pl.BlockSpec((c_lat, hk), lambda i: (0, 0)),
            pl.BlockSpec((c_lat, hv), lambda i: (0, 0)),
            pl.BlockSpec((1, LANES), lambda i: (0, 0)),
        ],
        out_specs=[pl.BlockSpec((tm, hk), lambda i: (i, 0)), pl.BlockSpec((tm, hv), lambda i: (i, 0))],
        out_shape=[jax.ShapeDtypeStruct((t, hk), BF16), jax.ShapeDtypeStruct((t, hv), BF16)],
        compiler_params=_cparams(("parallel",)),
        name="kprep",
    )(ckv_n, kr_n, wuk_p, wuv, gkn)


def _flash_kernel(ii_ref, jj_ref, q_ref, k_ref, v_ref, o_ref, m_sc, l_sc, acc_sc, *, c_exp, tq, hps, vd):
    i = ii_ref[pl.program_id(2)]
    j = jj_ref[pl.program_id(2)]
    nct = tq // LANES

    @pl.when(j == 0)
    def _():
        m_sc[...] = jnp.full(m_sc.shape, NEG_INF, F32)
        l_sc[...] = jnp.zeros(l_sc.shape, F32)
        acc_sc[...] = jnp.zeros(acc_sc.shape, F32)

    def step(diagonal):
        if diagonal:
            row = lax.broadcasted_iota(jnp.int32, (tq, tq), 0)
            col = lax.broadcasted_iota(jnp.int32, (tq, tq), 1)
            keep = col <= row
        for hh in range(hps):
            q = q_ref[:, LANES * hh:LANES * (hh + 1)]
            k = k_ref[:, LANES * hh:LANES * (hh + 1)]
            s = lax.dot_general(q, k, (((1,), (1,)), ((), ())), preferred_element_type=F32)
            if diagonal:
                s = jnp.where(keep, s, NEG_INF)
            m_prev = m_sc[hh]
            m_new = jnp.maximum(m_prev, jnp.max(s, axis=-1, keepdims=True))
            alpha = jnp.exp2((m_prev - m_new) * c_exp)
            ps = [jnp.exp2((s[:, LANES * c:LANES * (c + 1)] - m_new) * c_exp) for c in range(nct)]
            psum = ps[0]
            for c in range(1, nct):
                psum = psum + ps[c]
            l_sc[hh] = alpha * l_sc[hh] + jnp.sum(psum, axis=-1, keepdims=True)
            p = jnp.concatenate(ps, axis=1).astype(BF16)
            acc_sc[hh] = alpha[:, :vd] * acc_sc[hh] + jnp.dot(p, v_ref[:, vd * hh:vd * (hh + 1)],
                                                              preferred_element_type=F32)
            m_sc[hh] = m_new

    @pl.when(j < i)
    def _():
        step(False)

    @pl.when(j == i)
    def _():
        step(True)
        for hh in range(hps):
            o_ref[:, vd * hh:vd * (hh + 1)] = (acc_sc[hh] / l_sc[hh][:, :vd]).astype(o_ref.dtype)


def _flash(q, k, v, bsz, seq_len, heads, vd, scale):
    hps = LANES // vd
    assert heads % hps == 0
    tq = _tile(seq_len, 1024)
    nq = seq_len // tq
    t = bsz * seq_len
    assert tq % LANES == 0
    kern = functools.partial(_flash_kernel, c_exp=scale * math.log2(math.e), tq=tq, hps=hps, vd=vd)
    pairs = [(i, j) for i in range(nq) for j in range(i + 1)]
    ii = jnp.asarray([p[0] for p in pairs], jnp.int32)
    jj = jnp.asarray([p[1] for p in pairs], jnp.int32)
    grid_spec = pltpu.PrefetchScalarGridSpec(
        num_scalar_prefetch=2,
        grid=(bsz, heads // hps, len(pairs)),
        in_specs=[
            pl.BlockSpec((tq, hps * LANES), lambda b, h, s, ii, jj: (b * nq + ii[s], h)),
            pl.BlockSpec((tq, hps * LANES), lambda b, h, s, ii, jj: (b * nq + jj[s], h)),
            pl.BlockSpec((tq, LANES), lambda b, h, s, ii, jj: (b * nq + jj[s], h)),
        ],
        out_specs=pl.BlockSpec((tq, LANES), lambda b, h, s, ii, jj: (b * nq + ii[s], h)),
        scratch_shapes=[pltpu.VMEM((hps, tq, LANES), F32), pltpu.VMEM((hps, tq, LANES), F32),
                        pltpu.VMEM((hps, tq, vd), F32)],
    )
    return pl.pallas_call(
        kern,
        grid_spec=grid_spec,
        out_shape=jax.ShapeDtypeStruct((t, heads * vd), BF16),
        compiler_params=_cparams(("parallel", "parallel", "arbitrary")),
        name="flash",
    )(ii, jj, q, k, v)


def _paged_kernel(pt_ref, qbd_ref, qr_ref, cn_ref, rn_ref, *refs, pp, heads, nope, vd, lq, scale):
    del pt_ref
    c_pages = refs[:pp]
    r_pages = refs[pp:2 * pp]
    wuk_ref, wuv_ref, e_ref, gk_ref, o_ref, m_sc, l_sc, acc_sc, cb_sc, rb_sc = refs[2 * pp:]
    s_id = pl.program_id(1)
    rows = lq * heads
    page = c_pages[0].shape[0]

    @pl.when(s_id == 0)
    def _():
        m_sc[...] = jnp.full(m_sc.shape, NEG_INF, F32)
        l_sc[...] = jnp.zeros(l_sc.shape, F32)
        acc_sc[...] = jnp.zeros(acc_sc.shape, F32)

    qbd = (qbd_ref[...] * gk_ref[...]).astype(BF16)
    qr = qr_ref[...].astype(BF16)
    dn = (((1,), (1,)), ((), ()))
    qlat = lax.dot_general(qbd, wuk_ref[...], dn, preferred_element_type=F32).astype(BF16)

    def scores(cb, rb):
        kraw = jnp.dot(cb, wuk_ref[...], preferred_element_type=F32)
        ssq = lax.dot_general(e_ref[...], (kraw * kraw).astype(BF16), dn, preferred_element_type=F32)
        inv = lax.rsqrt(ssq / nope + EPS)
        inv = jnp.concatenate([inv] * lq, axis=0)
        sn = lax.dot_general(qlat, cb, dn, preferred_element_type=F32)
        sr = jnp.dot(qr, rb, preferred_element_type=F32)
        return (sn * inv + sr) * scale

    def update(cb, rb, causal):
        s = scores(cb, rb)
        if causal:
            nk = cb.shape[0]
            kk = lax.broadcasted_iota(jnp.int32, (rows, nk), 1)
            qq = lax.broadcasted_iota(jnp.int32, (rows, nk), 0) // heads
            s = jnp.where(kk <= qq, s, NEG_INF)
        m_prev = m_sc[...]
        m_new = jnp.maximum(m_prev, jnp.max(s, axis=-1, keepdims=True))
        alpha = jnp.exp(m_prev - m_new)
        p = jnp.exp(s - m_new)
        l_sc[...] = alpha * l_sc[...] + jnp.sum(p, axis=-1, keepdims=True)
        acc_sc[...] = alpha * acc_sc[...] + jnp.dot(p.astype(BF16), cb, preferred_element_type=F32)
        m_sc[...] = m_new

    for pg in range(pp):
        cb_sc[page * pg:page * (pg + 1), :] = c_pages[pg][...].astype(BF16)
        rb_sc[:, page * pg:page * (pg + 1)] = r_pages[pg][...].astype(BF16)
    update(cb_sc[...], rb_sc[...], False)

    @pl.when(s_id == pl.num_programs(1) - 1)
    def _():
        update(cn_ref[...].astype(BF16), rn_ref[...].astype(BF16), True)
        lat = (acc_sc[...] / l_sc[...]).astype(BF16)
        full = jnp.dot(lat, wuv_ref[...], preferred_element_type=F32)
        colh = lax.broadcasted_iota(jnp.int32, full.shape, 1) // vd
        rowh = lax.broadcasted_iota(jnp.int32, full.shape, 0) % heads
        full = jnp.where(colh == rowh, full, 0.0)
        o_ref[...] = jnp.sum(full.reshape(lq, heads, heads * vd), axis=1)


def _paged(page_table, qbd, qr, c_new, r_new, cache_ckv, cache_krope, layer, wuk, wuv, e_mat, gk, heads, nope, vd,
           scale):
    bs, rows, _ = qbd.shape
    lq = rows // heads
    n_pages = page_table.shape[1]
    page, c_lat = cache_ckv.shape[2], cache_ckv.shape[3]
    rope = cache_krope.shape[3]
    pp = math.gcd(n_pages, 32)
    kn = c_new.shape[1]
    cache_krope = jnp.swapaxes(cache_krope, 2, 3)
    r_new = jnp.swapaxes(r_new, 1, 2)

    def cmap(p):
        return lambda b, s, pt: (layer, pt[b, s * pp + p], 0, 0)

    in_specs = [
        pl.BlockSpec((None, rows, heads * nope), lambda b, s, pt: (b, 0, 0)),
        pl.BlockSpec((None, rows, rope), lambda b, s, pt: (b, 0, 0)),
        pl.BlockSpec((None, kn, c_lat), lambda b, s, pt: (b, 0, 0)),
        pl.BlockSpec((None, rope, kn), lambda b, s, pt: (b, 0, 0)),
    ]
    in_specs += [pl.BlockSpec((None, None, page, c_lat), cmap(p)) for p in range(pp)]
    in_specs += [pl.BlockSpec((None, None, rope, page), cmap(p)) for p in range(pp)]
    in_specs += [
        pl.BlockSpec(wuk.shape, lambda b, s, pt: (0, 0)),
        pl.BlockSpec(wuv.shape, lambda b, s, pt: (0, 0)),
        pl.BlockSpec(e_mat.shape, lambda b, s, pt: (0, 0)),
        pl.BlockSpec(gk.shape, lambda b, s, pt: (0, 0)),
    ]
    kern = functools.partial(_paged_kernel, pp=pp, heads=heads, nope=nope, vd=vd, lq=lq, scale=scale)
    grid_spec = pltpu.PrefetchScalarGridSpec(
        num_scalar_prefetch=1,
        grid=(bs, n_pages // pp),
        in_specs=in_specs,
        out_specs=pl.BlockSpec((None, lq, heads * vd), lambda b, s, pt: (b, 0, 0)),
        scratch_shapes=[pltpu.VMEM((rows, 1), F32), pltpu.VMEM((rows, 1), F32), pltpu.VMEM((rows, c_lat), F32),
                        pltpu.VMEM((pp * page, c_lat), BF16), pltpu.VMEM((rope, pp * page), BF16)],
    )
    return pl.pallas_call(
        kern,
        grid_spec=grid_spec,
        out_shape=jax.ShapeDtypeStruct((bs, lq, heads * vd), F32),
        compiler_params=_cparams(("parallel", "arbitrary")),
        name="paged",
    )(page_table, qbd, qr, c_new, r_new, *([cache_ckv] * pp), *([cache_krope] * pp), wuk, wuv, e_mat, gk)


def _merge_kernel(x_ref, oc_ref, ys_ref, us_ref, oa_ref, g0_ref, g1_ref, g2_ref, d_ref, wglu_ref, bglu_ref, wbr_ref,
                  wo_ref, gffn_ref, wpq_ref, x1_ref, h2_ref, qp_ref):
    y = ys_ref[...] + d_ref[...] * us_ref[...]
    zg = jax.nn.gelu(y)
    gl = jnp.dot(zg.astype(BF16), wglu_ref[...], preferred_element_type=F32) + bglu_ref[...]
    o_ssm = zg * jax.nn.sigmoid(gl)
    merged = jax.nn.sigmoid(g0_ref[...]) * jnp.dot(oc_ref[...], wbr_ref[0], preferred_element_type=F32)
    merged += jax.nn.sigmoid(g1_ref[...]) * jnp.dot(o_ssm.astype(BF16), wbr_ref[1], preferred_element_type=F32)
    merged += jax.nn.sigmoid(g2_ref[...]) * jnp.dot(oa_ref[...].astype(BF16), wbr_ref[2], preferred_element_type=F32)
    x1 = x_ref[...] + jnp.dot(merged.astype(BF16), wo_ref[...], preferred_element_type=F32)
    x1_ref[...] = x1
    ms = jnp.mean(x1 * x1, axis=-1, keepdims=True)
    h2 = x1 * lax.rsqrt(ms + EPS) * gffn_ref[...]
    h2_ref[...] = h2
    qp_ref[...] = jnp.dot(h2.astype(BF16), wpq_ref[...], preferred_element_type=F32)


def _merge(x2d, oc, ys, z, oa, lay, d, wglu, bglu, wbr, wo, gffn, wpq):
    t, dm = x2d.shape
    bw = lay["bw"]
    dq = wpq.shape[1]
    tm = _tile(t, 256)
    gi = lay["off_g"] // dm

    def full(a):
        return pl.BlockSpec(a.shape, lambda i, _n=a.ndim: (0,) * _n)

    return pl.pallas_call(
        _merge_kernel,
        grid=(t // tm,),
        in_specs=[
            pl.BlockSpec((tm, dm), lambda i: (i, 0)),
            pl.BlockSpec((tm, bw), lambda i: (i, 0)),
            pl.BlockSpec((tm, bw), lambda i: (i, 0)),
            pl.BlockSpec((tm, bw), lambda i: (i, 3)),
            pl.BlockSpec((tm, bw), lambda i: (i, 0)),
            pl.BlockSpec((tm, dm), lambda i: (i, gi)),
            pl.BlockSpec((tm, dm), lambda i: (i, gi + 1)),
            pl.BlockSpec((tm, dm), lambda i: (i, gi + 2)),
            full(d), full(wglu), full(bglu), full(wbr), full(wo), full(gffn), full(wpq),
        ],
        out_specs=[
            pl.BlockSpec((tm, dm), lambda i: (i, 0)),
            pl.BlockSpec((tm, dm), lambda i: (i, 0)),
            pl.BlockSpec((tm, dq), lambda i: (i, 0)),
        ],
        out_shape=[
            jax.ShapeDtypeStruct((t, dm), F32),
            jax.ShapeDtypeStruct((t, dm), F32),
            jax.ShapeDtypeStruct((t, dq), F32),
        ],
        compiler_params=_cparams(("parallel",)),
        name="merge",
    )(x2d, oc, ys, z, oa, z, z, z, d, wglu, bglu, wbr, wo, gffn, wpq)


def _topk_rows(x, k):
    n = x.shape[0]
    iota = lax.broadcasted_iota(jnp.int32, x.shape, 0)
    vals, idxs = [], []
    for _ in range(k):
        m = jnp.max(x, axis=0, keepdims=True)
        am = jnp.min(jnp.where(x == m, iota, n), axis=0, keepdims=True)
        vals.append(m)
        idxs.append(am)
        x = jnp.where(iota == am, -jnp.inf, x)
    return jnp.concatenate(vals, axis=0), jnp.concatenate(idxs, axis=0)


def _select_rows(table, sel, k):
    out = jnp.zeros(sel.shape, table.dtype)
    for r in range(k):
        out = jnp.where(sel == r, table[r:r + 1, :], out)
    return out


def _pair_candidates(v1, v2, k):
    chunks, meta, r0, i = [], [], 0, 0
    while i < k and k // (i + 1) >= 2:
        n = k // (i + 1)
        nr = -(-n // SUBLANES) * SUBLANES
        blk = v1[i:i + 1, :] + v2[0:nr, :]
        if n < nr:
            blk = jnp.where(lax.broadcasted_iota(jnp.int32, blk.shape, 0) < n, blk, -jnp.inf)
        chunks.append(blk)
        meta.append((r0, nr, i, None))
        r0 += nr
        i += 1
    assert (k - i) % SUBLANES == 0
    chunks.append(v1[i:k, :] + v2[0:1, :])
    meta.append((r0, k - i, None, i))
    return jnp.concatenate(chunks, axis=0), meta


def _route_kernel(qp_ref, keys_ref, e_ref, g_ref, *, heads, n_keys, dk, topk, row_mult):
    dn = (((1,), (1,)), ((), ()))
    for h in range(heads):
        sub = []
        for s in range(2):
            o = (2 * h + s) * dk
            qs = qp_ref[:, o:o + dk].astype(BF16)
            st = lax.dot_general(keys_ref[h, s], qs, dn, preferred_element_type=F32)
            sub.append(_topk_rows(st, topk))
        (v1, i1), (v2, i2) = sub
        cand, meta = _pair_candidates(v1, v2, topk)
        sc, ci = _topk_rows(cand, topk)
        ihi = jnp.zeros(ci.shape, jnp.int32)
        jlo = jnp.zeros(ci.shape, jnp.int32)
        for r0, nr, ic, i0 in meta:
            inr = (ci >= r0) & (ci < r0 + nr)
            if ic is None:
                ihi = jnp.where(inr, ci - r0 + i0, ihi)
            else:
                ihi = jnp.where(inr, ic, ihi)
                jlo = jnp.where(inr, ci - r0, jlo)
        e1 = _select_rows(i1, ihi, topk)
        e2 = _select_rows(i2, jlo, topk)
        ex = jnp.exp(sc - sc[0:1, :])
        e_ref[topk * h:topk * (h + 1), :] = (e1 * n_keys + e2) * row_mult
        g_ref[topk * h:topk * (h + 1), :] = ex / jnp.sum(ex, axis=0, keepdims=True)


def _route(qp, keys_bf16, row_mult):
    t, dq = qp.shape
    heads, _, n_keys, dk = keys_bf16.shape
    tm = _tile(t, 128)
    if t % LANES:
        tm = t
    npair = heads * PEER_TOPK
    kern = functools.partial(_route_kernel, heads=heads, n_keys=n_keys, dk=dk, topk=PEER_TOPK, row_mult=row_mult)
    return pl.pallas_call(
        kern,
        grid=(t // tm,),
        in_specs=[
            pl.BlockSpec((tm, dq), lambda i: (i, 0)),
            pl.BlockSpec(keys_bf16.shape, lambda i: (0, 0, 0, 0)),
        ],
        out_specs=[pl.BlockSpec((npair, tm), lambda i: (0, i)), pl.BlockSpec((npair, tm), lambda i: (0, i))],
        out_shape=[jax.ShapeDtypeStruct((npair, t), jnp.int32), jax.ShapeDtypeStruct((npair, t), F32)],
        compiler_params=_cparams(("parallel",)),
        name="route",
    )(qp, keys_bf16)


HI_MASK = 0xFFFF0000
_BUTTERFLY_ORDER = (0, 4, 2, 6, 1, 5, 3, 7)


def _pack_table(tab):
    e, d = tab.shape
    half = d // 2
    b = lax.bitcast_convert_type(tab.astype(BF16), jnp.uint16).astype(jnp.uint32)
    return (b[:, :half] | (b[:, half:] << 16)).reshape(e * (half // LANES), LANES)


def _pack_table_rows(tab):
    e, d = tab.shape
    b = lax.bitcast_convert_type(tab.astype(BF16), jnp.uint16).astype(jnp.uint32).reshape(e, d // (2 * LANES), 2, LANES)
    return (b[:, :, 0, :] | (b[:, :, 1, :] << 16)).reshape(e * (d // (2 * LANES)), LANES)


def _unpack(x):
    return pltpu.bitcast(x << 16, F32), pltpu.bitcast(x & jnp.uint32(HI_MASK), F32)


def _fold_sublanes(x, y, k, mask):
    return jnp.where(mask, x, pltpu.roll(y, k, 0)) + jnp.where(mask, pltpu.roll(x, SUBLANES - k, 0), y)


def _peer_u_kernel(e_ref, h_ref, tab_ref, a_ref, *, tb, npair, rows):
    sub = lax.broadcasted_iota(jnp.int32, (SUBLANES, LANES), 0)
    lane = lax.broadcasted_iota(jnp.int32, (SUBLANES, LANES), 1)
    m4 = sub < 4
    m2 = (sub % 4) < 2
    m1 = (sub % 2) == 0
    ng = npair // SUBLANES

    def body(t, carry):
        th = h_ref[pl.ds(t, 1), :].reshape(SUBLANES, LANES)
        out = jnp.zeros((SUBLANES, LANES), F32)
        for g in range(ng):
            ps = []
            for j in _BUTTERFLY_ORDER:
                idx = pl.multiple_of(e_ref[t, SUBLANES * g + j], rows)
                u_row = pltpu.bitcast(tab_ref[pl.ds(idx, rows), :], BF16).astype(F32)
                ps.append(u_row * th)
            v = [_fold_sublanes(ps[2 * k], ps[2 * k + 1], 4, m4) for k in range(4)]
            r = _fold_sublanes(_fold_sublanes(v[0], v[1], 2, m2), _fold_sublanes(v[2], v[3], 2, m2), 1, m1)
            out = jnp.where(lane == g, jnp.sum(r, axis=-1, keepdims=True), out)
        a_ref[t] = out[:, :ng]
        return carry

    lax.fori_loop(0, tb, body, 0, unroll=16)


def _peer_v_kernel(e_ref, w_ref, x_ref, tab_ref, o_ref, *, tb, npair, rows):
    def body(t, carry):
        lo = [jnp.zeros((rows, LANES), F32) for _ in range(2)]
        hi = [jnp.zeros((rows, LANES), F32) for _ in range(2)]
        for j in range(npair):
            idx = pl.multiple_of(e_ref[t, j], rows)
            w = w_ref[t, j]
            x_lo, x_hi = _unpack(tab_ref[pl.ds(idx, rows), :])
            lo[j % 2] = lo[j % 2] + w * x_lo
            hi[j % 2] = hi[j % 2] + w * x_hi
        delta = jnp.concatenate([lo[0] + lo[1], hi[0] + hi[1]], axis=0)
        x_t = x_ref[pl.ds(t, 1), :].reshape(2 * rows, LANES)
        o_ref[pl.ds(t, 1), :] = (x_t + delta).reshape(1, 2 * rows * LANES)
        return carry

    lax.fori_loop(0, tb, body, 0, unroll=2)


def _peer_w_kernel(a_ref, g_ref, w_ref):
    w_ref[...] = g_ref[...] * jax.nn.gelu(a_ref[...])


def _peer(x1, h2, e_pt, g_pt, u_pk, v_pk):
    t, dm = x1.shape
    npair = e_pt.shape[0]
    rows = dm // (2 * LANES)
    assert 2 * rows == SUBLANES and npair % SUBLANES == 0
    tb = _tile(t, 64)
    nblk = t // tb
    ng = npair // SUBLANES
    e_nat = jnp.transpose(e_pt)
    e_u = jnp.transpose(e_nat.reshape(t, SUBLANES, ng), (0, 2, 1)).reshape(nblk, tb, npair)
    e_v = e_nat.reshape(nblk, tb, npair)
    g_nat = jnp.transpose(g_pt)
    smem = functools.partial(pl.BlockSpec, memory_space=pltpu.SMEM)
    tab_spec = pl.BlockSpec(u_pk.shape, lambda i: (0, 0), pipeline_mode=pl.Buffered(1))
    tok_spec = pl.BlockSpec((tb, dm), lambda i: (i, 0))
    idx_spec = smem((None, tb, npair), lambda i: (i, 0, 0))
    a = pl.pallas_call(
        functools.partial(_peer_u_kernel, tb=tb, npair=npair, rows=rows),
        grid=(nblk,),
        in_specs=[idx_spec, tok_spec, tab_spec],
        out_specs=pl.BlockSpec((tb, SUBLANES, ng), lambda i: (i, 0, 0)),
        out_shape=jax.ShapeDtypeStruct((t, SUBLANES, ng), F32),
        compiler_params=_cparams(("arbitrary",), VMEM_LIMIT_TABLE),
        name="peer_u",
    )(e_u, h2, u_pk)
    tw = _tile(t, 512)
    w = pl.pallas_call(
        _peer_w_kernel,
        grid=(t // tw,),
        in_specs=[pl.BlockSpec((tw, npair), lambda i: (i, 0)), pl.BlockSpec((tw, npair), lambda i: (i, 0))],
        out_specs=pl.BlockSpec((tw, npair), lambda i: (i, 0)),
        out_shape=jax.ShapeDtypeStruct((t, npair), F32),
        compiler_params=_cparams(("parallel",)),
        name="peer_w",
    )(a.reshape(t, npair), g_nat)
    return pl.pallas_call(
        functools.partial(_peer_v_kernel, tb=tb, npair=npair, rows=rows),
        grid=(nblk,),
        in_specs=[idx_spec, idx_spec, tok_spec, tab_spec],
        out_specs=tok_spec,
        out_shape=jax.ShapeDtypeStruct((t, dm), F32),
        compiler_params=_cparams(("arbitrary",), VMEM_LIMIT_TABLE),
        name="peer_v",
    )(e_v, w.reshape(nblk, tb, npair), x1, v_pk)


def _layout(dm, bw, heads, nope, rope, c_lat):
    hq = heads * LANES
    lay = dict(bw=bw, heads=heads, nope=nope, rope=rope, c_lat=c_lat)
    lay["off_q"] = 4 * bw
    lay["off_g"] = lay["off_q"] + hq
    lay["off_ckv"] = lay["off_g"] + 3 * dm
    lay["off_kr"] = lay["off_ckv"] + c_lat
    lay["n"] = lay["off_kr"] + LANES
    assert nope + rope <= LANES and rope % 2 == 0
    assert lay["off_q"] % hq == 0 and lay["off_g"] % dm == 0 and lay["off_ckv"] % c_lat == 0
    assert bw % LANES == 0 and c_lat % LANES == 0
    return lay


def _pack_w_in(w_in, lay, dm):
    bw, heads, nope, rope, c_lat = lay["bw"], lay["heads"], lay["nope"], lay["rope"], lay["c_lat"]
    sizes = (bw, bw, bw, bw, heads * (nope + rope), c_lat, rope, 3 * dm)
    parts, off = [], 0
    for n in sizes:
        parts.append(w_in[:, off:off + n])
        off += n
    xc, bc, cc, us, q, ckv, kr, gates = parts
    q = jnp.pad(q.reshape(dm, heads, nope + rope), ((0, 0), (0, 0), (0, LANES - nope - rope))).reshape(dm, heads * LANES)
    kr = jnp.pad(kr, ((0, 0), (nope, LANES - nope - rope)))
    return jnp.concatenate([xc, bc, cc, us, q, gates, ckv, kr], axis=1).astype(BF16)


def _rope_tables(pos, nope, rope):
    half = rope // 2
    inv = ROPE_THETA ** (-jnp.arange(half, dtype=F32) / half)
    ang = pos.astype(F32)[:, None] * inv[None, :]
    cos, sin = jnp.cos(ang), jnp.sin(ang)
    n = pos.shape[0]
    ones = jnp.ones((n, nope), F32)
    tail = LANES - nope - rope
    cos_t = jnp.concatenate([ones, cos, cos, jnp.ones((n, tail), F32)], axis=1)
    sin_t = jnp.concatenate([jnp.zeros((n, nope), F32), -sin, sin, jnp.zeros((n, tail), F32)], axis=1)
    return cos_t, sin_t


def _lane_vec(parts):
    v = jnp.concatenate(parts)
    return jnp.pad(v, (0, LANES - v.shape[0])).reshape(1, LANES)


def _layer(x2d, bsz, seq_len, pos, lp, lay, prompt, conv_hist, h0_re, h0_im, sample_ctx):
    t, dm = x2d.shape
    bw, heads, nope, rope, c_lat = lay["bw"], lay["heads"], lay["nope"], lay["rope"], lay["c_lat"]
    vd = lp["w_uv"].shape[2]
    scale = (nope + rope) ** -0.5

    z = _proj(x2d, lp["g_mix"], lp["w_in_p"])

    cos_t, sin_t = _rope_tables(pos, nope, rope)
    gq = _lane_vec([lp["g_qn"], lp["g_qr"]])
    gkr = _lane_vec([jnp.zeros((nope,), F32), lp["g_kr"]])
    gkv = lp["g_kv"].reshape(1, c_lat)
    if prompt:
        oc, qn, ckv_n, kr_n, vt = _prep(z, lay, cos_t, sin_t, lp["conv_w"], gq, gkr, gkv, seq_len, True)
        nblk_seq = seq_len // (t // vt.shape[0])
        new_hist = vt.reshape(bsz, nblk_seq, SUBLANES, bw)[:, -1, SUBLANES - 2:, :]
    else:
        tm = _tile(t, 512)
        reps = tm // seq_len
        cos_t = jnp.tile(cos_t, (reps, 1))
        sin_t = jnp.tile(sin_t, (reps, 1))
        zeros = jnp.zeros((bsz, seq_len - 1, bw), F32)
        hist1 = jnp.concatenate([conv_hist[:, 1:2], zeros], axis=1).reshape(t, bw)
        hist2 = jnp.concatenate([conv_hist, zeros[:, 1:]], axis=1).reshape(t, bw)
        oc, qn, ckv_n, kr_n, vt = _prep(z, lay, cos_t, sin_t, lp["conv_w"], gq, gkr, gkv, seq_len, False, hist1, hist2,
                                        q_dtype=F32)
        new_hist = vt.reshape(bsz, seq_len, bw)[:, seq_len - 2:, :]

    lc = math.gcd(seq_len, S5_CHUNK)
    us = z[:, 3 * bw:4 * bw]
    ys, ht_re, ht_im = _s5(us, h0_re, h0_im, _s5_ops(lp["s5_mats"], lc), bsz, seq_len, lc)

    gkn = _lane_vec([lp["g_kn"]])
    if prompt:
        k_full, v_all = _kprep(ckv_n, kr_n, lp["w_uk_p"], lp["w_uv_f"], gkn, heads, nope)
        oa = _flash(qn, k_full, v_all, bsz, seq_len, heads, vd, scale)
    else:
        cache_ckv, cache_krope, page_table, layer = sample_ctx
        q4 = qn.reshape(bsz, seq_len, heads, LANES)
        eye = jnp.eye(heads, dtype=F32)
        qbd = (q4[..., :nope][:, :, :, None, :] * eye[None, None, :, :, None]).reshape(bsz, seq_len * heads, heads * nope)
        qr = q4[..., nope:nope + rope].reshape(bsz, seq_len * heads, rope)
        kn = -(-seq_len // 16) * 16
        c_new = jnp.pad(ckv_n.reshape(bsz, seq_len, c_lat), ((0, 0), (0, kn - seq_len), (0, 0)))
        r_new = jnp.pad(kr_n[:, nope:nope + rope].reshape(bsz, seq_len, rope), ((0, 0), (0, kn - seq_len), (0, 0)))
        e_mat = jnp.repeat(jnp.eye(heads, dtype=F32), nope, axis=1).astype(BF16)
        gk = jnp.tile(lp["g_kn"], heads).reshape(1, heads * nope)
        oa = _paged(page_table, qbd, qr, c_new, r_new, cache_ckv, cache_krope, layer, lp["w_uk_f"], lp["w_uv_f"], e_mat,
                    gk, heads, nope, vd, scale).reshape(t, heads * vd)

    x1, h2, qp = _merge(x2d, oc, ys, z, oa, lay, lp["ssm_d"].reshape(1, bw), lp["w_glu_b"], lp["b_glu"].reshape(1, bw),
                        lp["w_br_b"], lp["w_o_b"], lp["g_ffn"].reshape(1, dm), lp["w_pq_b"])
    e_pt, g_pt = _route(qp, lp["peer_keys_b"], dm // (2 * LANES))
    x2 = _peer(x1, h2, e_pt, g_pt, lp["peer_u_pk"], lp["peer_v_pk"])
    return x2, new_hist, ht_re, ht_im, ckv_n, kr_n[:, nope:nope + rope]


def kernel(x_prompt, x_sample, cache_conv, state_ssm_re, state_ssm_im, cache_ckv, cache_krope, page_table, g_mix, w_in, conv_w, ssm_lam_re, ssm_lam_im, ssm_log_dt, ssm_b_re, ssm_b_im, ssm_c_re, ssm_c_im, ssm_d, w_glu, b_glu, g_kv, w_uk, w_uv, g_qn, g_kn, g_qr, g_kr, w_br, w_o, g_ffn, w_pq, peer_keys, peer_u, peer_v):
    bp, lp_len, dm = x_prompt.shape
    bs, ls, _ = x_sample.shape
    depth = w_in.shape[0]
    bw = conv_w.shape[2]
    c_lat, heads, nope = w_uk.shape[1], w_uk.shape[2], w_uk.shape[3]
    vd = w_uv.shape[3]
    rope = cache_krope.shape[3]
    page = cache_ckv.shape[2]
    past = page_table.shape[1] * page
    n_groups, p_state = ssm_lam_re.shape[1], ssm_lam_re.shape[2]
    assert ssm_d.shape[1] == bw and lp_len % page == 0
    lay = _layout(dm, bw, heads, nope, rope, c_lat)

    pos_p = jnp.arange(lp_len, dtype=jnp.int32)
    pos_s = past + jnp.arange(ls, dtype=jnp.int32)
    xp = x_prompt.reshape(bp * lp_len, dm)
    xs = x_sample.reshape(bs * ls, dm)
    outs = {k: [] for k in ("p_conv", "p_re", "p_im", "p_ckv", "p_kr", "s_conv", "s_re", "s_im", "s_ckv", "s_kr")}
    for l in range(depth):
        lp = dict(
            g_mix=g_mix[l], conv_w=conv_w[l], ssm_lam_re=ssm_lam_re[l], ssm_lam_im=ssm_lam_im[l],
            ssm_log_dt=ssm_log_dt[l], ssm_b_re=ssm_b_re[l], ssm_b_im=ssm_b_im[l], ssm_c_re=ssm_c_re[l],
            ssm_c_im=ssm_c_im[l], ssm_d=ssm_d[l], b_glu=b_glu[l], g_kv=g_kv[l], g_qn=g_qn[l], g_kn=g_kn[l],
            g_qr=g_qr[l], g_kr=g_kr[l], g_ffn=g_ffn[l], w_uv=w_uv[l],
        )
        lp["s5_mats"] = _s5_mats(ssm_lam_re[l], ssm_lam_im[l], ssm_log_dt[l], ssm_b_re[l], ssm_b_im[l], ssm_c_re[l],
                                 ssm_c_im[l], S5_CHUNK)
        lp["peer_u_pk"] = _pack_table_rows(peer_u[l])
        lp["peer_v_pk"] = _pack_table(peer_v[l])
        lp["w_in_p"] = _pack_w_in(w_in[l], lay, dm)
        lp["w_uk_p"] = jnp.pad(w_uk[l], ((0, 0), (0, 0), (0, LANES - nope))).reshape(c_lat, heads * LANES).astype(BF16)
        lp["w_uk_f"] = w_uk[l].reshape(c_lat, heads * nope).astype(BF16)
        lp["w_uv_f"] = w_uv[l].reshape(c_lat, heads * vd).astype(BF16)
        lp["w_glu_b"] = w_glu[l].astype(BF16)
        lp["w_br_b"] = w_br[l].astype(BF16)
        lp["w_o_b"] = w_o[l].astype(BF16)
        lp["w_pq_b"] = w_pq[l].astype(BF16)
        lp["peer_keys_b"] = peer_keys[l].astype(BF16)

        zeros_state = jnp.zeros((bp, n_groups, p_state), F32)
        xp, hc, hr, hi, ck, kr = _layer(xp, bp, lp_len, pos_p, lp, lay, True, None, zeros_state, zeros_state, None)
        outs["p_conv"].append(hc)
        outs["p_re"].append(hr)
        outs["p_im"].append(hi)
        outs["p_ckv"].append(ck.reshape(bp, lp_len // page, page, c_lat))
        outs["p_kr"].append(kr.reshape(bp, lp_len // page, page, rope))
        xs, hc, hr, hi, ck, kr = _layer(xs, bs, ls, pos_s, lp, lay, False, cache_conv[l], state_ssm_re[l],
                                        state_ssm_im[l], (cache_ckv, cache_krope, page_table, l))
        outs["s_conv"].append(hc)
        outs["s_re"].append(hr)
        outs["s_im"].append(hi)
        outs["s_ckv"].append(ck.reshape(bs, ls, c_lat))
        outs["s_kr"].append(kr.reshape(bs, ls, rope))
    st = {k: jnp.stack(v) for k, v in outs.items()}
    return (xp.reshape(bp, lp_len, dm), xs.reshape(bs, ls, dm),
            st["p_conv"], st["p_re"], st["p_im"], st["p_ckv"], st["p_kr"],
            st["s_conv"], st["s_re"], st["s_im"], st["s_ckv"], st["s_kr"])
```

```python
import functools
import math

import jax
import jax.numpy as jnp
from jax import lax
from jax.experimental import pallas as pl
from jax.experimental.pallas import tpu as pltpu

EPS = 1e-6
ROPE_THETA = 10000.0
NEG_INF = -1e30
PEER_TOPK = 16
S5_CHUNK = 16

LANES = 128
SUBLANES = 8
VMEM_LIMIT = 48 * 1024 * 1024
VMEM_LIMIT_TABLE = 56 * 1024 * 1024

F32 = jnp.float32
BF16 = jnp.bfloat16


def _tile(n, pref):
    if n <= pref:
        return n
    t = pref - pref % SUBLANES
    while t >= SUBLANES:
        if n % t == 0:
            return t
        t -= SUBLANES
    return n


def _cparams(sem, limit=VMEM_LIMIT):
    return pltpu.CompilerParams(dimension_semantics=sem, vmem_limit_bytes=limit)


def _proj_kernel(x_ref, g_ref, w_ref, z_ref, h_ref):
    @pl.when(pl.program_id(1) == 0)
    def _():
        x = x_ref[...]
        ms = jnp.mean(x * x, axis=-1, keepdims=True)
        h_ref[...] = (x * lax.rsqrt(ms + EPS) * g_ref[...]).astype(BF16)

    z_ref[...] = jnp.dot(h_ref[...], w_ref[...], preferred_element_type=F32)


def _proj(x2d, g, w_p):
    t, d = x2d.shape
    n = w_p.shape[1]
    tm = _tile(t, 512)
    nb = n // LANES
    k = max(c for c in range(1, nb + 1) if nb % c == 0 and c * LANES <= 2304)
    tn = k * LANES
    return pl.pallas_call(
        _proj_kernel,
        grid=(t // tm, n // tn),
        in_specs=[
            pl.BlockSpec((tm, d), lambda i, j: (i, 0)),
            pl.BlockSpec((1, d), lambda i, j: (0, 0)),
            pl.BlockSpec((d, tn), lambda i, j: (0, j)),
        ],
        out_specs=pl.BlockSpec((tm, tn), lambda i, j: (i, j)),
        out_shape=jax.ShapeDtypeStruct((t, n), F32),
        scratch_shapes=[pltpu.VMEM((tm, d), BF16)],
        compiler_params=_cparams(("parallel", "arbitrary")),
        name="proj",
    )(x2d, g.reshape(1, d), w_p)


def _norm_rope(x, gain, cos, sin, nope, rope):
    lane = lax.broadcasted_iota(jnp.int32, (1, LANES), 1)
    m_n = lane < nope
    m_r = (lane >= nope) & (lane < nope + rope)
    sq = x * x
    ss_r = jnp.sum(jnp.where(m_r, sq, 0.0), axis=-1, keepdims=True)
    inv_r = lax.rsqrt(ss_r / rope + EPS)
    if nope:
        ss_n = jnp.sum(jnp.where(m_n, sq, 0.0), axis=-1, keepdims=True)
        inv = jnp.where(m_n, lax.rsqrt(ss_n / nope + EPS), inv_r)
    else:
        inv = inv_r
    y = x * inv * gain
    half = rope // 2
    first = lane < nope + half
    partner = jnp.where(first, pltpu.roll(y, LANES - half, 1), pltpu.roll(y, half, 1))
    return y * cos + partner * sin


def _prep_kernel(*refs, heads, nope, rope, seq_len, tm, prompt):
    if prompt:
        (xc_ref, bc_ref, cc_ref, q_ref, ckv_ref, kr_ref, cos_ref, sin_ref, cw_ref, gq_ref, gkr_ref, gkv_ref,
         hxc_ref, hcc_ref, oc_ref, qo_ref, ckvo_ref, kro_ref, vt_ref) = refs
    else:
        (xc_ref, bc_ref, cc_ref, q_ref, ckv_ref, kr_ref, cos_ref, sin_ref, cw_ref, gq_ref, gkr_ref, gkv_ref,
         h1_ref, h2_ref, oc_ref, qo_ref, ckvo_ref, kro_ref, vt_ref) = refs
    i = pl.program_id(0)
    v = cc_ref[...] * xc_ref[...]
    row = lax.broadcasted_iota(jnp.int32, (tm, 1), 0)
    r1 = pltpu.roll(v, 1, 0)
    r2 = pltpu.roll(v, 2, 0)
    if prompt:
        hv = hcc_ref[...] * hxc_ref[...]
        hv = jnp.where((i % (seq_len // tm)) == 0, 0.0, hv)
        v1 = jnp.where(row == 0, hv[7:8, :], r1)
        v2 = jnp.where(row == 0, hv[6:7, :], jnp.where(row == 1, hv[7:8, :], r2))
        vt_ref[0] = v[tm - SUBLANES:, :]
    else:
        l = row % seq_len
        v1 = jnp.where(l >= 1, r1, h1_ref[...])
        v2 = jnp.where(l >= 2, r2, h2_ref[...])
        vt_ref[...] = v
    y = v2 * cw_ref[0:1, :] + v1 * cw_ref[1:2, :] + v * cw_ref[2:3, :]
    oc_ref[...] = (bc_ref[...] * y).astype(oc_ref.dtype)

    cos = cos_ref[...]
    sin = sin_ref[...]
    gq = gq_ref[...]
    for h in range(heads):
        sl = slice(LANES * h, LANES * (h + 1))
        qo_ref[:, sl] = _norm_rope(q_ref[:, sl], gq, cos, sin, nope, rope).astype(qo_ref.dtype)
    kro_ref[...] = _kr_norm_rope(kr_ref[...], gkr_ref[...], cos, sin, nope, rope)
    c = ckv_ref[...]
    ms = jnp.mean(c * c, axis=-1, keepdims=True)
    ckvo_ref[...] = c * lax.rsqrt(ms + EPS) * gkv_ref[...]


def _kr_norm_rope(x, gain, cos, sin, nope, rope):
    lane = lax.broadcasted_iota(jnp.int32, (1, LANES), 1)
    ss = jnp.sum(x * x, axis=-1, keepdims=True)
    y = x * lax.rsqrt(ss / rope + EPS) * gain
    half = rope // 2
    first = lane < nope + half
    partner = jnp.where(first, pltpu.roll(y, LANES - half, 1), pltpu.roll(y, half, 1))
    return y * cos + partner * sin


def _prep(z, lay, cos_t, sin_t, conv_w, gq, gkr, gkv, seq_len, prompt, hist1=None, hist2=None, q_dtype=BF16):
    t = z.shape[0]
    bw, heads, c_lat = lay["bw"], lay["heads"], lay["c_lat"]
    hq = heads * LANES
    if prompt:
        tm = _tile(seq_len, 512)
        assert seq_len % tm == 0 and tm >= 2 * SUBLANES
    else:
        tm = _tile(t, 512)
        assert tm % seq_len == 0
    ntab = cos_t.shape[0] // tm
    in_specs = [
        pl.BlockSpec((tm, bw), lambda i: (i, 0)),
        pl.BlockSpec((tm, bw), lambda i: (i, 1)),
        pl.BlockSpec((tm, bw), lambda i: (i, 2)),
        pl.BlockSpec((tm, hq), lambda i: (i, lay["off_q"] // hq)),
        pl.BlockSpec((tm, c_lat), lambda i: (i, lay["off_ckv"] // c_lat)),
        pl.BlockSpec((tm, LANES), lambda i: (i, lay["off_kr"] // LANES)),
        pl.BlockSpec((tm, LANES), lambda i: (i % ntab, 0)),
        pl.BlockSpec((tm, LANES), lambda i: (i % ntab, 0)),
        pl.BlockSpec(conv_w.shape, lambda i: (0, 0)),
        pl.BlockSpec((1, LANES), lambda i: (0, 0)),
        pl.BlockSpec((1, LANES), lambda i: (0, 0)),
        pl.BlockSpec((1, c_lat), lambda i: (0, 0)),
    ]
    args = [z, z, z, z, z, z, cos_t, sin_t, conv_w, gq, gkr, gkv]
    if prompt:
        rb = tm // SUBLANES
        in_specs += [
            pl.BlockSpec((SUBLANES, bw), lambda i: (jnp.maximum(i * rb - 1, 0), 0)),
            pl.BlockSpec((SUBLANES, bw), lambda i: (jnp.maximum(i * rb - 1, 0), 2)),
        ]
        args += [z, z]
        vt_spec = pl.BlockSpec((1, SUBLANES, bw), lambda i: (i, 0, 0))
        vt_shape = jax.ShapeDtypeStruct((t // tm, SUBLANES, bw), F32)
    else:
        in_specs += [pl.BlockSpec((tm, bw), lambda i: (i, 0)), pl.BlockSpec((tm, bw), lambda i: (i, 0))]
        args += [hist1, hist2]
        vt_spec = pl.BlockSpec((tm, bw), lambda i: (i, 0))
        vt_shape = jax.ShapeDtypeStruct((t, bw), F32)
    kern = functools.partial(_prep_kernel, heads=heads, nope=lay["nope"], rope=lay["rope"], seq_len=seq_len, tm=tm,
                             prompt=prompt)
    return pl.pallas_call(
        kern,
        grid=(t // tm,),
        in_specs=in_specs,
        out_specs=[
            pl.BlockSpec((tm, bw), lambda i: (i, 0)),
            pl.BlockSpec((tm, hq), lambda i: (i, 0)),
            pl.BlockSpec((tm, c_lat), lambda i: (i, 0)),
            pl.BlockSpec((tm, LANES), lambda i: (i, 0)),
            vt_spec,
        ],
        out_shape=[
            jax.ShapeDtypeStruct((t, bw), BF16),
            jax.ShapeDtypeStruct((t, hq), q_dtype),
            jax.ShapeDtypeStruct((t, c_lat), F32),
            jax.ShapeDtypeStruct((t, LANES), F32),
            vt_shape,
        ],
        compiler_params=_cparams(("parallel",)),
        name="prep_prompt" if prompt else "prep_sample",
    )(*args)


def _s5_mats(lam_re, lam_im, log_dt, b_re, b_im, c_re, c_im, lc):
    hp = lax.Precision.HIGHEST
    g, p, n_in = b_re.shape
    n_out = c_re.shape[1]
    dt = jnp.exp(log_dt)[:, None]
    lr, li = lam_re, lam_im
    mag = jnp.exp(lr * dt)
    a_re, a_im = mag * jnp.cos(li * dt), mag * jnp.sin(li * dt)
    den = lr * lr + li * li
    f_re = ((a_re - 1.0) * lr + a_im * li) / den
    f_im = (a_im * lr - (a_re - 1.0) * li) / den
    bb_re = f_re[..., None] * b_re - f_im[..., None] * b_im
    bb_im = f_re[..., None] * b_im + f_im[..., None] * b_re
    k = jnp.arange(lc + 1, dtype=F32)[:, None, None]
    pm = jnp.exp(lr[None] * dt[None] * k)
    pr = pm * jnp.cos(li[None] * dt[None] * k)
    pi = pm * jnp.sin(li[None] * dt[None] * k)
    ab_re = pr[:lc, ..., None] * bb_re[None] - pi[:lc, ..., None] * bb_im[None]
    ab_im = pr[:lc, ..., None] * bb_im[None] + pi[:lc, ..., None] * bb_re[None]
    ms = jnp.concatenate([ab_re[::-1], ab_im[::-1]], axis=2)
    ms = jnp.transpose(ms, (1, 0, 3, 2)).reshape(g, lc * n_in, 2 * p)
    kk = (jnp.einsum("gop,dgpi->dgoi", c_re, ab_re, precision=hp)
          - jnp.einsum("gop,dgpi->dgoi", c_im, ab_im, precision=hp))
    s_idx = jnp.arange(lc)[:, None]
    t_idx = jnp.arange(lc)[None, :]
    delta = t_idx - s_idx
    kt = jnp.where((delta >= 0)[:, :, None, None, None], kk[jnp.clip(delta, 0, lc - 1)], 0.0)
    tk = jnp.transpose(kt, (2, 0, 4, 1, 3)).reshape(g, lc * n_in, lc * n_out)
    ca_re = c_re[None] * pr[1:, :, None, :] - c_im[None] * pi[1:, :, None, :]
    ca_im = c_re[None] * pi[1:, :, None, :] + c_im[None] * pr[1:, :, None, :]
    gs = jnp.concatenate([ca_re, -ca_im], axis=3)
    gs = jnp.transpose(gs, (1, 3, 0, 2)).reshape(g, 2 * p, lc * n_out)
    return dict(ms=ms.astype(BF16), tk=tk.astype(BF16), gs=gs.astype(BF16), pr=pr, pi=pi, lc=lc, n_in=n_in, n_out=n_out)


def _s5_ops(m, lc):
    full, n_in, n_out = m["lc"], m["n_in"], m["n_out"]
    assert lc <= full
    ms = m["ms"][:, (full - lc) * n_in:, :]
    tk = m["tk"][:, :lc * n_in, :lc * n_out]
    gs = m["gs"][:, :, :lc * n_out]
    a1 = jnp.concatenate([m["pr"][lc], m["pr"][lc]], axis=-1)[:, None, :]
    a2 = jnp.concatenate([-m["pi"][lc], m["pi"][lc]], axis=-1)[:, None, :]
    return ms, tk, gs, a1, a2


def _s5_kernel(u_ref, tk_ref, ms_ref, gs_ref, a1_ref, a2_ref, h0_ref, y_ref, ht_ref, s_sc, sw_sc, hin_sc, *, n_blocks,
               bsz, spb, p_state):
    u = u_ref[...]
    s = jnp.dot(u, ms_ref[...], preferred_element_type=F32)
    s_sc[...] = s
    sw_sc[...] = pltpu.roll(s, p_state, 1)
    a1 = a1_ref[...]
    a2 = a2_ref[...]
    rb = spb * bsz

    def body(k, carry):
        h, hw = carry
        r = pl.multiple_of(k * rb, SUBLANES)
        s_blk = s_sc[pl.ds(r, rb), :]
        sw_blk = sw_sc[pl.ds(r, rb), :]
        hs = []
        for q in range(spb):
            hs.append(h)
            rows = slice(q * bsz, (q + 1) * bsz)
            h, hw = a1 * h + a2 * hw + s_blk[rows, :], a1 * hw - a2 * h + sw_blk[rows, :]
        hin_sc[pl.ds(r, rb), :] = hs[0] if spb == 1 else jnp.concatenate(hs, axis=0)
        return h, hw

    h0 = h0_ref[...]
    h, _ = lax.fori_loop(0, n_blocks, body, (h0, pltpu.roll(h0, p_state, 1)))
    ht_ref[...] = h
    y_ref[...] = (jnp.dot(u, tk_ref[...], preferred_element_type=F32)
                  + jnp.dot(hin_sc[...].astype(BF16), gs_ref[...], preferred_element_type=F32))


def _s5(us, h0_re, h0_im, mats, bsz, seq_len, lc):
    ms, tk, gs, a1, a2 = mats
    g, _, p2 = ms.shape
    p_state = p2 // 2
    n_in = ms.shape[1] // lc
    n_out = tk.shape[2] // lc
    n_chunks = seq_len // lc
    spb = max(1, SUBLANES // bsz)
    assert (spb * bsz) % SUBLANES == 0 and n_chunks % spb == 0
    nc = n_chunks * bsz
    bp = bsz
    u = us.astype(BF16).reshape(bsz, n_chunks, lc, g, n_in)
    u = jnp.transpose(u, (3, 1, 0, 2, 4)).reshape(g, nc, lc * n_in)
    h0 = jnp.transpose(jnp.concatenate([h0_re, h0_im], axis=-1), (1, 0, 2))
    kern = functools.partial(_s5_kernel, n_blocks=n_chunks // spb, bsz=bsz, spb=spb, p_state=p_state)
    y, ht = pl.pallas_call(
        kern,
        grid=(g,),
        in_specs=[
            pl.BlockSpec((None, nc, lc * n_in), lambda i: (i, 0, 0)),
            pl.BlockSpec((None, lc * n_in, lc * n_out), lambda i: (i, 0, 0)),
            pl.BlockSpec((None, lc * n_in, p2), lambda i: (i, 0, 0)),
            pl.BlockSpec((None, p2, lc * n_out), lambda i: (i, 0, 0)),
            pl.BlockSpec((None, 1, p2), lambda i: (i, 0, 0)),
            pl.BlockSpec((None, 1, p2), lambda i: (i, 0, 0)),
            pl.BlockSpec((None, bp, p2), lambda i: (i, 0, 0)),
        ],
        out_specs=[
            pl.BlockSpec((None, nc, lc * n_out), lambda i: (i, 0, 0)),
            pl.BlockSpec((None, bp, p2), lambda i: (i, 0, 0)),
        ],
        out_shape=[
            jax.ShapeDtypeStruct((g, nc, lc * n_out), F32),
            jax.ShapeDtypeStruct((g, bp, p2), F32),
        ],
        scratch_shapes=[pltpu.VMEM((nc, p2), F32), pltpu.VMEM((nc, p2), F32), pltpu.VMEM((nc, p2), F32)],
        compiler_params=_cparams(("parallel",)),
        name="s5",
    )(u, tk, ms, gs, a1, a2, h0)
    y = y.reshape(g, n_chunks, bp, lc, n_out)[:, :, :bsz]
    y = jnp.transpose(y, (2, 1, 3, 0, 4)).reshape(bsz * seq_len, g * n_out)
    ht = jnp.transpose(ht[:, :bsz], (1, 0, 2))
    return y, ht[..., :p_state], ht[..., p_state:]


def _kprep_kernel(ckv_ref, kr_ref, wuk_ref, wuv_ref, gkn_ref, k_ref, v_ref, *, heads, nope):
    c = ckv_ref[...].astype(BF16)
    kraw = jnp.dot(c, wuk_ref[...], preferred_element_type=F32)
    kr = kr_ref[...]
    gkn = gkn_ref[...]
    for h in range(heads):
        sl = slice(LANES * h, LANES * (h + 1))
        kh = kraw[:, sl]
        ss = jnp.sum(kh * kh, axis=-1, keepdims=True)
        k_ref[:, sl] = (kh * lax.rsqrt(ss / nope + EPS) * gkn + kr).astype(BF16)
    v_ref[...] = jnp.dot(c, wuv_ref[...], preferred_element_type=F32).astype(BF16)


def _kprep(ckv_n, kr_n, wuk_p, wuv, gkn, heads, nope):
    t, c_lat = ckv_n.shape
    tm = _tile(t, 512)
    hk = wuk_p.shape[1]
    hv = wuv.shape[1]
    return pl.pallas_call(
        functools.partial(_kprep_kernel, heads=heads, nope=nope),
        grid=(t // tm,),
        in_specs=[
            pl.BlockSpec((tm, c_lat), lambda i: (i, 0)),
            pl.BlockSpec((tm, LANES), lambda i: (i, 0)),
            pl.BlockSpec((c_lat, hk), lambda i: (0, 0)),
            pl.BlockSpec((c_lat, hv), lambda i: (0, 0)),
            pl.BlockSpec((1, LANES), lambda i: (0, 0)),
        ],
        out_specs=[pl.BlockSpec((tm, hk), lambda i: (i, 0)), pl.BlockSpec((tm, hv), lambda i: (i, 0))],
        out_shape=[jax.ShapeDtypeStruct((t, hk), BF16), jax.ShapeDtypeStruct((t, hv), BF16)],
        compiler_params=_cparams(("parallel",)),
        name="kprep",
    )(ckv_n, kr_n, wuk_p, wuv, gkn)


def _flash_kernel(ii_ref, jj_ref, q_ref, k_ref, v_ref, o_ref, m_sc, l_sc, acc_sc, *, c_exp, tq, hps, vd):
    i = ii_ref[pl.program_id(2)]
    j = jj_ref[pl.program_id(2)]
    nct = tq // LANES

    @pl.when(j == 0)
    def _():
        m_sc[...] = jnp.full(m_sc.shape, NEG_INF, F32)
        l_sc[...] = jnp.zeros(l_sc.shape, F32)
        acc_sc[...] = jnp.zeros(acc_sc.shape, F32)

    def step(diagonal):
        if diagonal:
            row = lax.broadcasted_iota(jnp.int32, (tq, tq), 0)
            col = lax.broadcasted_iota(jnp.int32, (tq, tq), 1)
            keep = col <= row
        for hh in range(hps):
            q = q_ref[:, LANES * hh:LANES * (hh + 1)]
            k = k_ref[:, LANES * hh:LANES * (hh + 1)]
            s = lax.dot_general(q, k, (((1,), (1,)), ((), ())), preferred_element_type=F32)
            if diagonal:
                s = jnp.where(keep, s, NEG_INF)
            m_prev = m_sc[hh]
            m_new = jnp.maximum(m_prev, jnp.max(s, axis=-1, keepdims=True))
            alpha = jnp.exp2((m_prev - m_new) * c_exp)
            ps = [jnp.exp2((s[:, LANES * c:LANES * (c + 1)] - m_new) * c_exp) for c in range(nct)]
            psum = ps[0]
            for c in range(1, nct):
                psum = psum + ps[c]
            l_sc[hh] = alpha * l_sc[hh] + jnp.sum(psum, axis=-1, keepdims=True)
            p = jnp.concatenate(ps, axis=1).astype(BF16)
            acc_sc[hh] = alpha[:, :vd] * acc_sc[hh] + jnp.dot(p, v_ref[:, vd * hh:vd * (hh + 1)],
                                                              preferred_element_type=F32)
            m_sc[hh] = m_new

    @pl.when(j < i)
    def _():
        step(False)

    @pl.when(j == i)
    def _():
        step(True)
        for hh in range(hps):
            o_ref[:, vd * hh:vd * (hh + 1)] = (acc_sc[hh] / l_sc[hh][:, :vd]).astype(o_ref.dtype)


def _flash(q, k, v, bsz, seq_len, heads, vd, scale):
    hps = LANES // vd
    assert heads % hps == 0
    tq = _tile(seq_len, 1024)
    nq = seq_len // tq
    t = bsz * seq_len
    assert tq % LANES == 0
    kern = functools.partial(_flash_kernel, c_exp=scale * math.log2(math.e), tq=tq, hps=hps, vd=vd)
    pairs = [(i, j) for i in range(nq) for j in range(i + 1)]
    ii = jnp.asarray([p[0] for p in pairs], jnp.int32)
    jj = jnp.asarray([p[1] for p in pairs], jnp.int32)
    grid_spec = pltpu.PrefetchScalarGridSpec(
        num_scalar_prefetch=2,
        grid=(bsz, heads // hps, len(pairs)),
        in_specs=[
            pl.BlockSpec((tq, hps * LANES), lambda b, h, s, ii, jj: (b * nq + ii[s], h)),
            pl.BlockSpec((tq, hps * LANES), lambda b, h, s, ii, jj: (b * nq + jj[s], h)),
            pl.BlockSpec((tq, LANES), lambda b, h, s, ii, jj: (b * nq + jj[s], h)),
        ],
        out_specs=pl.BlockSpec((tq, LANES), lambda b, h, s, ii, jj: (b * nq + ii[s], h)),
        scratch_shapes=[pltpu.VMEM((hps, tq, LANES), F32), pltpu.VMEM((hps, tq, LANES), F32),
                        pltpu.VMEM((hps, tq, vd), F32)],
    )
    return pl.pallas_call(
        kern,
        grid_spec=grid_spec,
        out_shape=jax.ShapeDtypeStruct((t, heads * vd), BF16),
        compiler_params=_cparams(("parallel", "parallel", "arbitrary")),
        name="flash",
    )(ii, jj, q, k, v)


def _paged_kernel(pt_ref, qbd_ref, qr_ref, cn_ref, rn_ref, *refs, pp, heads, nope, vd, lq, scale):
    del pt_ref
    c_pages = refs[:pp]
    r_pages = refs[pp:2 * pp]
    wuk_ref, wuv_ref, e_ref, gk_ref, o_ref, m_sc, l_sc, acc_sc, cb_sc, rb_sc = refs[2 * pp:]
    s_id = pl.program_id(1)
    rows = lq * heads
    page = c_pages[0].shape[0]

    @pl.when(s_id == 0)
    def _():
        m_sc[...] = jnp.full(m_sc.shape, NEG_INF, F32)
        l_sc[...] = jnp.zeros(l_sc.shape, F32)
        acc_sc[...] = jnp.zeros(acc_sc.shape, F32)

    qbd = (qbd_ref[...] * gk_ref[...]).astype(BF16)
    qr = qr_ref[...].astype(BF16)
    dn = (((1,), (1,)), ((), ()))
    qlat = lax.dot_general(qbd, wuk_ref[...], dn, preferred_element_type=F32).astype(BF16)

    def scores(cb, rb):
        kraw = jnp.dot(cb, wuk_ref[...], preferred_element_type=F32)
        ssq = lax.dot_general(e_ref[...], (kraw * kraw).astype(BF16), dn, preferred_element_type=F32)
        inv = lax.rsqrt(ssq / nope + EPS)
        inv = jnp.concatenate([inv] * lq, axis=0)
        sn = lax.dot_general(qlat, cb, dn, preferred_element_type=F32)
        sr = jnp.dot(qr, rb, preferred_element_type=F32)
        return (sn * inv + sr) * scale

    def update(cb, rb, causal):
        s = scores(cb, rb)
        if causal:
            nk = cb.shape[0]
            kk = lax.broadcasted_iota(jnp.int32, (rows, nk), 1)
            qq = lax.broadcasted_iota(jnp.int32, (rows, nk), 0) // heads
            s = jnp.where(kk <= qq, s, NEG_INF)
        m_prev = m_sc[...]
        m_new = jnp.maximum(m_prev, jnp.max(s, axis=-1, keepdims=True))
        alpha = jnp.exp(m_prev - m_new)
        p = jnp.exp(s - m_new)
        l_sc[...] = alpha * l_sc[...] + jnp.sum(p, axis=-1, keepdims=True)
        acc_sc[...] = alpha * acc_sc[...] + jnp.dot(p.astype(BF16), cb, preferred_element_type=F32)
        m_sc[...] = m_new

    for pg in range(pp):
        cb_sc[page * pg:page * (pg + 1), :] = c_pages[pg][...].astype(BF16)
        rb_sc[:, page * pg:page * (pg + 1)] = r_pages[pg][...].astype(BF16)
    update(cb_sc[...], rb_sc[...], False)

    @pl.when(s_id == pl.num_programs(1) - 1)
    def _():
        update(cn_ref[...].astype(BF16), rn_ref[...].astype(BF16), True)
        lat = (acc_sc[...] / l_sc[...]).astype(BF16)
        full = jnp.dot(lat, wuv_ref[...], preferred_element_type=F32)
        colh = lax.broadcasted_iota(jnp.int32, full.shape, 1) // vd
        rowh = lax.broadcasted_iota(jnp.int32, full.shape, 0) % heads
        full = jnp.where(colh == rowh, full, 0.0)
        o_ref[...] = jnp.sum(full.reshape(lq, heads, heads * vd), axis=1)


def _paged(page_table, qbd, qr, c_new, r_new, cache_ckv, cache_krope, layer, wuk, wuv, e_mat, gk, heads, nope, vd,
           scale):
    bs, rows, _ = qbd.shape
    lq = rows // heads
    n_pages = page_table.shape[1]
    page, c_lat = cache_ckv.shape[2], cache_ckv.shape[3]
    rope = cache_krope.shape[3]
    pp = math.gcd(n_pages, 32)
    kn = c_new.shape[1]
    cache_krope = jnp.swapaxes(cache_krope, 2, 3)
    r_new = jnp.swapaxes(r_new, 1, 2)

    def cmap(p):
        return lambda b, s, pt: (layer, pt[b, s * pp + p], 0, 0)

    in_specs = [
        pl.BlockSpec((None, rows, heads * nope), lambda b, s, pt: (b, 0, 0)),
        pl.BlockSpec((None, rows, rope), lambda b, s, pt: (b, 0, 0)),
        pl.BlockSpec((None, kn, c_lat), lambda b, s, pt: (b, 0, 0)),
        pl.BlockSpec((None, rope, kn), lambda b, s, pt: (b, 0, 0)),
    ]
    in_specs += [pl.BlockSpec((None, None, page, c_lat), cmap(p)) for p in range(pp)]
    in_specs += [pl.BlockSpec((None, None, rope, page), cmap(p)) for p in range(pp)]
    in_specs += [
        pl.BlockSpec(wuk.shape, lambda b, s, pt: (0, 0)),
        pl.BlockSpec(wuv.shape, lambda b, s, pt: (0, 0)),
        pl.BlockSpec(e_mat.shape, lambda b, s, pt: (0, 0)),
        pl.BlockSpec(gk.shape, lambda b, s, pt: (0, 0)),
    ]
    kern = functools.partial(_paged_kernel, pp=pp, heads=heads, nope=nope, vd=vd, lq=lq, scale=scale)
    grid_spec = pltpu.PrefetchScalarGridSpec(
        num_scalar_prefetch=1,
        grid=(bs, n_pages // pp),
        in_specs=in_specs,
        out_specs=pl.BlockSpec((None, lq, heads * vd), lambda b, s, pt: (b, 0, 0)),
        scratch_shapes=[pltpu.VMEM((rows, 1), F32), pltpu.VMEM((rows, 1), F32), pltpu.VMEM((rows, c_lat), F32),
                        pltpu.VMEM((pp * page, c_lat), BF16), pltpu.VMEM((rope, pp * page), BF16)],
    )
    return pl.pallas_call(
        kern,
        grid_spec=grid_spec,
        out_shape=jax.ShapeDtypeStruct((bs, lq, heads * vd), F32),
        compiler_params=_cparams(("parallel", "arbitrary")),
        name="paged",
    )(page_table, qbd, qr, c_new, r_new, *([cache_ckv] * pp), *([cache_krope] * pp), wuk, wuv, e_mat, gk)


def _merge_kernel(x_ref, oc_ref, ys_ref, us_ref, oa_ref, g0_ref, g1_ref, g2_ref, d_ref, wglu_ref, bglu_ref, wbr_ref,
                  wo_ref, gffn_ref, wpq_ref, x1_ref, h2_ref, qp_ref):
    y = ys_ref[...] + d_ref[...] * us_ref[...]
    zg = jax.nn.gelu(y)
    gl = jnp.dot(zg.astype(BF16), wglu_ref[...], preferred_element_type=F32) + bglu_ref[...]
    o_ssm = zg * jax.nn.sigmoid(gl)
    merged = jax.nn.sigmoid(g0_ref[...]) * jnp.dot(oc_ref[...], wbr_ref[0], preferred_element_type=F32)
    merged += jax.nn.sigmoid(g1_ref[...]) * jnp.dot(o_ssm.astype(BF16), wbr_ref[1], preferred_element_type=F32)
    merged += jax.nn.sigmoid(g2_ref[...]) * jnp.dot(oa_ref[...].astype(BF16), wbr_ref[2], preferred_element_type=F32)
    x1 = x_ref[...] + jnp.dot(merged.astype(BF16), wo_ref[...], preferred_element_type=F32)
    x1_ref[...] = x1
    ms = jnp.mean(x1 * x1, axis=-1, keepdims=True)
    h2 = x1 * lax.rsqrt(ms + EPS) * gffn_ref[...]
    h2_ref[...] = h2
    qp_ref[...] = jnp.dot(h2.astype(BF16), wpq_ref[...], preferred_element_type=F32)


def _merge(x2d, oc, ys, z, oa, lay, d, wglu, bglu, wbr, wo, gffn, wpq):
    t, dm = x2d.shape
    bw = lay["bw"]
    dq = wpq.shape[1]
    tm = _tile(t, 256)
    gi = lay["off_g"] // dm

    def full(a):
        return pl.BlockSpec(a.shape, lambda i, _n=a.ndim: (0,) * _n)

    return pl.pallas_call(
        _merge_kernel,
        grid=(t // tm,),
        in_specs=[
            pl.BlockSpec((tm, dm), lambda i: (i, 0)),
            pl.BlockSpec((tm, bw), lambda i: (i, 0)),
            pl.BlockSpec((tm, bw), lambda i: (i, 0)),
            pl.BlockSpec((tm, bw), lambda i: (i, 3)),
            pl.BlockSpec((tm, bw), lambda i: (i, 0)),
            pl.BlockSpec((tm, dm), lambda i: (i, gi)),
            pl.BlockSpec((tm, dm), lambda i: (i, gi + 1)),
            pl.BlockSpec((tm, dm), lambda i: (i, gi + 2)),
            full(d), full(wglu), full(bglu), full(wbr), full(wo), full(gffn), full(wpq),
        ],
        out_specs=[
            pl.BlockSpec((tm, dm), lambda i: (i, 0)),
            pl.BlockSpec((tm, dm), lambda i: (i, 0)),
            pl.BlockSpec((tm, dq), lambda i: (i, 0)),
        ],
        out_shape=[
            jax.ShapeDtypeStruct((t, dm), F32),
            jax.ShapeDtypeStruct((t, dm), F32),
            jax.ShapeDtypeStruct((t, dq), F32),
        ],
        compiler_params=_cparams(("parallel",)),
        name="merge",
    )(x2d, oc, ys, z, oa, z, z, z, d, wglu, bglu, wbr, wo, gffn, wpq)


def _topk_rows(x, k):
    n = x.shape[0]
    iota = lax.broadcasted_iota(jnp.int32, x.shape, 0)
    vals, idxs = [], []
    for it in range(k):
        m = jnp.max(x, axis=0, keepdims=True)
        am = jnp.min(jnp.where(x == m, iota, n), axis=0, keepdims=True)
        vals.append(m)
        idxs.append(am)
        if it + 1 < k:
            x = jnp.where(iota == am, -jnp.inf, x)
    return jnp.concatenate(vals, axis=0), jnp.concatenate(idxs, axis=0)


def _select_rows(table, sel, k):
    out = jnp.zeros(sel.shape, table.dtype)
    for r in range(k):
        out = jnp.where(sel == r, table[r:r + 1, :], out)
    return out


def _pair_candidates(v1, v2, k):
    chunks, meta, r0, i = [], [], 0, 0
    while i < k and k // (i + 1) >= 2:
        n = k // (i + 1)
        nr = -(-n // SUBLANES) * SUBLANES
        blk = v1[i:i + 1, :] + v2[0:nr, :]
        if n < nr:
            blk = jnp.where(lax.broadcasted_iota(jnp.int32, blk.shape, 0) < n, blk, -jnp.inf)
        chunks.append(blk)
        meta.append((r0, nr, i, None))
        r0 += nr
        i += 1
    assert (k - i) % SUBLANES == 0
    chunks.append(v1[i:k, :] + v2[0:1, :])
    meta.append((r0, k - i, None, i))
    return jnp.concatenate(chunks, axis=0), meta


def _route_kernel(qp_ref, keys_ref, e_ref, g_ref, *, heads, n_keys, dk, topk, row_mult):
    dn = (((1,), (1,)), ((), ()))
    for h in range(heads):
        sub = []
        for s in range(2):
            o = (2 * h + s) * dk
            qs = qp_ref[:, o:o + dk].astype(BF16)
            st = lax.dot_general(keys_ref[h, s], qs, dn, preferred_element_type=F32)
            sub.append(_topk_rows(st, topk))
        (v1, i1), (v2, i2) = sub
        cand, meta = _pair_candidates(v1, v2, topk)
        sc, ci = _topk_rows(cand, topk)
        ihi = jnp.zeros(ci.shape, jnp.int32)
        jlo = jnp.zeros(ci.shape, jnp.int32)
        for r0, nr, ic, i0 in meta:
            inr = (ci >= r0) & (ci < r0 + nr)
            if ic is None:
                ihi = jnp.where(inr, ci - r0 + i0, ihi)
            else:
                ihi = jnp.where(inr, ic, ihi)
                jlo = jnp.where(inr, ci - r0, jlo)
        e1 = _select_rows(i1, ihi, topk)
        e2 = _select_rows(i2, jlo, topk)
        ex = jnp.exp(sc - sc[0:1, :])
        e_ref[topk * h:topk * (h + 1), :] = (e1 * n_keys + e2) * row_mult
        g_ref[topk * h:topk * (h + 1), :] = ex / jnp.sum(ex, axis=0, keepdims=True)


def _route(qp, keys_bf16, row_mult):
    t, dq = qp.shape
    heads, _, n_keys, dk = keys_bf16.shape
    tm = _tile(t, 128)
    if t % LANES:
        tm = t
    npair = heads * PEER_TOPK
    kern = functools.partial(_route_kernel, heads=heads, n_keys=n_keys, dk=dk, topk=PEER_TOPK, row_mult=row_mult)
    return pl.pallas_call(
        kern,
        grid=(t // tm,),
        in_specs=[
            pl.BlockSpec((tm, dq), lambda i: (i, 0)),
            pl.BlockSpec(keys_bf16.shape, lambda i: (0, 0, 0, 0)),
        ],
        out_specs=[pl.BlockSpec((npair, tm), lambda i: (0, i)), pl.BlockSpec((npair, tm), lambda i: (0, i))],
        out_shape=[jax.ShapeDtypeStruct((npair, t), jnp.int32), jax.ShapeDtypeStruct((npair, t), F32)],
        compiler_params=_cparams(("parallel",)),
        name="route",
    )(qp, keys_bf16)


HI_MASK = 0xFFFF0000
_BUTTERFLY_ORDER = (0, 4, 2, 6, 1, 5, 3, 7)


def _pack_table(tab):
    e, d = tab.shape
    half = d // 2
    b = lax.bitcast_convert_type(tab.astype(BF16), jnp.uint16).astype(jnp.uint32)
    return (b[:, :half] | (b[:, half:] << 16)).reshape(e * (half // LANES), LANES)


def _pack_table_rows(tab):
    e, d = tab.shape
    b = lax.bitcast_convert_type(tab.astype(BF16), jnp.uint16).astype(jnp.uint32).reshape(e, d // (2 * LANES), 2, LANES)
    return (b[:, :, 0, :] | (b[:, :, 1, :] << 16)).reshape(e * (d // (2 * LANES)), LANES)


def _unpack(x):
    return pltpu.bitcast(x << 16, F32), pltpu.bitcast(x & jnp.uint32(HI_MASK), F32)


def _fold_sublanes(x, y, k, mask):
    return jnp.where(mask, x, pltpu.roll(y, k, 0)) + jnp.where(mask, pltpu.roll(x, SUBLANES - k, 0), y)


def _peer_u_kernel(e_ref, h_ref, tab_ref, a_ref, *, tb, npair, rows):
    sub = lax.broadcasted_iota(jnp.int32, (SUBLANES, LANES), 0)
    lane = lax.broadcasted_iota(jnp.int32, (SUBLANES, LANES), 1)
    m4 = sub < 4
    m2 = (sub % 4) < 2
    m1 = (sub % 2) == 0
    ng = npair // SUBLANES

    def body(t, carry):
        th = h_ref[pl.ds(t, 1), :].reshape(SUBLANES, LANES)
        out = jnp.zeros((SUBLANES, LANES), F32)
        for g in range(ng):
            ps = []
            for j in _BUTTERFLY_ORDER:
                idx = pl.multiple_of(e_ref[t, SUBLANES * g + j], rows)
                u_row = pltpu.bitcast(tab_ref[pl.ds(idx, rows), :], BF16).astype(F32)
                ps.append(u_row * th)
            v = [_fold_sublanes(ps[2 * k], ps[2 * k + 1], 4, m4) for k in range(4)]
            r = _fold_sublanes(_fold_sublanes(v[0], v[1], 2, m2), _fold_sublanes(v[2], v[3], 2, m2), 1, m1)
            out = jnp.where(lane == g, jnp.sum(r, axis=-1, keepdims=True), out)
        a_ref[t] = out[:, :ng]
        return carry

    lax.fori_loop(0, tb, body, 0, unroll=32)


def _peer_v_kernel(e_ref, w_ref, x_ref, tab_ref, o_ref, *, tb, npair, rows):
    def body(t, carry):
        lo = [jnp.zeros((rows, LANES), F32) for _ in range(2)]
        hi = [jnp.zeros((rows, LANES), F32) for _ in range(2)]
        for j in range(npair):
            idx = pl.multiple_of(e_ref[t, j], rows)
            w = w_ref[t, j]
            x_lo, x_hi = _unpack(tab_ref[pl.ds(idx, rows), :])
            lo[j % 2] = lo[j % 2] + w * x_lo
            hi[j % 2] = hi[j % 2] + w * x_hi
        delta = jnp.concatenate([lo[0] + lo[1], hi[0] + hi[1]], axis=0)
        x_t = x_ref[pl.ds(t, 1), :].reshape(2 * rows, LANES)
        o_ref[pl.ds(t, 1), :] = (x_t + delta).reshape(1, 2 * rows * LANES)
        return carry

    lax.fori_loop(0, tb, body, 0, unroll=2)


def _peer_w_kernel(a_ref, g_ref, w_ref):
    w_ref[...] = g_ref[...] * jax.nn.gelu(a_ref[...])


def _peer(x1, h2, e_pt, g_pt, u_pk, v_pk):
    t, dm = x1.shape
    npair = e_pt.shape[0]
    rows = dm // (2 * LANES)
    assert 2 * rows == SUBLANES and npair % SUBLANES == 0
    tb = _tile(t, 64)
    nblk = t // tb
    ng = npair // SUBLANES
    e_nat = jnp.transpose(e_pt)
    e_u = jnp.transpose(e_nat.reshape(t, SUBLANES, ng), (0, 2, 1)).reshape(nblk, tb, npair)
    e_v = e_nat.reshape(nblk, tb, npair)
    g_nat = jnp.transpose(g_pt)
    smem = functools.partial(pl.BlockSpec, memory_space=pltpu.SMEM)
    tab_spec = pl.BlockSpec(u_pk.shape, lambda i: (0, 0), pipeline_mode=pl.Buffered(1))
    tok_spec = pl.BlockSpec((tb, dm), lambda i: (i, 0))
    idx_spec = smem((None, tb, npair), lambda i: (i, 0, 0))
    a = pl.pallas_call(
        functools.partial(_peer_u_kernel, tb=tb, npair=npair, rows=rows),
        grid=(nblk,),
        in_specs=[idx_spec, tok_spec, tab_spec],
        out_specs=pl.BlockSpec((tb, SUBLANES, ng), lambda i: (i, 0, 0)),
        out_shape=jax.ShapeDtypeStruct((t, SUBLANES, ng), F32),
        compiler_params=_cparams(("arbitrary",), VMEM_LIMIT_TABLE),
        name="peer_u",
    )(e_u, h2, u_pk)
    tw = _tile(t, 512)
    w = pl.pallas_call(
        _peer_w_kernel,
        grid=(t // tw,),
        in_specs=[pl.BlockSpec((tw, npair), lambda i: (i, 0)), pl.BlockSpec((tw, npair), lambda i: (i, 0))],
        out_specs=pl.BlockSpec((tw, npair), lambda i: (i, 0)),
        out_shape=jax.ShapeDtypeStruct((t, npair), F32),
        compiler_params=_cparams(("parallel",)),
        name="peer_w",
    )(a.reshape(t, npair), g_nat)
    return pl.pallas_call(
        functools.partial(_peer_v_kernel, tb=tb, npair=npair, rows=rows),
        grid=(nblk,),
        in_specs=[idx_spec, idx_spec, tok_spec, tab_spec],
        out_specs=tok_spec,
        out_shape=jax.ShapeDtypeStruct((t, dm), F32),
        compiler_params=_cparams(("arbitrary",), VMEM_LIMIT_TABLE),
        name="peer_v",
    )(e_v, w.reshape(nblk, tb, npair), x1, v_pk)


def _layout(dm, bw, heads, nope, rope, c_lat):
    hq = heads * LANES
    lay = dict(bw=bw, heads=heads, nope=nope, rope=rope, c_lat=c_lat)
    lay["off_q"] = 4 * bw
    lay["off_g"] = lay["off_q"] + hq
    lay["off_ckv"] = lay["off_g"] + 3 * dm
    lay["off_kr"] = lay["off_ckv"] + c_lat
    lay["n"] = lay["off_kr"] + LANES
    assert nope + rope <= LANES and rope % 2 == 0
    assert lay["off_q"] % hq == 0 and lay["off_g"] % dm == 0 and lay["off_ckv"] % c_lat == 0
    assert bw % LANES == 0 and c_lat % LANES == 0
    return lay


def _pack_w_in(w_in, lay, dm):
    bw, heads, nope, rope, c_lat = lay["bw"], lay["heads"], lay["nope"], lay["rope"], lay["c_lat"]
    sizes = (bw, bw, bw, bw, heads * (nope + rope), c_lat, rope, 3 * dm)
    parts, off = [], 0
    for n in sizes:
        parts.append(w_in[:, off:off + n])
        off += n
    xc, bc, cc, us, q, ckv, kr, gates = parts
    q = jnp.pad(q.reshape(dm, heads, nope + rope), ((0, 0), (0, 0), (0, LANES - nope - rope))).reshape(dm, heads * LANES)
    kr = jnp.pad(kr, ((0, 0), (nope, LANES - nope - rope)))
    return jnp.concatenate([xc, bc, cc, us, q, gates, ckv, kr], axis=1).astype(BF16)


def _rope_tables(pos, nope, rope):
    half = rope // 2
    inv = ROPE_THETA ** (-jnp.arange(half, dtype=F32) / half)
    ang = pos.astype(F32)[:, None] * inv[None, :]
    cos, sin = jnp.cos(ang), jnp.sin(ang)
    n = pos.shape[0]
    ones = jnp.ones((n, nope), F32)
    tail = LANES - nope - rope
    cos_t = jnp.concatenate([ones, cos, cos, jnp.ones((n, tail), F32)], axis=1)
    sin_t = jnp.concatenate([jnp.zeros((n, nope), F32), -sin, sin, jnp.zeros((n, tail), F32)], axis=1)
    return cos_t, sin_t


def _lane_vec(parts):
    v = jnp.concatenate(parts)
    return jnp.pad(v, (0, LANES - v.shape[0])).reshape(1, LANES)


def _layer(x2d, bsz, seq_len, pos, lp, lay, prompt, conv_hist, h0_re, h0_im, sample_ctx):
    t, dm = x2d.shape
    bw, heads, nope, rope, c_lat = lay["bw"], lay["heads"], lay["nope"], lay["rope"], lay["c_lat"]
    vd = lp["w_uv"].shape[2]
    scale = (nope + rope) ** -0.5

    z = _proj(x2d, lp["g_mix"], lp["w_in_p"])

    cos_t, sin_t = _rope_tables(pos, nope, rope)
    gq = _lane_vec([lp["g_qn"], lp["g_qr"]])
    gkr = _lane_vec([jnp.zeros((nope,), F32), lp["g_kr"]])
    gkv = lp["g_kv"].reshape(1, c_lat)
    if prompt:
        oc, qn, ckv_n, kr_n, vt = _prep(z, lay, cos_t, sin_t, lp["conv_w"], gq, gkr, gkv, seq_len, True)
        nblk_seq = seq_len // (t // vt.shape[0])
        new_hist = vt.reshape(bsz, nblk_seq, SUBLANES, bw)[:, -1, SUBLANES - 2:, :]
    else:
        tm = _tile(t, 512)
        reps = tm // seq_len
        cos_t = jnp.tile(cos_t, (reps, 1))
        sin_t = jnp.tile(sin_t, (reps, 1))
        zeros = jnp.zeros((bsz, seq_len - 1, bw), F32)
        hist1 = jnp.concatenate([conv_hist[:, 1:2], zeros], axis=1).reshape(t, bw)
        hist2 = jnp.concatenate([conv_hist, zeros[:, 1:]], axis=1).reshape(t, bw)
        oc, qn, ckv_n, kr_n, vt = _prep(z, lay, cos_t, sin_t, lp["conv_w"], gq, gkr, gkv, seq_len, False, hist1, hist2,
                                        q_dtype=F32)
        new_hist = vt.reshape(bsz, seq_len, bw)[:, seq_len - 2:, :]

    lc = math.gcd(seq_len, S5_CHUNK)
    us = z[:, 3 * bw:4 * bw]
    ys, ht_re, ht_im = _s5(us, h0_re, h0_im, _s5_ops(lp["s5_mats"], lc), bsz, seq_len, lc)

    gkn = _lane_vec([lp["g_kn"]])
    if prompt:
        k_full, v_all = _kprep(ckv_n, kr_n, lp["w_uk_p"], lp["w_uv_f"], gkn, heads, nope)
        oa = _flash(qn, k_full, v_all, bsz, seq_len, heads, vd, scale)
    else:
        cache_ckv, cache_krope, page_table, layer = sample_ctx
        q4 = qn.reshape(bsz, seq_len, heads, LANES)
        eye = jnp.eye(heads, dtype=F32)
        qbd = (q4[..., :nope][:, :, :, None, :] * eye[None, None, :, :, None]).reshape(bsz, seq_len * heads, heads * nope)
        qr = q4[..., nope:nope + rope].reshape(bsz, seq_len * heads, rope)
        kn = -(-seq_len // 16) * 16
        c_new = jnp.pad(ckv_n.reshape(bsz, seq_len, c_lat), ((0, 0), (0, kn - seq_len), (0, 0)))
        r_new = jnp.pad(kr_n[:, nope:nope + rope].reshape(bsz, seq_len, rope), ((0, 0), (0, kn - seq_len), (0, 0)))
        e_mat = jnp.repeat(jnp.eye(heads, dtype=F32), nope, axis=1).astype(BF16)
        gk = jnp.tile(lp["g_kn"], heads).reshape(1, heads * nope)
        oa = _paged(page_table, qbd, qr, c_new, r_new, cache_ckv, cache_krope, layer, lp["w_uk_f"], lp["w_uv_f"], e_mat,
                    gk, heads, nope, vd, scale).reshape(t, heads * vd)

    x1, h2, qp = _merge(x2d, oc, ys, z, oa, lay, lp["ssm_d"].reshape(1, bw), lp["w_glu_b"], lp["b_glu"].reshape(1, bw),
                        lp["w_br_b"], lp["w_o_b"], lp["g_ffn"].reshape(1, dm), lp["w_pq_b"])
    e_pt, g_pt = _route(qp, lp["peer_keys_b"], dm // (2 * LANES))
    x2 = _peer(x1, h2, e_pt, g_pt, lp["peer_u_pk"], lp["peer_v_pk"])
    return x2, new_hist, ht_re, ht_im, ckv_n, kr_n[:, nope:nope + rope]


def kernel(x_prompt, x_sample, cache_conv, state_ssm_re, state_ssm_im, cache_ckv, cache_krope, page_table, g_mix, w_in, conv_w, ssm_lam_re, ssm_lam_im, ssm_log_dt, ssm_b_re, ssm_b_im, ssm_c_re, ssm_c_im, ssm_d, w_glu, b_glu, g_kv, w_uk, w_uv, g_qn, g_kn, g_qr, g_kr, w_br, w_o, g_ffn, w_pq, peer_keys, peer_u, peer_v):
    bp, lp_len, dm = x_prompt.shape
    bs, ls, _ = x_sample.shape
    depth = w_in.shape[0]
    bw = conv_w.shape[2]
    c_lat, heads, nope = w_uk.shape[1], w_uk.shape[2], w_uk.shape[3]
    vd = w_uv.shape[3]
    rope = cache_krope.shape[3]
    page = cache_ckv.shape[2]
    past = page_table.shape[1] * page
    n_groups, p_state = ssm_lam_re.shape[1], ssm_lam_re.shape[2]
    assert ssm_d.shape[1] == bw and lp_len % page == 0
    lay = _layout(dm, bw, heads, nope, rope, c_lat)

    pos_p = jnp.arange(lp_len, dtype=jnp.int32)
    pos_s = past + jnp.arange(ls, dtype=jnp.int32)
    xp = x_prompt.reshape(bp * lp_len, dm)
    xs = x_sample.reshape(bs * ls, dm)
    outs = {k: [] for k in ("p_conv", "p_re", "p_im", "p_ckv", "p_kr", "s_conv", "s_re", "s_im", "s_ckv", "s_kr")}
    for l in range(depth):
        lp = dict(
            g_mix=g_mix[l], conv_w=conv_w[l], ssm_lam_re=ssm_lam_re[l], ssm_lam_im=ssm_lam_im[l],
            ssm_log_dt=ssm_log_dt[l], ssm_b_re=ssm_b_re[l], ssm_b_im=ssm_b_im[l], ssm_c_re=ssm_c_re[l],
            ssm_c_im=ssm_c_im[l], ssm_d=ssm_d[l], b_glu=b_glu[l], g_kv=g_kv[l], g_qn=g_qn[l], g_kn=g_kn[l],
            g_qr=g_qr[l], g_kr=g_kr[l], g_ffn=g_ffn[l], w_uv=w_uv[l],
        )
        lp["s5_mats"] = _s5_mats(ssm_lam_re[l], ssm_lam_im[l], ssm_log_dt[l], ssm_b_re[l], ssm_b_im[l], ssm_c_re[l],
                                 ssm_c_im[l], S5_CHUNK)
        lp["peer_u_pk"] = _pack_table_rows(peer_u[l])
        lp["peer_v_pk"] = _pack_table(peer_v[l])
        lp["w_in_p"] = _pack_w_in(w_in[l], lay, dm)
        lp["w_uk_p"] = jnp.pad(w_uk[l], ((0, 0), (0, 0), (0, LANES - nope))).reshape(c_lat, heads * LANES).astype(BF16)
        lp["w_uk_f"] = w_uk[l].reshape(c_lat, heads * nope).astype(BF16)
        lp["w_uv_f"] = w_uv[l].reshape(c_lat, heads * vd).astype(BF16)
        lp["w_glu_b"] = w_glu[l].astype(BF16)
        lp["w_br_b"] = w_br[l].astype(BF16)
        lp["w_o_b"] = w_o[l].astype(BF16)
        lp["w_pq_b"] = w_pq[l].astype(BF16)
        lp["peer_keys_b"] = peer_keys[l].astype(BF16)

        zeros_state = jnp.zeros((bp, n_groups, p_state), F32)
        xp, hc, hr, hi, ck, kr = _layer(xp, bp, lp_len, pos_p, lp, lay, True, None, zeros_state, zeros_state, None)
        outs["p_conv"].append(hc)
        outs["p_re"].append(hr)
        outs["p_im"].append(hi)
        outs["p_ckv"].append(ck.reshape(bp, lp_len // page, page, c_lat))
        outs["p_kr"].append(kr.reshape(bp, lp_len // page, page, rope))
        xs, hc, hr, hi, ck, kr = _layer(xs, bs, ls, pos_s, lp, lay, False, cache_conv[l], state_ssm_re[l],
                                        state_ssm_im[l], (cache_ckv, cache_krope, page_table, l))
        outs["s_conv"].append(hc)
        outs["s_re"].append(hr)
        outs["s_im"].append(hi)
        outs["s_ckv"].append(ck.reshape(bs, ls, c_lat))
        outs["s_kr"].append(kr.reshape(bs, ls, rope))
    st = {k: jnp.stack(v) for k, v in outs.items()}
    return (xp.reshape(bp, lp_len, dm), xs.reshape(bs, ls, dm),
            st["p_conv"], st["p_re"], st["p_im"], st["p_ckv"], st["p_kr"],
            st["s_conv"], st["s_re"], st["s_im"], st["s_ckv"], st["s_kr"])
```
